```python
import math
import jax, jax.numpy as jnp
from jax import lax
import numpy as np

D_MODEL = 2048
BATCH = 1
SEQ = 8192
DEPTH = 4

GRID_W = 64
CTX_LEN = 256
POS_BASE = 10000.0
N_MIXERS = 3
N_MOD = 6
EPS = 1e-6
S5_GROUP = 16
S5_GROUPS = D_MODEL // S5_GROUP
S5_STATE = 64
S5_DT_MIN = 1e-3
S5_DT_MAX = 1e-1
SG_CHUNK = 128
SG_HALF = D_MODEL
SG_HEADS = 8
SG_HEAD_DIM = SG_HALF // SG_HEADS
CONV_WIDTH = 31
D_FF = 5632
N_EXPERTS = 8
TOP_K = 2
D_FF_EXPERT = 5632
N_S5 = (DEPTH + 2) // 3
N_SG = (DEPTH + 1) // 3
N_CV = DEPTH // 3
N_DENSE = (DEPTH + 1) // 2
N_MOE = DEPTH // 2

kernel_name = "hybrid_s5_gmlp_conformer_moe_dit"


def rms_norm(h, g):
    hf = h.astype(jnp.float32)
    hf = hf * lax.rsqrt(jnp.mean(hf * hf, axis=-1, keepdims=True) + EPS)
    return (hf * g.astype(jnp.float32)).astype(h.dtype)


def layer_norm(h, g, b):
    hf = h.astype(jnp.float32)
    mu = jnp.mean(hf, axis=-1, keepdims=True)
    var = jnp.mean(jnp.square(hf - mu), axis=-1, keepdims=True)
    out = (hf - mu) * lax.rsqrt(var + EPS) * g.astype(jnp.float32) + b.astype(jnp.float32)
    return out.astype(h.dtype)


def modulate(h, shift, scale):
    return h * (1 + scale) + shift


def grid_sincos(rows, dim):
    t = jnp.arange(rows * GRID_W)
    row = (t // GRID_W).astype(jnp.float32)
    col = (t % GRID_W).astype(jnp.float32)
    quarter = dim // 4
    omega = 1.0 / (POS_BASE ** (jnp.arange(quarter, dtype=jnp.float32) / quarter))

    def emb(p):
        ang = p[:, None] * omega[None, :]
        return jnp.concatenate([jnp.sin(ang), jnp.cos(ang)], axis=-1)

    return jnp.concatenate([emb(row), emb(col)], axis=-1)


def s5_discretise(lam_re, lam_im, log_dt):
    lam_re = jnp.minimum(lam_re.astype(jnp.float32), -1e-4)
    lam_im = lam_im.astype(jnp.float32)
    dt = jnp.exp(log_dt.astype(jnp.float32))[:, None]
    mag = jnp.exp(lam_re * dt)
    ab_re = mag * jnp.cos(lam_im * dt)
    ab_im = mag * jnp.sin(lam_im * dt)
    x_re = ab_re - 1.0
    x_im = ab_im
    den = lam_re * lam_re + lam_im * lam_im
    k_re = (x_re * lam_re + x_im * lam_im) / den
    k_im = (x_im * lam_re - x_re * lam_im) / den
    return ab_re, ab_im, k_re, k_im


def _linear_recurrence_combine(e1, e2):
    a1r, a1i, b1r, b1i = e1
    a2r, a2i, b2r, b2i = e2
    ar = a1r * a2r - a1i * a2i
    ai = a1r * a2i + a1i * a2r
    br = a2r * b1r - a2i * b1i + b2r
    bi = a2r * b1i + a2i * b1r + b2i
    return ar, ai, br, bi


def s5_scan(bu_re, bu_im, ab_re, ab_im, s0, reverse):
    if s0 is not None:
        s0_re, s0_im = s0
        edge = -1 if reverse else 0
        bu_re = bu_re.at[:, edge].add(ab_re * s0_re - ab_im * s0_im)
        bu_im = bu_im.at[:, edge].add(ab_re * s0_im + ab_im * s0_re)
    a_re = jnp.broadcast_to(ab_re, bu_re.shape)
    a_im = jnp.broadcast_to(ab_im, bu_im.shape)
    _, _, s_re, s_im = lax.associative_scan(
        _linear_recurrence_combine, (a_re, a_im, bu_re, bu_im), reverse=reverse, axis=1)
    return s_re, s_im


def s5_mixer(hc, hx, w_in, lam_re, lam_im, log_dt, b_re, b_im, c_re, c_im, d_skip, w_out, with_ctx):
    f32 = jnp.float32

    def project(h):
        u = (h @ w_in).astype(f32)
        return u.reshape(h.shape[0], h.shape[1], S5_GROUPS, S5_GROUP)

    uc, ux = project(hc), project(hx)
    d_g = d_skip.astype(f32).reshape(S5_GROUPS, S5_GROUP)
    yx = d_g * ux
    yc = d_g * uc if with_ctx else None
    for direction, reverse in enumerate((False, True)):
        ab_re, ab_im, k_re, k_im = s5_discretise(lam_re[direction], lam_im[direction], log_dt[direction])
        br = b_re[direction].astype(f32)
        bi = b_im[direction].astype(f32)
        bbar_re = k_re[..., None] * br - k_im[..., None] * bi
        bbar_im = k_re[..., None] * bi + k_im[..., None] * br
        cr = c_re[direction].astype(f32)
        ci = c_im[direction].astype(f32)

        def drive(u):
            return (jnp.einsum('blgh,gph->blgp', u, bbar_re),
                    jnp.einsum('blgh,gph->blgp', u, bbar_im))

        def readout(s_re, s_im):
            return jnp.einsum('blgp,ghp->blgh', s_re, cr) - jnp.einsum('blgp,ghp->blgh', s_im, ci)

        sc_re, sc_im = s5_scan(*drive(uc), ab_re, ab_im, None, reverse)
        edge = 0 if reverse else -1
        sx_re, sx_im = s5_scan(*drive(ux), ab_re, ab_im, (sc_re[:, edge], sc_im[:, edge]), reverse)
        yx = yx + readout(sx_re, sx_im)
        if with_ctx:
            yc = yc + readout(sc_re, sc_im)

    def glu_out(y, like):
        y = jax.nn.gelu(y.reshape(y.shape[0], y.shape[1], D_MODEL)).astype(like.dtype)
        a, g = jnp.split(y @ w_out, 2, axis=-1)
        return a * jax.nn.sigmoid(g)

    return (glu_out(yc, hc) if with_ctx else None), glu_out(yx, hx)


def sg_mixer(hc, hx, w_in, ln_g, ln_b, w_s, b_s, w_out, with_ctx):
    def run(h):
        bsz, length, _ = h.shape
        z = jax.nn.gelu(h @ w_in)
        u, v = z[..., :SG_HALF], z[..., SG_HALF:]
        v = layer_norm(v, ln_g, ln_b)
        v = v.reshape(bsz, length // SG_CHUNK, SG_CHUNK, SG_HEADS, SG_HEAD_DIM)
        sv = jnp.einsum('hpq,bcqhd->bcphd', w_s, v) + b_s.T[:, :, None]
        return (u * sv.reshape(bsz, length, SG_HALF)) @ w_out

    return (run(hc) if with_ctx else None), run(hx)


def conv_mixer(hc, hx, w_pw1, dw_w, dw_b, ln_g, ln_b, w_pw2, with_ctx):
    half = CONV_WIDTH // 2

    def run(h):
        a, g = jnp.split(h @ w_pw1, 2, axis=-1)
        z = a * jax.nn.sigmoid(g)
        z = lax.conv_general_dilated(
            z, dw_w[:, None, :], window_strides=(1,), padding=[(half, half)],
            dimension_numbers=('NWC', 'WIO', 'NWC'), feature_group_count=D_MODEL) + dw_b
        z = layer_norm(z, ln_g, ln_b)
        return jax.nn.silu(z) @ w_pw2

    return (run(hc) if with_ctx else None), run(hx)


def swiglu(h, w_gate, w_up, w_down):
    return (jax.nn.silu(h @ w_gate) * (h @ w_up)) @ w_down


def moe_swiglu(h, w_router, w_gate, w_up, w_down):
    logits = (h @ w_router).astype(jnp.float32)
    top_val, top_idx = lax.top_k(logits, TOP_K)
    top_w = jax.nn.softmax(top_val, axis=-1)
    gates = jnp.sum(jax.nn.one_hot(top_idx, N_EXPERTS, dtype=jnp.float32) * top_w[..., None], axis=-2)
    gates = gates.astype(h.dtype)
    out = jnp.zeros_like(h)
    for e in range(N_EXPERTS):
        out = out + gates[..., e:e + 1] * swiglu(h, w_gate[e], w_up[e], w_down[e])
    return out


def setup_inputs(seed: int = 0) -> dict:
    key = jax.random.key(seed)
    keys = iter(jax.random.split(key, 48))

    def nrm(shape, scale):
        return jax.random.normal(next(keys), shape, jnp.float32) * scale

    def gain(shape):
        return 1.0 + nrm(shape, 0.01)

    D, G, P, H = D_MODEL, S5_GROUPS, S5_STATE, S5_GROUP
    inputs = {}
    inputs['x'] = nrm((BATCH, SEQ, D), 1.0)
    inputs['c'] = nrm((BATCH, D), 1.0)
    inputs['ctx'] = nrm((BATCH, CTX_LEN, D), 1.0)
    inputs['c_ctx'] = nrm((D,), 1.0)
    inputs['ada_w'] = nrm((DEPTH, D, N_MOD * D), 0.5 * D ** -0.5)
    inputs['ada_b'] = nrm((DEPTH, N_MOD * D), 0.01)
    inputs['norm_mix_g'] = gain((DEPTH, D))
    inputs['norm_ffn_g'] = gain((DEPTH, D))
    inputs['final_g'] = gain((D,))
    inputs['s5_w_in'] = nrm((N_S5, D, D), D ** -0.5)
    inputs['s5_lam_re'] = -0.5 + nrm((N_S5, 2, G, P), 0.01)
    inputs['s5_lam_im'] = math.pi * jnp.arange(P, dtype=jnp.float32) + nrm((N_S5, 2, G, P), 0.01)
    inputs['s5_log_dt'] = jax.random.uniform(next(keys), (N_S5, 2, G), jnp.float32,
                                             minval=math.log(S5_DT_MIN), maxval=math.log(S5_DT_MAX))
    inputs['s5_b_re'] = nrm((N_S5, 2, G, P, H), (2 * H) ** -0.5)
    inputs['s5_b_im'] = nrm((N_S5, 2, G, P, H), (2 * H) ** -0.5)
    inputs['s5_c_re'] = nrm((N_S5, 2, G, H, P), (2 * P) ** -0.5)
    inputs['s5_c_im'] = nrm((N_S5, 2, G, H, P), (2 * P) ** -0.5)
    inputs['s5_d'] = nrm((N_S5, D), 1.0)
    inputs['s5_w_out'] = nrm((N_S5, D, 2 * D), D ** -0.5)
    inputs['sg_w_in'] = nrm((N_SG, D, 2 * SG_HALF), D ** -0.5)
    inputs['sg_ln_g'] = gain((N_SG, SG_HALF))
    inputs['sg_ln_b'] = nrm((N_SG, SG_HALF), 0.01)
    inputs['sg_w_s'] = nrm((N_SG, SG_HEADS, SG_CHUNK, SG_CHUNK), SG_CHUNK ** -0.5)
    inputs['sg_b_s'] = gain((N_SG, SG_HEADS, SG_CHUNK))
    inputs['sg_w_out'] = nrm((N_SG, SG_HALF, D), SG_HALF ** -0.5)
    inputs['cv_w_pw1'] = nrm((N_CV, D, 2 * D), D ** -0.5)
    inputs['cv_dw_w'] = nrm((N_CV, CONV_WIDTH, D), CONV_WIDTH ** -0.5)
    inputs['cv_dw_b'] = nrm((N_CV, D), 0.01)
    inputs['cv_ln_g'] = gain((N_CV, D))
    inputs['cv_ln_b'] = nrm((N_CV, D), 0.01)
    inputs['cv_w_pw2'] = nrm((N_CV, D, D), D ** -0.5)
    inputs['ff_w_gate'] = nrm((N_DENSE, D, D_FF), D ** -0.5)
    inputs['ff_w_up'] = nrm((N_DENSE, D, D_FF), D ** -0.5)
    inputs['ff_w_down'] = nrm((N_DENSE, D_FF, D), D_FF ** -0.5)
    inputs['moe_w_router'] = nrm((N_MOE, D, N_EXPERTS), D ** -0.5)
    inputs['moe_w_gate'] = nrm((N_MOE, N_EXPERTS, D, D_FF_EXPERT), D ** -0.5)
    inputs['moe_w_up'] = nrm((N_MOE, N_EXPERTS, D, D_FF_EXPERT), D ** -0.5)
    inputs['moe_w_down'] = nrm((N_MOE, N_EXPERTS, D_FF_EXPERT, D), D_FF_EXPERT ** -0.5)
    return inputs


def reference(x, c, ctx, c_ctx, ada_w, ada_b, norm_mix_g, norm_ffn_g, final_g,
              s5_w_in, s5_lam_re, s5_lam_im, s5_log_dt, s5_b_re, s5_b_im, s5_c_re, s5_c_im, s5_d, s5_w_out,
              sg_w_in, sg_ln_g, sg_ln_b, sg_w_s, sg_b_s, sg_w_out,
              cv_w_pw1, cv_dw_w, cv_dw_b, cv_ln_g, cv_ln_b, cv_w_pw2,
              ff_w_gate, ff_w_up, ff_w_down,
              moe_w_router, moe_w_gate, moe_w_up, moe_w_down):
    bsz, length, dim = x.shape
    rows = length // GRID_W
    x = x + grid_sincos(rows, dim).astype(x.dtype)[None]
    h_ctx = ctx
    cond_x = jax.nn.silu(c)
    cond_c = jax.nn.silu(c_ctx)
    for i in range(DEPTH):
        last = i == DEPTH - 1
        kind = i % N_MIXERS
        j = i // N_MIXERS
        mod_x = (cond_x @ ada_w[i] + ada_b[i])[:, None, :]
        mod_c = (cond_c @ ada_w[i] + ada_b[i])[None, None, :]
        sh1x, sc1x, g1x, sh2x, sc2x, g2x = jnp.split(mod_x, N_MOD, axis=-1)
        sh1c, sc1c, g1c, sh2c, sc2c, g2c = jnp.split(mod_c, N_MOD, axis=-1)

        hx = modulate(rms_norm(x, norm_mix_g[i]), sh1x, sc1x)
        hc = modulate(rms_norm(h_ctx, norm_mix_g[i]), sh1c, sc1c)
        if kind == 0:
            yc, yx = s5_mixer(hc, hx, s5_w_in[j], s5_lam_re[j], s5_lam_im[j], s5_log_dt[j],
                              s5_b_re[j], s5_b_im[j], s5_c_re[j], s5_c_im[j], s5_d[j], s5_w_out[j],
                              with_ctx=not last)
        elif kind == 1:
            yc, yx = sg_mixer(hc, hx, sg_w_in[j], sg_ln_g[j], sg_ln_b[j], sg_w_s[j], sg_b_s[j],
                              sg_w_out[j], with_ctx=not last)
        else:
            yc, yx = conv_mixer(hc, hx, cv_w_pw1[j], cv_dw_w[j], cv_dw_b[j], cv_ln_g[j], cv_ln_b[j],
                                cv_w_pw2[j], with_ctx=not last)
        x = x + g1x * yx
        if not last:
            h_ctx = h_ctx + g1c * yc

        k = i // 2
        if i % 2 == 0:
            ffn = lambda h: swiglu(h, ff_w_gate[k], ff_w_up[k], ff_w_down[k])
        else:
            ffn = lambda h: moe_swiglu(h, moe_w_router[k], moe_w_gate[k], moe_w_up[k], moe_w_down[k])
        x = x + g2x * ffn(modulate(rms_norm(x, norm_ffn_g[i]), sh2x, sc2x))
        if not last:
            h_ctx = h_ctx + g2c * ffn(modulate(rms_norm(h_ctx, norm_ffn_g[i]), sh2c, sc2c))
    return rms_norm(x, final_g)
```

```python
import functools
import math

import jax
import jax.numpy as jnp
from jax import lax
from jax.experimental import pallas as pl
from jax.experimental.pallas import tpu as pltpu

F32 = jnp.float32
BF16 = jnp.bfloat16
HIGHEST = lax.Precision.HIGHEST

D_MODEL = 2048
DEPTH = 4
GRID_W = 64
CTX_LEN = 256
POS_BASE = 10000.0
N_MIXERS = 3
N_MOD = 6
EPS = 1e-6
S5_GROUP = 16
S5_GROUPS = D_MODEL // S5_GROUP
S5_STATE = 64
S5_CHUNK = 16
S5_CW = S5_CHUNK * S5_GROUP
SG_CHUNK = 128
SG_HEADS = 8
SG_HEAD_DIM = D_MODEL // SG_HEADS
CONV_WIDTH = 31
CONV_HALO = 16
N_EXPERTS = 8
LANES = 128
VMEM_LIMIT = 56 * 1024 * 1024


def _pick(n, candidates):
    for c in candidates:
        if n % c == 0:
            return c
    raise ValueError(f"no tile in {candidates} divides {n}")


def _params(sem):
    return pltpu.CompilerParams(dimension_semantics=sem, vmem_limit_bytes=VMEM_LIMIT)


def _sigmoid(x):
    return 1.0 / (1.0 + jnp.exp(-x))


def _silu(x):
    return x * _sigmoid(x)


def _gelu_tanh(x):
    return 0.5 * x * (1.0 + jnp.tanh(math.sqrt(2.0 / math.pi) * (x + 0.044715 * (x * x * x))))


def _is_ctx(i, tm):
    rows = i * tm + lax.broadcasted_iota(jnp.int32, (tm, 1), 0)
    return rows < CTX_LEN


def _mod_row(mod_ref, k, is_ctx):
    return jnp.where(is_ctx, mod_ref[1, k:k + 1, :], mod_ref[0, k:k + 1, :])


def _norm_mod(i, tm, h_ref, g_ref, mod_ref, k_shift):
    h = h_ref[...]
    inv = lax.rsqrt(jnp.mean(h * h, axis=-1, keepdims=True) + EPS)
    hn = (h * inv) * g_ref[...]
    is_ctx = _is_ctx(i, tm)
    return hn * (1.0 + _mod_row(mod_ref, k_shift + 1, is_ctx)) + _mod_row(mod_ref, k_shift, is_ctx)


def _layer_norm(v, g, b):
    mu = jnp.mean(v, axis=-1, keepdims=True)
    vc = v - mu
    var = jnp.mean(vc * vc, axis=-1, keepdims=True)
    return vc * lax.rsqrt(var + EPS) * g + b


def _fused_matmul(*, name, n_rows, k_dim, n_out, tm, tn, pro_args, pro_specs, pro_fn,
                  w, w_layer, w_col_offsets, epi_args, epi_specs, epi_fn, out_dtype):
    n_pro, n_w, n_epi = len(pro_args), len(w_col_offsets), len(epi_args)
    nj = n_out // tn

    def body(*refs):
        pro_refs = refs[:n_pro]
        w_refs = refs[n_pro:n_pro + n_w]
        epi_refs = refs[n_pro + n_w:n_pro + n_w + n_epi]
        out_ref, a_ref = refs[-2], refs[-1]
        i = pl.program_id(0)
        j = pl.program_id(1)

        @pl.when(j == 0)
        def _():
            a_ref[...] = pro_fn(i, *pro_refs).astype(BF16)

        a = a_ref[...]
        ys = [jnp.dot(a, wr[...].astype(BF16), preferred_element_type=F32) for wr in w_refs]
        out_ref[...] = epi_fn(i, j, ys, *epi_refs).astype(out_dtype)

    w_specs = [
        pl.BlockSpec((None, k_dim, tn), functools.partial(lambda i, j, o: (w_layer, 0, j + o), o=off // tn))
        for off in w_col_offsets
    ]
    return pl.pallas_call(
        body,
        grid=(n_rows // tm, nj),
        in_specs=list(pro_specs) + w_specs + list(epi_specs),
        out_specs=pl.BlockSpec((tm, tn), lambda i, j: (i, j)),
        out_shape=jax.ShapeDtypeStruct((n_rows, n_out), out_dtype),
        scratch_shapes=[pltpu.VMEM((tm, k_dim), BF16)],
        compiler_params=_params(("arbitrary", "arbitrary")),
        name=name,
    )(*pro_args, *([w] * n_w), *epi_args)


def _norm_specs(layer, tm):
    return [
        pl.BlockSpec((tm, D_MODEL), lambda i, j: (i, 0)),
        pl.BlockSpec((None, 1, D_MODEL), lambda i, j: (layer, 0, 0)),
        pl.BlockSpec((None, 2, N_MOD, D_MODEL), lambda i, j: (layer, 0, 0, 0)),
    ]


def _res_specs(layer, tm, tn):
    return [
        pl.BlockSpec((tm, tn), lambda i, j: (i, j)),
        pl.BlockSpec((None, 2, N_MOD, tn), lambda i, j: (layer, 0, 0, j)),
    ]


def _gated_residual(i, tm, k_gate, y, h_ref, mod_ref):
    return h_ref[...] + _mod_row(mod_ref, k_gate, _is_ctx(i, tm)) * y


def _embed(x2, ctx2, row_tab, col_tab):
    seq = x2.shape[0]
    tm = 4 * GRID_W
    half = D_MODEL // 2
    n_tiles = (CTX_LEN + seq) // tm
    rows_per_tile = tm // GRID_W
    row_tab3 = row_tab.reshape(seq // tm, rows_per_tile, half)

    def body(ctx_ref, x_ref, row_ref, col_ref, out_ref):
        i = pl.program_id(0)

        @pl.when(i == 0)
        def _():
            out_ref[...] = ctx_ref[...]

        @pl.when(i > 0)
        def _():
            rt = row_ref[0]
            row_part = jnp.concatenate(
                [jnp.broadcast_to(rt[r:r + 1, :], (GRID_W, half)) for r in range(rows_per_tile)], axis=0)
            col_part = jnp.concatenate([col_ref[...]] * rows_per_tile, axis=0)
            out_ref[:, :half] = x_ref[:, :half] + row_part
            out_ref[:, half:] = x_ref[:, half:] + col_part

    assert CTX_LEN == tm
    return pl.pallas_call(
        body,
        grid=(n_tiles,),
        in_specs=[
            pl.BlockSpec((tm, D_MODEL), lambda i: (0, 0)),
            pl.BlockSpec((tm, D_MODEL), lambda i: (jnp.maximum(i - 1, 0), 0)),
            pl.BlockSpec((1, rows_per_tile, half), lambda i: (jnp.maximum(i - 1, 0), 0, 0)),
            pl.BlockSpec((GRID_W, half), lambda i: (0, 0)),
        ],
        out_specs=pl.BlockSpec((tm, D_MODEL), lambda i: (i, 0)),
        out_shape=jax.ShapeDtypeStruct((CTX_LEN + seq, D_MODEL), F32),
        compiler_params=_params(("arbitrary",)),
        name="embed",
    )(ctx2, x2, row_tab3, col_tab)


def _modulation(cond8, ada_w, ada_b):
    n = N_MOD * D_MODEL
    tn = 1024

    def body(c_ref, w_ref, b_ref, out_ref):
        a = _silu(c_ref[...]).astype(BF16)
        out_ref[...] = jnp.dot(a, w_ref[...].astype(BF16), preferred_element_type=F32) + b_ref[...]

    return pl.pallas_call(
        body,
        grid=(DEPTH, n // tn),
        in_specs=[
            pl.BlockSpec((8, D_MODEL), lambda l, j: (0, 0)),
            pl.BlockSpec((None, D_MODEL, tn), lambda l, j: (l, 0, j)),
            pl.BlockSpec((None, 1, tn), lambda l, j: (l, 0, j)),
        ],
        out_specs=pl.BlockSpec((None, 8, tn), lambda l, j: (l, 0, j)),
        out_shape=jax.ShapeDtypeStruct((DEPTH, 8, n), F32),
        compiler_params=_params(("arbitrary", "arbitrary")),
        name="modulation",
    )(cond8, ada_w, ada_b.reshape(DEPTH, 1, n))


def _s5_operators(lam_re, lam_im, log_dt, bt_re, bt_im, c_re, c_im):
    G, T, H, P, CW = S5_GROUPS, S5_CHUNK, S5_GROUP, S5_STATE, S5_CW

    def body(lre_ref, lim_ref, ldt_ref, btr_ref, bti_ref, cr_ref, ci_ref,
             mt_ref, we_ref, wof_ref, wob_ref, at_ref):
        lane_blk = lax.broadcasted_iota(jnp.int32, (CW, CW), 1) // H
        rep = (lax.broadcasted_iota(jnp.int32, (H, CW), 1) % H
               == lax.broadcasted_iota(jnp.int32, (H, CW), 0)).astype(F32)
        zeros64 = jnp.zeros((CW, P), F32)
        mt = jnp.zeros((CW, CW), F32)
        we_parts, wo_parts, at_parts = [], [], []
        for d in range(2):
            lr = jnp.minimum(lre_ref[d, 0], -1e-4)
            li = lim_ref[d, 0]
            dt = jnp.exp(ldt_ref[d, 0])
            mag = jnp.exp(lr * dt)
            ar = mag * jnp.cos(li * dt)
            ai = mag * jnp.sin(li * dt)
            xr = ar - 1.0
            xi = ai
            den = lr * lr + li * li
            kr = (xr * lr + xi * li) / den
            ki = (xi * lr - xr * li) / den
            btr, bti = btr_ref[d, 0], bti_ref[d, 0]
            bbr = kr * btr - ki * bti
            bbi = kr * bti + ki * btr
            cr, ci = cr_ref[d, 0], ci_ref[d, 0]
            pr = [jnp.ones((1, P), F32)]
            pi = [jnp.zeros((1, P), F32)]
            for _ in range(T):
                pr.append(pr[-1] * ar - pi[-1] * ai)
                pi.append(pr[-2] * ai + pi[-1] * ar)
            ba_re = [bbr * pr[l] - bbi * pi[l] for l in range(T)]
            ba_im = [bbr * pi[l] + bbi * pr[l] for l in range(T)]
            order = list(range(T - 1, -1, -1)) if d == 0 else list(range(T))
            e_re = jnp.concatenate([ba_re[l] for l in order], axis=0)
            e_im = jnp.concatenate([ba_im[l] for l in order], axis=0)
            cmat = jnp.concatenate([cr, -ci], axis=1)
            kn = lax.dot_general(jnp.concatenate([e_re, e_im], axis=1), cmat,
                                 (((1,), (1,)), ((), ())), precision=HIGHEST,
                                 preferred_element_type=F32)
            kw = jnp.dot(kn, rep, precision=HIGHEST, preferred_element_type=F32)
            for t in range(T):
                if d == 0:
                    sh = H * (T - 1 - t)
                    shifted = kw if sh == 0 else jnp.concatenate(
                        [kw[sh:, :], jnp.zeros((sh, CW), F32)], axis=0)
                else:
                    sh = H * t
                    shifted = kw if sh == 0 else jnp.concatenate(
                        [jnp.zeros((sh, CW), F32), kw[:CW - sh, :]], axis=0)
                mt = mt + jnp.where(lane_blk == t, shifted, 0.0)
            lags = [t + 1 for t in range(T)] if d == 0 else [T - t for t in range(T)]
            ca_re = jnp.concatenate([cr * pr[l] - ci * pi[l] for l in lags], axis=0)
            ca_im = jnp.concatenate([cr * pi[l] + ci * pr[l] for l in lags], axis=0)
            we_parts.append((e_re, e_im))
            wo_parts.append((ca_re, -ca_im))
            at_parts.append((pr[T], pi[T]))
        mt_ref[0] = mt.astype(BF16)
        we_ref[0] = jnp.concatenate(
            [we_parts[0][0], we_parts[1][0], we_parts[0][1], we_parts[1][1]], axis=1).astype(BF16)
        wof_ref[0] = jnp.concatenate(
            [wo_parts[0][0], zeros64, wo_parts[0][1], zeros64], axis=1).astype(BF16)
        wob_ref[0] = jnp.concatenate(
            [zeros64, wo_parts[1][0], zeros64, wo_parts[1][1]], axis=1).astype(BF16)
        at_ref[0] = jnp.concatenate(
            [at_parts[0][0], at_parts[1][0], at_parts[0][1], at_parts[1][1]], axis=1)

    vec_spec = pl.BlockSpec((2, 1, 1, P), lambda g: (0, g, 0, 0))
    mat_spec = pl.BlockSpec((2, 1, H, P), lambda g: (0, g, 0, 0))
    op_spec = pl.BlockSpec((1, CW, CW), lambda g: (g, 0, 0))
    op_shape = jax.ShapeDtypeStruct((G, CW, CW), BF16)
    return pl.pallas_call(
        body,
        grid=(G,),
        in_specs=[vec_spec, vec_spec, pl.BlockSpec((2, 1, 1, 1), lambda g: (0, g, 0, 0)),
                  mat_spec, mat_spec, mat_spec, mat_spec],
        out_specs=[op_spec, op_spec, op_spec, op_spec, pl.BlockSpec((1, 1, CW), lambda g: (g, 0, 0))],
        out_shape=[op_shape, op_shape, op_shape, op_shape, jax.ShapeDtypeStruct((G, 1, CW), F32)],
        compiler_params=_params(("arbitrary",)),
        name="s5_operators",
    )(lam_re.reshape(2, G, 1, P), lam_im.reshape(2, G, 1, P), log_dt.reshape(2, G, 1, 1),
      bt_re, bt_im, c_re, c_im)


def _s5_local_state(u_r, we):
    G, nc, cw = u_r.shape

    def body(u_ref, we_ref, out_ref):
        out_ref[0] = jnp.dot(u_ref[0].astype(BF16), we_ref[0], preferred_element_type=F32)

    return pl.pallas_call(
        body,
        grid=(G,),
        in_specs=[pl.BlockSpec((1, nc, cw), lambda g: (g, 0, 0)),
                  pl.BlockSpec((1, cw, cw), lambda g: (g, 0, 0))],
        out_specs=pl.BlockSpec((1, nc, cw), lambda g: (g, 0, 0)),
        out_shape=jax.ShapeDtypeStruct((G, nc, cw), F32),
        compiler_params=_params(("arbitrary",)),
        name="s5_local_state",
    )(u_r, we)


def _s5_carry(sloc_t, at):
    nc, G, cw = sloc_t.shape
    cb = CTX_LEN // S5_CHUNK
    nb = nc // cb
    half = cw // 2

    def rev_block(s):
        return jnp.where(s == 0, 0, nb - s)

    def body(sf_ref, sb_ref, at_ref, pf_ref, pb_ref, st_ref):
        s = pl.program_id(0)

        @pl.when(s == 0)
        def _():
            st_ref[...] = jnp.zeros_like(st_ref)

        is_fwd = (lax.broadcasted_iota(jnp.int32, (G, half), 1) < S5_STATE)
        a_re, a_im = at_ref[:, :half], at_ref[:, half:]
        s_re, s_im = st_ref[:, :half], st_ref[:, half:]
        for k in range(cb):
            kb = cb - 1 - k
            state = jnp.concatenate([s_re, s_im], axis=1).astype(BF16)
            pf_ref[k] = state
            pb_ref[kb] = state
            l_re = jnp.where(is_fwd, sf_ref[k, :, :half], sb_ref[kb, :, :half])
            l_im = jnp.where(is_fwd, sf_ref[k, :, half:], sb_ref[kb, :, half:])
            n_re = a_re * s_re - a_im * s_im + l_re
            n_im = a_re * s_im + a_im * s_re + l_im
            s_re, s_im = n_re, n_im
        st_ref[:, :half] = s_re
        st_ref[:, half:] = s_im

    blk = (cb, G, cw)
    return pl.pallas_call(
        body,
        grid=(nb,),
        in_specs=[pl.BlockSpec(blk, lambda s: (s, 0, 0)),
                  pl.BlockSpec(blk, lambda s: (rev_block(s), 0, 0)),
                  pl.BlockSpec((G, cw), lambda s: (0, 0))],
        out_specs=[pl.BlockSpec(blk, lambda s: (s, 0, 0)),
                   pl.BlockSpec(blk, lambda s: (rev_block(s), 0, 0))],
        out_shape=[jax.ShapeDtypeStruct((nc, G, cw), BF16)] * 2,
        scratch_shapes=[pltpu.VMEM((G, cw), F32)],
        compiler_params=_params(("arbitrary",)),
        name="s5_carry",
    )(sloc_t, sloc_t, at)


def _s5_output(u_r, mt, pf, pb, wof, wob, d_tile):
    G, nc, cw = u_r.shape
    nt = (((1,), (1,)), ((), ()))

    def body(u_ref, mt_ref, pf_ref, pb_ref, wof_ref, wob_ref, d_ref, y_ref):
        u = u_ref[0]
        y = jnp.dot(u.astype(BF16), mt_ref[0], preferred_element_type=F32)
        y = y + lax.dot_general(pf_ref[0], wof_ref[0], nt, preferred_element_type=F32)
        y = y + lax.dot_general(pb_ref[0], wob_ref[0], nt, preferred_element_type=F32)
        y_ref[0] = y + d_ref[0] * u

    act = pl.BlockSpec((1, nc, cw), lambda g: (g, 0, 0))
    op = pl.BlockSpec((1, cw, cw), lambda g: (g, 0, 0))
    return pl.pallas_call(
        body,
        grid=(G,),
        in_specs=[act, op, act, act, op, op, pl.BlockSpec((1, 1, cw), lambda g: (g, 0, 0))],
        out_specs=act,
        out_shape=jax.ShapeDtypeStruct((G, nc, cw), F32),
        compiler_params=_params(("arbitrary",)),
        name="s5_output",
    )(u_r, mt, pf, pb, wof, wob, d_tile)


def _s5_layer(h, layer, j, mod, norm_g, p):
    n_rows = h.shape[0]
    G, T, H = S5_GROUPS, S5_CHUNK, S5_GROUP
    nc = n_rows // T
    tm = _pick(n_rows, (768, 256))
    tn = 512
    u = _fused_matmul(
        name="s5_in", n_rows=n_rows, k_dim=D_MODEL, n_out=D_MODEL, tm=tm, tn=tn,
        pro_args=(h, norm_g, mod), pro_specs=_norm_specs(layer, tm),
        pro_fn=lambda i, h_ref, g_ref, m_ref: _norm_mod(i, tm, h_ref, g_ref, m_ref, 0),
        w=p['s5_w_in'], w_layer=j, w_col_offsets=(0,),
        epi_args=(), epi_specs=(), epi_fn=lambda i, jj, ys: ys[0], out_dtype=F32)
    u_r = u.reshape(nc, T, G, H).transpose(2, 0, 1, 3).reshape(G, nc, T * H)
    mt, we, wof, wob, at = _s5_operators(
        p['s5_lam_re'][j], p['s5_lam_im'][j], p['s5_log_dt'][j],
        jnp.swapaxes(p['s5_b_re'][j], -1, -2), jnp.swapaxes(p['s5_b_im'][j], -1, -2),
        p['s5_c_re'][j], p['s5_c_im'][j])
    sloc = _s5_local_state(u_r, we)
    pf, pb = _s5_carry(sloc.transpose(1, 0, 2), at.reshape(G, T * H))
    d_tile = jnp.tile(p['s5_d'][j].reshape(G, 1, H), (1, 1, T))
    y_r = _s5_output(u_r, mt, pf.transpose(1, 0, 2), pb.transpose(1, 0, 2), wof, wob, d_tile)
    y = y_r.reshape(G, nc, T, H).transpose(1, 2, 0, 3).reshape(n_rows, D_MODEL)
    return _fused_matmul(
        name="s5_out", n_rows=n_rows, k_dim=D_MODEL, n_out=D_MODEL, tm=tm, tn=tn,
        pro_args=(y,), pro_specs=[pl.BlockSpec((tm, D_MODEL), lambda i, jj: (i, 0))],
        pro_fn=lambda i, y_ref: _gelu_tanh(y_ref[...]),
        w=p['s5_w_out'], w_layer=j, w_col_offsets=(0, D_MODEL),
        epi_args=(h, mod), epi_specs=_res_specs(layer, tm, tn),
        epi_fn=lambda i, jj, ys, h_ref, m_ref: _gated_residual(
            i, tm, 2, ys[0] * _sigmoid(ys[1]), h_ref, m_ref),
        out_dtype=F32)


def _sg_layer(h, layer, j, mod, norm_g, p):
    n_rows = h.shape[0]
    tm = _pick(n_rows, (768, 256))
    tn = 512
    z = _fused_matmul(
        name="sg_in", n_rows=n_rows, k_dim=D_MODEL, n_out=2 * D_MODEL, tm=tm, tn=tn,
        pro_args=(h, norm_g, mod), pro_specs=_norm_specs(layer, tm),
        pro_fn=lambda i, h_ref, g_ref, m_ref: _norm_mod(i, tm, h_ref, g_ref, m_ref, 0),
        w=p['sg_w_in'], w_layer=j, w_col_offsets=(0,),
        epi_args=(), epi_specs=(), epi_fn=lambda i, jj, ys: _gelu_tanh(ys[0]), out_dtype=BF16)

    def gate(i, u_ref, v_ref, g_ref, b_ref, ws_ref, bs_ref):
        v = _layer_norm(v_ref[...].astype(F32), g_ref[...], b_ref[...]).astype(BF16)
        rows = []
        for c in range(tm // SG_CHUNK):
            heads = []
            for hd in range(SG_HEADS):
                vc = v[c * SG_CHUNK:(c + 1) * SG_CHUNK, hd * SG_HEAD_DIM:(hd + 1) * SG_HEAD_DIM]
                sv = jnp.dot(ws_ref[hd].astype(BF16), vc, preferred_element_type=F32)
                heads.append(sv + bs_ref[:, hd:hd + 1])
            rows.append(jnp.concatenate(heads, axis=1))
        return u_ref[...].astype(F32) * jnp.concatenate(rows, axis=0)

    return _fused_matmul(
        name="sg_out", n_rows=n_rows, k_dim=D_MODEL, n_out=D_MODEL, tm=tm, tn=tn,
        pro_args=(z, z, p['sg_ln_g'], p['sg_ln_b'], p['sg_w_s'], jnp.swapaxes(p['sg_b_s'], -1, -2)),
        pro_specs=[
            pl.BlockSpec((tm, D_MODEL), lambda i, jj: (i, 0)),
            pl.BlockSpec((tm, D_MODEL), lambda i, jj: (i, 1)),
            pl.BlockSpec((1, D_MODEL), lambda i, jj: (j, 0)),
            pl.BlockSpec((1, D_MODEL), lambda i, jj: (j, 0)),
            pl.BlockSpec((None, SG_HEADS, SG_CHUNK, SG_CHUNK), lambda i, jj: (j, 0, 0, 0)),
            pl.BlockSpec((None, SG_CHUNK, SG_HEADS), lambda i, jj: (j, 0, 0)),
        ],
        pro_fn=gate,
        w=p['sg_w_out'], w_layer=j, w_col_offsets=(0,),
        epi_args=(h, mod), epi_specs=_res_specs(layer, tm, tn),
        epi_fn=lambda i, jj, ys, h_ref, m_ref: _gated_residual(i, tm, 2, ys[0], h_ref, m_ref),
        out_dtype=F32)


def _conv_norm(z, dw_w, dw_b, ln_g, ln_b, j):
    n_rows = z.shape[0]
    tm = CTX_LEN
    n_tiles = n_rows // tm
    hb = tm // CONV_HALO
    half = CONV_WIDTH // 2
    rb, cbw = 64, LANES

    def body(prev_ref, cur_ref, next_ref, w_ref, b_ref, g_ref, beta_ref, out_ref, zp_ref, acc_ref):
        i = pl.program_id(0)
        prev_ok = i >= 2
        next_ok = jnp.logical_and(i >= 1, i < n_tiles - 1)
        zp_ref[0:CONV_HALO, :] = jnp.where(prev_ok, prev_ref[...], 0.0)
        zp_ref[CONV_HALO:CONV_HALO + tm, :] = cur_ref[...]
        zp_ref[CONV_HALO + tm:, :] = jnp.where(next_ok, next_ref[...], 0.0)

        def col_block(cb, carry):
            c0 = pl.multiple_of(cb * cbw, cbw)
            w = w_ref[:, pl.ds(c0, cbw)]
            for r in range(tm // rb):
                acc = jnp.zeros((rb, cbw), F32)
                for k in range(CONV_WIDTH):
                    r0 = r * rb + CONV_HALO - half + k
                    acc = acc + zp_ref[r0:r0 + rb, pl.ds(c0, cbw)] * w[k:k + 1, :]
                acc_ref[r * rb:(r + 1) * rb, pl.ds(c0, cbw)] = acc
            return carry

        lax.fori_loop(0, D_MODEL // cbw, col_block, 0)
        y = _layer_norm(acc_ref[...] + b_ref[...], g_ref[...], beta_ref[...])
        out_ref[...] = _silu(y).astype(BF16)

    vec = pl.BlockSpec((1, D_MODEL), lambda i: (j, 0))
    return pl.pallas_call(
        body,
        grid=(n_tiles,),
        in_specs=[
            pl.BlockSpec((CONV_HALO, D_MODEL), lambda i: (jnp.maximum(i * hb - 1, 0), 0)),
            pl.BlockSpec((tm, D_MODEL), lambda i: (i, 0)),
            pl.BlockSpec((CONV_HALO, D_MODEL), lambda i: (jnp.minimum((i + 1) * hb, n_tiles * hb - 1), 0)),
            pl.BlockSpec((None, CONV_WIDTH, D_MODEL), lambda i: (j, 0, 0)),
            vec, vec, vec,
        ],
        out_specs=pl.BlockSpec((tm, D_MODEL), lambda i: (i, 0)),
        out_shape=jax.ShapeDtypeStruct((n_rows, D_MODEL), BF16),
        scratch_shapes=[pltpu.VMEM((tm + 2 * CONV_HALO, D_MODEL), F32), pltpu.VMEM((tm, D_MODEL), F32)],
        compiler_params=_params(("arbitrary",)),
        name="conv_norm",
    )(z, z, z, dw_w, dw_b, ln_g, ln_b)


def _conv_layer(h, layer, j, mod, norm_g, p):
    n_rows = h.shape[0]
    tm = _pick(n_rows, (768, 256))
    tn = 512
    z = _fused_matmul(
        name="cv_pw1", n_rows=n_rows, k_dim=D_MODEL, n_out=D_MODEL, tm=tm, tn=tn,
        pro_args=(h, norm_g, mod), pro_specs=_norm_specs(layer, tm),
        pro_fn=lambda i, h_ref, g_ref, m_ref: _norm_mod(i, tm, h_ref, g_ref, m_ref, 0),
        w=p['cv_w_pw1'], w_layer=j, w_col_offsets=(0, D_MODEL),
        epi_args=(), epi_specs=(), epi_fn=lambda i, jj, ys: ys[0] * _sigmoid(ys[1]), out_dtype=F32)
    zc = _conv_norm(z, p['cv_dw_w'], p['cv_dw_b'], p['cv_ln_g'], p['cv_ln_b'], j)
    return _fused_matmul(
        name="cv_pw2", n_rows=n_rows, k_dim=D_MODEL, n_out=D_MODEL, tm=tm, tn=tn,
        pro_args=(zc,), pro_specs=[pl.BlockSpec((tm, D_MODEL), lambda i, jj: (i, 0))],
        pro_fn=lambda i, z_ref: z_ref[...],
        w=p['cv_w_pw2'], w_layer=j, w_col_offsets=(0,),
        epi_args=(h, mod), epi_specs=_res_specs(layer, tm, tn),
        epi_fn=lambda i, jj, ys, h_ref, m_ref: _gated_residual(i, tm, 2, ys[0], h_ref, m_ref),
        out_dtype=F32)


def _router(h, layer, k, mod, norm_g, w_router):
    n_rows = h.shape[0]
    tm = _pick(n_rows, (768, 256))
    wr = jnp.pad(w_router[k], ((0, 0), (0, LANES - N_EXPERTS)))

    def body(h_ref, g_ref, m_ref, wr_ref, out_ref):
        i = pl.program_id(0)
        hn = _norm_mod(i, tm, h_ref, g_ref, m_ref, 3)
        logits = jnp.dot(hn, wr_ref[...], precision=HIGHEST, preferred_element_type=F32)
        lane = lax.broadcasted_iota(jnp.int32, (tm, LANES), 1).astype(F32)
        neg = jnp.float32(-jnp.inf)
        lg = jnp.where(lane < N_EXPERTS, logits, neg)
        m1 = jnp.max(lg, axis=1, keepdims=True)
        i1 = jnp.min(jnp.where(lg == m1, lane, float(LANES)), axis=1, keepdims=True)
        lg2 = jnp.where(lane == i1, neg, lg)
        m2 = jnp.max(lg2, axis=1, keepdims=True)
        i2 = jnp.min(jnp.where(lg2 == m2, lane, float(LANES)), axis=1, keepdims=True)
        e = jnp.exp(m2 - m1)
        w1 = 1.0 / (1.0 + e)
        w2 = e / (1.0 + e)
        out_ref[...] = jnp.where(lane == i1, w1, 0.0) + jnp.where(lane == i2, w2, 0.0)

    return pl.pallas_call(
        body,
        grid=(n_rows // tm,),
        in_specs=[
            pl.BlockSpec((tm, D_MODEL), lambda i: (i, 0)),
            pl.BlockSpec((None, 1, D_MODEL), lambda i: (layer, 0, 0)),
            pl.BlockSpec((None, 2, N_MOD, D_MODEL), lambda i: (layer, 0, 0, 0)),
            pl.BlockSpec((D_MODEL, LANES), lambda i: (0, 0)),
        ],
        out_specs=pl.BlockSpec((tm, LANES), lambda i: (i, 0)),
        out_shape=jax.ShapeDtypeStruct((n_rows, LANES), F32),
        compiler_params=_params(("arbitrary",)),
        name="router",
    )(h, norm_g, mod, wr)


def _ffn(h, layer, mod, norm_g, w_gate, w_up, w_down, e_base, n_exp, gates):
    n_rows = h.shape[0]
    d_ff = w_gate.shape[-1]
    tm = _pick(n_rows, (768, 256))
    tf = 256
    nf = d_ff // tf

    def body(h_ref, g_ref, m_ref, gt_ref, wg_ref, wu_ref, wd_ref, out_ref, a_ref, acc_ref):
        i = pl.program_id(0)
        e = pl.program_id(1)
        f = pl.program_id(2)

        @pl.when(jnp.logical_and(e == 0, f == 0))
        def _():
            a_ref[...] = _norm_mod(i, tm, h_ref, g_ref, m_ref, 3).astype(BF16)
            acc_ref[...] = jnp.zeros_like(acc_ref)

        a = a_ref[...]
        g = jnp.dot(a, wg_ref[...].astype(BF16), preferred_element_type=F32)
        u = jnp.dot(a, wu_ref[...].astype(BF16), preferred_element_type=F32)
        act = _silu(g) * u
        if gates is not None:
            lane = lax.broadcasted_iota(jnp.int32, (tm, LANES), 1)
            act = act * jnp.sum(jnp.where(lane == e, gt_ref[...], 0.0), axis=1, keepdims=True)
        acc_ref[...] += jnp.dot(act.astype(BF16), wd_ref[...].astype(BF16), preferred_element_type=F32)

        @pl.when(jnp.logical_and(e == n_exp - 1, f == nf - 1))
        def _():
            out_ref[...] = h_ref[...] + _mod_row(m_ref, 5, _is_ctx(i, tm)) * acc_ref[...]

    once = pl.Buffered(1)
    gt = gates if gates is not None else jnp.zeros((8, LANES), F32)
    gt_spec = (pl.BlockSpec((tm, LANES), lambda i, e, f: (i, 0)) if gates is not None
               else pl.BlockSpec((8, LANES), lambda i, e, f: (0, 0)))
    return pl.pallas_call(
        body,
        grid=(n_rows // tm, n_exp, nf),
        in_specs=[
            pl.BlockSpec((tm, D_MODEL), lambda i, e, f: (i, 0), pipeline_mode=once),
            pl.BlockSpec((None, 1, D_MODEL), lambda i, e, f: (layer, 0, 0)),
            pl.BlockSpec((None, 2, N_MOD, D_MODEL), lambda i, e, f: (layer, 0, 0, 0)),
            gt_spec,
            pl.BlockSpec((None, D_MODEL, tf), lambda i, e, f: (e_base + e, 0, f)),
            pl.BlockSpec((None, D_MODEL, tf), lambda i, e, f: (e_base + e, 0, f)),
            pl.BlockSpec((None, tf, D_MODEL), lambda i, e, f: (e_base + e, f, 0)),
        ],
        out_specs=pl.BlockSpec((tm, D_MODEL), lambda i, e, f: (i, 0), pipeline_mode=once),
        out_shape=jax.ShapeDtypeStruct((n_rows, D_MODEL), F32),
        scratch_shapes=[pltpu.VMEM((tm, D_MODEL), BF16), pltpu.VMEM((tm, D_MODEL), F32)],
        compiler_params=_params(("arbitrary", "arbitrary", "arbitrary")),
        name="ffn",
    )(h, norm_g, mod, gt, w_gate, w_up, w_down)


def _final_norm(h, final_g):
    n_rows = h.shape[0]
    tm = CTX_LEN

    def body(h_ref, g_ref, out_ref):
        x = h_ref[...]
        out_ref[...] = x * lax.rsqrt(jnp.mean(x * x, axis=-1, keepdims=True) + EPS) * g_ref[...]

    return pl.pallas_call(
        body,
        grid=((n_rows - CTX_LEN) // tm,),
        in_specs=[pl.BlockSpec((tm, D_MODEL), lambda i: (i + 1, 0)),
                  pl.BlockSpec((1, D_MODEL), lambda i: (0, 0))],
        out_specs=pl.BlockSpec((tm, D_MODEL), lambda i: (i, 0)),
        out_shape=jax.ShapeDtypeStruct((n_rows - CTX_LEN, D_MODEL), F32),
        compiler_params=_params(("arbitrary",)),
        name="final_norm",
    )(h, final_g.reshape(1, D_MODEL))


def _sincos_tables(rows):
    quarter = D_MODEL // 4
    omega = 1.0 / (POS_BASE ** (jnp.arange(quarter, dtype=F32) / quarter))

    def emb(pos):
        ang = pos[:, None] * omega[None, :]
        return jnp.concatenate([jnp.sin(ang), jnp.cos(ang)], axis=-1)

    return emb(jnp.arange(rows, dtype=F32)), emb(jnp.arange(GRID_W, dtype=F32))


def kernel(x, c, ctx, c_ctx, ada_w, ada_b, norm_mix_g, norm_ffn_g, final_g, s5_w_in, s5_lam_re, s5_lam_im, s5_log_dt, s5_b_re, s5_b_im, s5_c_re, s5_c_im, s5_d, s5_w_out, sg_w_in, sg_ln_g, sg_ln_b, sg_w_s, sg_b_s, sg_w_out, cv_w_pw1, cv_dw_w, cv_dw_b, cv_ln_g, cv_ln_b, cv_w_pw2, ff_w_gate, ff_w_up, ff_w_down, moe_w_router, moe_w_gate, moe_w_up, moe_w_down):
    bsz, seq, dim = x.shape
    assert bsz == 1 and dim == D_MODEL and ctx.shape == (1, CTX_LEN, D_MODEL)
    p = dict(s5_w_in=s5_w_in, s5_lam_re=s5_lam_re, s5_lam_im=s5_lam_im, s5_log_dt=s5_log_dt,
             s5_b_re=s5_b_re, s5_b_im=s5_b_im, s5_c_re=s5_c_re, s5_c_im=s5_c_im, s5_d=s5_d,
             s5_w_out=s5_w_out, sg_w_in=sg_w_in, sg_ln_g=sg_ln_g, sg_ln_b=sg_ln_b, sg_w_s=sg_w_s,
             sg_b_s=sg_b_s, sg_w_out=sg_w_out, cv_w_pw1=cv_w_pw1, cv_dw_w=cv_dw_w, cv_dw_b=cv_dw_b,
             cv_ln_g=cv_ln_g, cv_ln_b=cv_ln_b, cv_w_pw2=cv_w_pw2)
    row_tab, col_tab = _sincos_tables(seq // GRID_W)
    h = _embed(x[0], ctx[0], row_tab, col_tab)
    cond8 = jnp.concatenate([c, c_ctx[None, :], jnp.zeros((6, D_MODEL), F32)], axis=0)
    mod = _modulation(cond8, ada_w, ada_b)[:, :2, :].reshape(DEPTH, 2, N_MOD, D_MODEL)
    g_mix = norm_mix_g.reshape(DEPTH, 1, D_MODEL)
    g_ffn = norm_ffn_g.reshape(DEPTH, 1, D_MODEL)
    for layer in range(DEPTH):
        kind, j = layer % N_MIXERS, layer // N_MIXERS
        if kind == 0:
            h = _s5_layer(h, layer, j, mod, g_mix, p)
        elif kind == 1:
            h = _sg_layer(h, layer, j, mod, g_mix, p)
        else:
            h = _conv_layer(h, layer, j, mod, g_mix, p)
        k = layer // 2
        if layer % 2 == 0:
            h = _ffn(h, layer, mod, g_ffn, ff_w_gate, ff_w_up, ff_w_down, k, 1, None)
        else:
            gates = _router(h, layer, k, mod, g_ffn, moe_w_router)
            d_ff = moe_w_gate.shape[-1]
            h = _ffn(h, layer, mod, g_ffn, moe_w_gate.reshape(-1, D_MODEL, d_ff),
                     moe_w_up.reshape(-1, D_MODEL, d_ff), moe_w_down.reshape(-1, d_ff, D_MODEL),
                     k * N_EXPERTS, N_EXPERTS, gates)
    return _final_norm(h, final_g)[None]
```

```python
import functools
import math

import jax
import jax.numpy as jnp
from jax import lax
from jax.experimental import pallas as pl
from jax.experimental.pallas import tpu as pltpu

F32 = jnp.float32
BF16 = jnp.bfloat16
HIGHEST = lax.Precision.HIGHEST

D_MODEL = 2048
DEPTH = 4
GRID_W = 64
CTX_LEN = 256
POS_BASE = 10000.0
N_MIXERS = 3
N_MOD = 6
EPS = 1e-6
S5_GROUP = 16
S5_GROUPS = D_MODEL // S5_GROUP
S5_STATE = 64
S5_CHUNK = 16
S5_CW = S5_CHUNK * S5_GROUP
SG_CHUNK = 128
SG_HEADS = 8
SG_HEAD_DIM = D_MODEL // SG_HEADS
CONV_WIDTH = 31
CONV_HALO = 16
N_EXPERTS = 8
TOP_K = 2
LANES = 128
VMEM_LIMIT = 56 * 1024 * 1024


def _pick(n, candidates):
    for c in candidates:
        if n % c == 0:
            return c
    raise ValueError(f"no tile in {candidates} divides {n}")


def _params(sem):
    return pltpu.CompilerParams(dimension_semantics=sem, vmem_limit_bytes=VMEM_LIMIT)


def _sigmoid(x):
    return 1.0 / (1.0 + jnp.exp(-x))


def _silu(x):
    return x * _sigmoid(x)


def _gelu_tanh(x):
    return 0.5 * x * (1.0 + jnp.tanh(math.sqrt(2.0 / math.pi) * (x + 0.044715 * (x * x * x))))


def _is_ctx(i, tm):
    rows = i * tm + lax.broadcasted_iota(jnp.int32, (tm, 1), 0)
    return rows < CTX_LEN


def _mod_row(mod_ref, k, is_ctx):
    return jnp.where(is_ctx, mod_ref[1, k:k + 1, :], mod_ref[0, k:k + 1, :])


def _norm_mod_rows(row0, h, g, mod_ref, k_shift):
    inv = lax.rsqrt(jnp.mean(h * h, axis=-1, keepdims=True) + EPS)
    hn = (h * inv) * g
    is_ctx = (row0 + lax.broadcasted_iota(jnp.int32, (h.shape[0], 1), 0)) < CTX_LEN
    return hn * (1.0 + _mod_row(mod_ref, k_shift + 1, is_ctx)) + _mod_row(mod_ref, k_shift, is_ctx)


def _norm_mod(i, tm, h_ref, g_ref, mod_ref, k_shift):
    return _norm_mod_rows(i * tm, h_ref[...], g_ref[...], mod_ref, k_shift)


def _row_chunk(tm):
    for c in range(128, 0, -16):
        if tm % c == 0:
            return c
    raise ValueError(f"no row chunk for tile of {tm} rows")


def _layer_norm(v, g, b):
    mu = jnp.mean(v, axis=-1, keepdims=True)
    vc = v - mu
    var = jnp.mean(vc * vc, axis=-1, keepdims=True)
    return vc * lax.rsqrt(var + EPS) * g + b


def _fused_matmul(*, name, n_rows, k_dim, n_out, tm, tn, pro_args, pro_specs, pro_fn,
                  w, w_layer, w_col_offsets, epi_args, epi_specs, epi_fn, out_dtype):
    n_pro, n_w, n_epi = len(pro_args), len(w_col_offsets), len(epi_args)
    nj = n_out // tn

    def body(*refs):
        pro_refs = refs[:n_pro]
        w_refs = refs[n_pro:n_pro + n_w]
        epi_refs = refs[n_pro + n_w:n_pro + n_w + n_epi]
        out_ref, a_ref = refs[-2], refs[-1]
        i = pl.program_id(0)
        j = pl.program_id(1)

        @pl.when(j == 0)
        def _():
            a_ref[...] = pro_fn(i, *pro_refs).astype(BF16)

        a = a_ref[...]
        ys = [jnp.dot(a, wr[...].astype(BF16), preferred_element_type=F32) for wr in w_refs]
        out_ref[...] = epi_fn(i, j, ys, *epi_refs).astype(out_dtype)

    w_specs = [
        pl.BlockSpec((None, k_dim, tn), functools.partial(lambda i, j, o: (w_layer, 0, j + o), o=off // tn))
        for off in w_col_offsets
    ]
    return pl.pallas_call(
        body,
        grid=(n_rows // tm, nj),
        in_specs=list(pro_specs) + w_specs + list(epi_specs),
        out_specs=pl.BlockSpec((tm, tn), lambda i, j: (i, j)),
        out_shape=jax.ShapeDtypeStruct((n_rows, n_out), out_dtype),
        scratch_shapes=[pltpu.VMEM((tm, k_dim), BF16)],
        compiler_params=_params(("arbitrary", "arbitrary")),
        name=name,
    )(*pro_args, *([w] * n_w), *epi_args)


def _norm_specs(layer, tm):
    return [
        pl.BlockSpec((tm, D_MODEL), lambda i, j: (i, 0)),
        pl.BlockSpec((None, 1, D_MODEL), lambda i, j: (layer, 0, 0)),
        pl.BlockSpec((None, 2, N_MOD, D_MODEL), lambda i, j: (layer, 0, 0, 0)),
    ]


def _res_specs(layer, tm, tn):
    return [
        pl.BlockSpec((tm, tn), lambda i, j: (i, j)),
        pl.BlockSpec((None, 2, N_MOD, tn), lambda i, j: (layer, 0, 0, j)),
    ]


def _gated_residual(i, tm, k_gate, y, h_ref, mod_ref):
    return h_ref[...] + _mod_row(mod_ref, k_gate, _is_ctx(i, tm)) * y


def _embed(x2, ctx2, row_tab, col_tab):
    seq = x2.shape[0]
    tm = 4 * GRID_W
    half = D_MODEL // 2
    n_tiles = (CTX_LEN + seq) // tm
    rows_per_tile = tm // GRID_W
    row_tab3 = row_tab.reshape(seq // tm, rows_per_tile, half)

    def body(ctx_ref, x_ref, row_ref, col_ref, out_ref):
        i = pl.program_id(0)

        @pl.when(i == 0)
        def _():
            out_ref[...] = ctx_ref[...]

        @pl.when(i > 0)
        def _():
            rt = row_ref[0]
            row_part = jnp.concatenate(
                [jnp.broadcast_to(rt[r:r + 1, :], (GRID_W, half)) for r in range(rows_per_tile)], axis=0)
            col_part = jnp.concatenate([col_ref[...]] * rows_per_tile, axis=0)
            out_ref[:, :half] = x_ref[:, :half] + row_part
            out_ref[:, half:] = x_ref[:, half:] + col_part

    assert CTX_LEN == tm
    return pl.pallas_call(
        body,
        grid=(n_tiles,),
        in_specs=[
            pl.BlockSpec((tm, D_MODEL), lambda i: (0, 0)),
            pl.BlockSpec((tm, D_MODEL), lambda i: (jnp.maximum(i - 1, 0), 0)),
            pl.BlockSpec((1, rows_per_tile, half), lambda i: (jnp.maximum(i - 1, 0), 0, 0)),
            pl.BlockSpec((GRID_W, half), lambda i: (0, 0)),
        ],
        out_specs=pl.BlockSpec((tm, D_MODEL), lambda i: (i, 0)),
        out_shape=jax.ShapeDtypeStruct((CTX_LEN + seq, D_MODEL), F32),
        compiler_params=_params(("arbitrary",)),
        name="embed",
    )(ctx2, x2, row_tab3, col_tab)


def _modulation(cond8, ada_w, ada_b):
    n = N_MOD * D_MODEL
    tn = 1024

    def body(c_ref, w_ref, b_ref, out_ref):
        a = _silu(c_ref[...]).astype(BF16)
        out_ref[...] = jnp.dot(a, w_ref[...].astype(BF16), preferred_element_type=F32) + b_ref[...]

    return pl.pallas_call(
        body,
        grid=(DEPTH, n // tn),
        in_specs=[
            pl.BlockSpec((8, D_MODEL), lambda l, j: (0, 0)),
            pl.BlockSpec((None, D_MODEL, tn), lambda l, j: (l, 0, j)),
            pl.BlockSpec((None, 1, tn), lambda l, j: (l, 0, j)),
        ],
        out_specs=pl.BlockSpec((None, 8, tn), lambda l, j: (l, 0, j)),
        out_shape=jax.ShapeDtypeStruct((DEPTH, 8, n), F32),
        compiler_params=_params(("arbitrary", "arbitrary")),
        name="modulation",
    )(cond8, ada_w, ada_b.reshape(DEPTH, 1, n))


def _s5_operators(lam_re, lam_im, log_dt, bt_re, bt_im, c_re, c_im):
    G, T, H, P, CW = S5_GROUPS, S5_CHUNK, S5_GROUP, S5_STATE, S5_CW

    def body(lre_ref, lim_ref, ldt_ref, btr_ref, bti_ref, cr_ref, ci_ref,
             mt_ref, we_ref, wof_ref, wob_ref, at_ref):
        lane_blk = lax.broadcasted_iota(jnp.int32, (CW, CW), 1) // H
        rep = (lax.broadcasted_iota(jnp.int32, (H, CW), 1) % H
               == lax.broadcasted_iota(jnp.int32, (H, CW), 0)).astype(F32)
        zeros64 = jnp.zeros((CW, P), F32)
        mt = jnp.zeros((CW, CW), F32)
        we_parts, wo_parts, at_parts = [], [], []
        for d in range(2):
            lr = jnp.minimum(lre_ref[d, 0], -1e-4)
            li = lim_ref[d, 0]
            dt = jnp.exp(ldt_ref[d, 0])
            mag = jnp.exp(lr * dt)
            ar = mag * jnp.cos(li * dt)
            ai = mag * jnp.sin(li * dt)
            xr = ar - 1.0
            xi = ai
            den = lr * lr + li * li
            kr = (xr * lr + xi * li) / den
            ki = (xi * lr - xr * li) / den
            btr, bti = btr_ref[d, 0], bti_ref[d, 0]
            bbr = kr * btr - ki * bti
            bbi = kr * bti + ki * btr
            cr, ci = cr_ref[d, 0], ci_ref[d, 0]
            pr = [jnp.ones((1, P), F32)]
            pi = [jnp.zeros((1, P), F32)]
            for _ in range(T):
                pr.append(pr[-1] * ar - pi[-1] * ai)
                pi.append(pr[-2] * ai + pi[-1] * ar)
            ba_re = [bbr * pr[l] - bbi * pi[l] for l in range(T)]
            ba_im = [bbr * pi[l] + bbi * pr[l] for l in range(T)]
            order = list(range(T - 1, -1, -1)) if d == 0 else list(range(T))
            e_re = jnp.concatenate([ba_re[l] for l in order], axis=0)
            e_im = jnp.concatenate([ba_im[l] for l in order], axis=0)
            cmat = jnp.concatenate([cr, -ci], axis=1)
            kn = lax.dot_general(jnp.concatenate([e_re, e_im], axis=1), cmat,
                                 (((1,), (1,)), ((), ())), precision=HIGHEST,
                                 preferred_element_type=F32)
            kw = jnp.dot(kn, rep, precision=HIGHEST, preferred_element_type=F32)
            for t in range(T):
                if d == 0:
                    sh = H * (T - 1 - t)
                    shifted = kw if sh == 0 else jnp.concatenate(
                        [kw[sh:, :], jnp.zeros((sh, CW), F32)], axis=0)
                else:
                    sh = H * t
                    shifted = kw if sh == 0 else jnp.concatenate(
                        [jnp.zeros((sh, CW), F32), kw[:CW - sh, :]], axis=0)
                mt = mt + jnp.where(lane_blk == t, shifted, 0.0)
            lags = [t + 1 for t in range(T)] if d == 0 else [T - t for t in range(T)]
            ca_re = jnp.concatenate([cr * pr[l] - ci * pi[l] for l in lags], axis=0)
            ca_im = jnp.concatenate([cr * pi[l] + ci * pr[l] for l in lags], axis=0)
            we_parts.append((e_re, e_im))
            wo_parts.append((ca_re, -ca_im))
            at_parts.append((pr[T], pi[T]))
        mt_ref[0] = mt.astype(BF16)
        we_ref[0] = jnp.concatenate(
            [we_parts[0][0], we_parts[1][0], we_parts[0][1], we_parts[1][1]], axis=1).astype(BF16)
        wof_ref[0] = jnp.concatenate(
            [wo_parts[0][0], zeros64, wo_parts[0][1], zeros64], axis=1).astype(BF16)
        wob_ref[0] = jnp.concatenate(
            [zeros64, wo_parts[1][0], zeros64, wo_parts[1][1]], axis=1).astype(BF16)
        at_ref[0] = jnp.concatenate(
            [at_parts[0][0], at_parts[1][0], at_parts[0][1], at_parts[1][1]], axis=1)

    vec_spec = pl.BlockSpec((2, 1, 1, P), lambda g: (0, g, 0, 0))
    mat_spec = pl.BlockSpec((2, 1, H, P), lambda g: (0, g, 0, 0))
    op_spec = pl.BlockSpec((1, CW, CW), lambda g: (g, 0, 0))
    op_shape = jax.ShapeDtypeStruct((G, CW, CW), BF16)
    return pl.pallas_call(
        body,
        grid=(G,),
        in_specs=[vec_spec, vec_spec, pl.BlockSpec((2, 1, 1, 1), lambda g: (0, g, 0, 0)),
                  mat_spec, mat_spec, mat_spec, mat_spec],
        out_specs=[op_spec, op_spec, op_spec, op_spec, pl.BlockSpec((1, 1, CW), lambda g: (g, 0, 0))],
        out_shape=[op_shape, op_shape, op_shape, op_shape, jax.ShapeDtypeStruct((G, 1, CW), F32)],
        compiler_params=_params(("arbitrary",)),
        name="s5_operators",
    )(lam_re.reshape(2, G, 1, P), lam_im.reshape(2, G, 1, P), log_dt.reshape(2, G, 1, 1),
      bt_re, bt_im, c_re, c_im)


def _s5_local_state(u_r, we):
    G, nc, cw = u_r.shape

    def body(u_ref, we_ref, out_ref):
        out_ref[0] = jnp.dot(u_ref[0].astype(BF16), we_ref[0], preferred_element_type=F32)

    return pl.pallas_call(
        body,
        grid=(G,),
        in_specs=[pl.BlockSpec((1, nc, cw), lambda g: (g, 0, 0)),
                  pl.BlockSpec((1, cw, cw), lambda g: (g, 0, 0))],
        out_specs=pl.BlockSpec((1, nc, cw), lambda g: (g, 0, 0)),
        out_shape=jax.ShapeDtypeStruct((G, nc, cw), F32),
        compiler_params=_params(("arbitrary",)),
        name="s5_local_state",
    )(u_r, we)


def _s5_carry(sloc_t, at):
    nc, G, cw = sloc_t.shape
    cb = CTX_LEN // S5_CHUNK
    nb = nc // cb
    half = cw // 2

    def rev_block(s):
        return jnp.where(s == 0, 0, nb - s)

    def body(sf_ref, sb_ref, at_ref, pf_ref, pb_ref, st_ref):
        s = pl.program_id(0)

        @pl.when(s == 0)
        def _():
            st_ref[...] = jnp.zeros_like(st_ref)

        is_fwd = (lax.broadcasted_iota(jnp.int32, (G, half), 1) < S5_STATE)
        a_re, a_im = at_ref[:, :half], at_ref[:, half:]
        s_re, s_im = st_ref[:, :half], st_ref[:, half:]
        for k in range(cb):
            kb = cb - 1 - k
            state = jnp.concatenate([s_re, s_im], axis=1).astype(BF16)
            pf_ref[k] = state
            pb_ref[kb] = state
            l_re = jnp.where(is_fwd, sf_ref[k, :, :half], sb_ref[kb, :, :half])
            l_im = jnp.where(is_fwd, sf_ref[k, :, half:], sb_ref[kb, :, half:])
            n_re = a_re * s_re - a_im * s_im + l_re
            n_im = a_re * s_im + a_im * s_re + l_im
            s_re, s_im = n_re, n_im
        st_ref[:, :half] = s_re
        st_ref[:, half:] = s_im

    blk = (cb, G, cw)
    return pl.pallas_call(
        body,
        grid=(nb,),
        in_specs=[pl.BlockSpec(blk, lambda s: (s, 0, 0)),
                  pl.BlockSpec(blk, lambda s: (rev_block(s), 0, 0)),
                  pl.BlockSpec((G, cw), lambda s: (0, 0))],
        out_specs=[pl.BlockSpec(blk, lambda s: (s, 0, 0)),
                   pl.BlockSpec(blk, lambda s: (rev_block(s), 0, 0))],
        out_shape=[jax.ShapeDtypeStruct((nc, G, cw), BF16)] * 2,
        scratch_shapes=[pltpu.VMEM((G, cw), F32)],
        compiler_params=_params(("arbitrary",)),
        name="s5_carry",
    )(sloc_t, sloc_t, at)


def _s5_output(u_r, mt, pf, pb, wof, wob, d_tile):
    G, nc, cw = u_r.shape
    nt = (((1,), (1,)), ((), ()))

    def body(u_ref, mt_ref, pf_ref, pb_ref, wof_ref, wob_ref, d_ref, y_ref):
        u = u_ref[0]
        y = jnp.dot(u.astype(BF16), mt_ref[0], preferred_element_type=F32)
        y = y + lax.dot_general(pf_ref[0], wof_ref[0], nt, preferred_element_type=F32)
        y = y + lax.dot_general(pb_ref[0], wob_ref[0], nt, preferred_element_type=F32)
        y_ref[0] = y + d_ref[0] * u

    act = pl.BlockSpec((1, nc, cw), lambda g: (g, 0, 0))
    op = pl.BlockSpec((1, cw, cw), lambda g: (g, 0, 0))
    return pl.pallas_call(
        body,
        grid=(G,),
        in_specs=[act, op, act, act, op, op, pl.BlockSpec((1, 1, cw), lambda g: (g, 0, 0))],
        out_specs=act,
        out_shape=jax.ShapeDtypeStruct((G, nc, cw), F32),
        compiler_params=_params(("arbitrary",)),
        name="s5_output",
    )(u_r, mt, pf, pb, wof, wob, d_tile)


def _s5_layer(h, layer, j, mod, norm_g, p):
    n_rows = h.shape[0]
    G, T, H = S5_GROUPS, S5_CHUNK, S5_GROUP
    nc = n_rows // T
    tm = _pick(n_rows, (768, 256))
    tn = 512
    u = _fused_matmul(
        name="s5_in", n_rows=n_rows, k_dim=D_MODEL, n_out=D_MODEL, tm=tm, tn=tn,
        pro_args=(h, norm_g, mod), pro_specs=_norm_specs(layer, tm),
        pro_fn=lambda i, h_ref, g_ref, m_ref: _norm_mod(i, tm, h_ref, g_ref, m_ref, 0),
        w=p['s5_w_in'], w_layer=j, w_col_offsets=(0,),
        epi_args=(), epi_specs=(), epi_fn=lambda i, jj, ys: ys[0], out_dtype=F32)
    u_r = u.reshape(nc, T, G, H).transpose(2, 0, 1, 3).reshape(G, nc, T * H)
    mt, we, wof, wob, at = _s5_operators(
        p['s5_lam_re'][j], p['s5_lam_im'][j], p['s5_log_dt'][j],
        jnp.swapaxes(p['s5_b_re'][j], -1, -2), jnp.swapaxes(p['s5_b_im'][j], -1, -2),
        p['s5_c_re'][j], p['s5_c_im'][j])
    sloc = _s5_local_state(u_r, we)
    pf, pb = _s5_carry(sloc.transpose(1, 0, 2), at.reshape(G, T * H))
    d_tile = jnp.tile(p['s5_d'][j].reshape(G, 1, H), (1, 1, T))
    y_r = _s5_output(u_r, mt, pf.transpose(1, 0, 2), pb.transpose(1, 0, 2), wof, wob, d_tile)
    y = y_r.reshape(G, nc, T, H).transpose(1, 2, 0, 3).reshape(n_rows, D_MODEL)
    return _fused_matmul(
        name="s5_out", n_rows=n_rows, k_dim=D_MODEL, n_out=D_MODEL, tm=tm, tn=tn,
        pro_args=(y,), pro_specs=[pl.BlockSpec((tm, D_MODEL), lambda i, jj: (i, 0))],
        pro_fn=lambda i, y_ref: _gelu_tanh(y_ref[...]),
        w=p['s5_w_out'], w_layer=j, w_col_offsets=(0, D_MODEL),
        epi_args=(h, mod), epi_specs=_res_specs(layer, tm, tn),
        epi_fn=lambda i, jj, ys, h_ref, m_ref: _gated_residual(
            i, tm, 2, ys[0] * _sigmoid(ys[1]), h_ref, m_ref),
        out_dtype=F32)


def _sg_layer(h, layer, j, mod, norm_g, p):
    n_rows = h.shape[0]
    tm = _pick(n_rows, (768, 256))
    tn = 512
    z = _fused_matmul(
        name="sg_in", n_rows=n_rows, k_dim=D_MODEL, n_out=2 * D_MODEL, tm=tm, tn=tn,
        pro_args=(h, norm_g, mod), pro_specs=_norm_specs(layer, tm),
        pro_fn=lambda i, h_ref, g_ref, m_ref: _norm_mod(i, tm, h_ref, g_ref, m_ref, 0),
        w=p['sg_w_in'], w_layer=j, w_col_offsets=(0,),
        epi_args=(), epi_specs=(), epi_fn=lambda i, jj, ys: _gelu_tanh(ys[0]), out_dtype=BF16)

    def gate(i, u_ref, v_ref, g_ref, b_ref, ws_ref, bs_ref):
        v = _layer_norm(v_ref[...].astype(F32), g_ref[...], b_ref[...]).astype(BF16)
        rows = []
        for c in range(tm // SG_CHUNK):
            heads = []
            for hd in range(SG_HEADS):
                vc = v[c * SG_CHUNK:(c + 1) * SG_CHUNK, hd * SG_HEAD_DIM:(hd + 1) * SG_HEAD_DIM]
                sv = jnp.dot(ws_ref[hd].astype(BF16), vc, preferred_element_type=F32)
                heads.append(sv + bs_ref[:, hd:hd + 1])
            rows.append(jnp.concatenate(heads, axis=1))
        return u_ref[...].astype(F32) * jnp.concatenate(rows, axis=0)

    return _fused_matmul(
        name="sg_out", n_rows=n_rows, k_dim=D_MODEL, n_out=D_MODEL, tm=tm, tn=tn,
        pro_args=(z, z, p['sg_ln_g'], p['sg_ln_b'], p['sg_w_s'], jnp.swapaxes(p['sg_b_s'], -1, -2)),
        pro_specs=[
            pl.BlockSpec((tm, D_MODEL), lambda i, jj: (i, 0)),
            pl.BlockSpec((tm, D_MODEL), lambda i, jj: (i, 1)),
            pl.BlockSpec((1, D_MODEL), lambda i, jj: (j, 0)),
            pl.BlockSpec((1, D_MODEL), lambda i, jj: (j, 0)),
            pl.BlockSpec((None, SG_HEADS, SG_CHUNK, SG_CHUNK), lambda i, jj: (j, 0, 0, 0)),
            pl.BlockSpec((None, SG_CHUNK, SG_HEADS), lambda i, jj: (j, 0, 0)),
        ],
        pro_fn=gate,
        w=p['sg_w_out'], w_layer=j, w_col_offsets=(0,),
        epi_args=(h, mod), epi_specs=_res_specs(layer, tm, tn),
        epi_fn=lambda i, jj, ys, h_ref, m_ref: _gated_residual(i, tm, 2, ys[0], h_ref, m_ref),
        out_dtype=F32)


def _conv_norm(z, dw_w, dw_b, ln_g, ln_b, j):
    n_rows = z.shape[0]
    tm = CTX_LEN
    n_tiles = n_rows // tm
    hb = tm // CONV_HALO
    half = CONV_WIDTH // 2
    rb, cbw = 64, LANES

    def body(prev_ref, cur_ref, next_ref, w_ref, b_ref, g_ref, beta_ref, out_ref, zp_ref, acc_ref):
        i = pl.program_id(0)
        prev_ok = i >= 2
        next_ok = jnp.logical_and(i >= 1, i < n_tiles - 1)
        zp_ref[0:CONV_HALO, :] = jnp.where(prev_ok, prev_ref[...], 0.0)
        zp_ref[CONV_HALO:CONV_HALO + tm, :] = cur_ref[...]
        zp_ref[CONV_HALO + tm:, :] = jnp.where(next_ok, next_ref[...], 0.0)

        def col_block(cb, carry):
            c0 = pl.multiple_of(cb * cbw, cbw)
            w = w_ref[:, pl.ds(c0, cbw)]
            for r in range(tm // rb):
                acc = jnp.zeros((rb, cbw), F32)
                for k in range(CONV_WIDTH):
                    r0 = r * rb + CONV_HALO - half + k
                    acc = acc + zp_ref[r0:r0 + rb, pl.ds(c0, cbw)] * w[k:k + 1, :]
                acc_ref[r * rb:(r + 1) * rb, pl.ds(c0, cbw)] = acc
            return carry

        lax.fori_loop(0, D_MODEL // cbw, col_block, 0)
        y = _layer_norm(acc_ref[...] + b_ref[...], g_ref[...], beta_ref[...])
        out_ref[...] = _silu(y).astype(BF16)

    vec = pl.BlockSpec((1, D_MODEL), lambda i: (j, 0))
    return pl.pallas_call(
        body,
        grid=(n_tiles,),
        in_specs=[
            pl.BlockSpec((CONV_HALO, D_MODEL), lambda i: (jnp.maximum(i * hb - 1, 0), 0)),
            pl.BlockSpec((tm, D_MODEL), lambda i: (i, 0)),
            pl.BlockSpec((CONV_HALO, D_MODEL), lambda i: (jnp.minimum((i + 1) * hb, n_tiles * hb - 1), 0)),
            pl.BlockSpec((None, CONV_WIDTH, D_MODEL), lambda i: (j, 0, 0)),
            vec, vec, vec,
        ],
        out_specs=pl.BlockSpec((tm, D_MODEL), lambda i: (i, 0)),
        out_shape=jax.ShapeDtypeStruct((n_rows, D_MODEL), BF16),
        scratch_shapes=[pltpu.VMEM((tm + 2 * CONV_HALO, D_MODEL), F32), pltpu.VMEM((tm, D_MODEL), F32)],
        compiler_params=_params(("arbitrary",)),
        name="conv_norm",
    )(z, z, z, dw_w, dw_b, ln_g, ln_b)


def _conv_layer(h, layer, j, mod, norm_g, p):
    n_rows = h.shape[0]
    tm = _pick(n_rows, (768, 256))
    tn = 512
    z = _fused_matmul(
        name="cv_pw1", n_rows=n_rows, k_dim=D_MODEL, n_out=D_MODEL, tm=tm, tn=tn,
        pro_args=(h, norm_g, mod), pro_specs=_norm_specs(layer, tm),
        pro_fn=lambda i, h_ref, g_ref, m_ref: _norm_mod(i, tm, h_ref, g_ref, m_ref, 0),
        w=p['cv_w_pw1'], w_layer=j, w_col_offsets=(0, D_MODEL),
        epi_args=(), epi_specs=(), epi_fn=lambda i, jj, ys: ys[0] * _sigmoid(ys[1]), out_dtype=F32)
    zc = _conv_norm(z, p['cv_dw_w'], p['cv_dw_b'], p['cv_ln_g'], p['cv_ln_b'], j)
    return _fused_matmul(
        name="cv_pw2", n_rows=n_rows, k_dim=D_MODEL, n_out=D_MODEL, tm=tm, tn=tn,
        pro_args=(zc,), pro_specs=[pl.BlockSpec((tm, D_MODEL), lambda i, jj: (i, 0))],
        pro_fn=lambda i, z_ref: z_ref[...],
        w=p['cv_w_pw2'], w_layer=j, w_col_offsets=(0,),
        epi_args=(h, mod), epi_specs=_res_specs(layer, tm, tn),
        epi_fn=lambda i, jj, ys, h_ref, m_ref: _gated_residual(i, tm, 2, ys[0], h_ref, m_ref),
        out_dtype=F32)


SLAB_RANK1, SLAB_RANK2, SLAB_EXP1, SLAB_EXP2, SLAB_W1, SLAB_W2 = range(6)


def _router(h, layer, k, mod, norm_g, w_router):
    n_rows = h.shape[0]
    tm = _pick(n_rows, (768, 256))
    wr = jnp.pad(w_router[k], ((0, 0), (0, LANES - N_EXPERTS)))

    def body(h_ref, g_ref, m_ref, wr_ref, hn_ref, slab_ref, cnt_ref, base_ref):
        i = pl.program_id(0)

        @pl.when(i == 0)
        def _():
            base_ref[...] = jnp.zeros_like(base_ref)

        hn = _norm_mod(i, tm, h_ref, g_ref, m_ref, 3)
        hn_ref[...] = hn
        logits = jnp.dot(hn, wr_ref[...], precision=HIGHEST, preferred_element_type=F32)
        lane = lax.broadcasted_iota(jnp.int32, (tm, LANES), 1).astype(F32)
        neg = jnp.float32(-jnp.inf)
        lg = jnp.where(lane < N_EXPERTS, logits, neg)
        m1 = jnp.max(lg, axis=1, keepdims=True)
        i1 = jnp.min(jnp.where(lg == m1, lane, float(LANES)), axis=1, keepdims=True)
        lg2 = jnp.where(lane == i1, neg, lg)
        m2 = jnp.max(lg2, axis=1, keepdims=True)
        i2 = jnp.min(jnp.where(lg2 == m2, lane, float(LANES)), axis=1, keepdims=True)
        e = jnp.exp(m2 - m1)
        w1 = 1.0 / (1.0 + e)
        w2 = e / (1.0 + e)
        onehot = jnp.where(lane == i1, 1.0, 0.0) + jnp.where(lane == i2, 1.0, 0.0)
        tri = jnp.where(lax.broadcasted_iota(jnp.int32, (tm, tm), 1)
                        < lax.broadcasted_iota(jnp.int32, (tm, tm), 0), 1.0, 0.0).astype(BF16)
        base = base_ref[0:1, :]
        excl = jnp.dot(tri, onehot.astype(BF16), preferred_element_type=F32) + base
        r1 = jnp.sum(jnp.where(lane == i1, excl, 0.0), axis=1, keepdims=True)
        r2 = jnp.sum(jnp.where(lane == i2, excl, 0.0), axis=1, keepdims=True)
        total = base + jnp.sum(onehot, axis=0, keepdims=True)
        base_ref[...] = jnp.broadcast_to(total, base_ref.shape)
        cnt_ref[...] = jnp.broadcast_to(total, cnt_ref.shape)
        slab = jnp.zeros((tm, LANES), F32)
        for ln, val in ((SLAB_RANK1, r1), (SLAB_RANK2, r2), (SLAB_EXP1, i1), (SLAB_EXP2, i2),
                        (SLAB_W1, w1), (SLAB_W2, w2)):
            slab = jnp.where(lane == float(ln), val, slab)
        slab_ref[...] = slab

    return pl.pallas_call(
        body,
        grid=(n_rows // tm,),
        in_specs=[
            pl.BlockSpec((tm, D_MODEL), lambda i: (i, 0)),
            pl.BlockSpec((None, 1, D_MODEL), lambda i: (layer, 0, 0)),
            pl.BlockSpec((None, 2, N_MOD, D_MODEL), lambda i: (layer, 0, 0, 0)),
            pl.BlockSpec((D_MODEL, LANES), lambda i: (0, 0)),
        ],
        out_specs=[pl.BlockSpec((tm, D_MODEL), lambda i: (i, 0)),
                   pl.BlockSpec((tm, LANES), lambda i: (i, 0)),
                   pl.BlockSpec((8, LANES), lambda i: (0, 0))],
        out_shape=[jax.ShapeDtypeStruct((n_rows, D_MODEL), F32),
                   jax.ShapeDtypeStruct((n_rows, LANES), F32),
                   jax.ShapeDtypeStruct((8, LANES), F32)],
        scratch_shapes=[pltpu.VMEM((8, LANES), F32)],
        compiler_params=_params(("arbitrary",)),
        name="router",
    )(h, norm_g, mod, wr)


def _routing_tables(slab, counts, tb, nblk):
    cnt = counts[0, :N_EXPERTS].astype(jnp.int32)
    padded = ((cnt + tb - 1) // tb) * tb
    ends = jnp.cumsum(padded)
    starts = ends - padded
    pos1 = starts[slab[:, SLAB_EXP1].astype(jnp.int32)] + slab[:, SLAB_RANK1].astype(jnp.int32)
    pos2 = starts[slab[:, SLAB_EXP2].astype(jnp.int32)] + slab[:, SLAB_RANK2].astype(jnp.int32)
    nvalid = ends[-1] // tb
    blk = jnp.minimum(jnp.arange(nblk, dtype=jnp.int32), nvalid - 1)
    blk_expert = jnp.minimum(jnp.searchsorted(ends, blk * tb, side='right'), N_EXPERTS - 1)
    return pos1, pos2, blk_expert.astype(jnp.int32), nvalid.reshape(1).astype(jnp.int32)


def _dispatch(hn, pos1, pos2, n_slots):
    n_rows = hn.shape[0]
    tm = CTX_LEN

    def body(p1_ref, p2_ref, hn_ref, init_ref, xs_ref, sem):
        i = pl.program_id(0)

        def row_copy(t, p):
            return pltpu.make_async_copy(hn_ref.at[pl.ds(t, 1)], xs_ref.at[pl.ds(p, 1)], sem)

        def issue(t, carry):
            row_copy(t, p1_ref[i * tm + t]).start()
            row_copy(t, p2_ref[i * tm + t]).start()
            return carry

        def drain(t, carry):
            row_copy(0, 0).wait()
            row_copy(0, 0).wait()
            return carry

        lax.fori_loop(0, tm, issue, 0)
        lax.fori_loop(0, tm, drain, 0)

    return pl.pallas_call(
        body,
        grid_spec=pltpu.PrefetchScalarGridSpec(
            num_scalar_prefetch=2,
            grid=(n_rows // tm,),
            in_specs=[pl.BlockSpec((tm, D_MODEL), lambda i, p1, p2: (i, 0)),
                      pl.BlockSpec(memory_space=pl.ANY)],
            out_specs=pl.BlockSpec(memory_space=pl.ANY),
            scratch_shapes=[pltpu.SemaphoreType.DMA],
        ),
        out_shape=jax.ShapeDtypeStruct((n_slots, D_MODEL), F32),
        input_output_aliases={3: 0},
        compiler_params=_params(("arbitrary",)),
        name="moe_dispatch",
    )(pos1, pos2, hn, jnp.zeros((n_slots, D_MODEL), F32))


def _swiglu_chunk(a, wg_ref, wu_ref, wd_ref):
    g = jnp.dot(a, wg_ref[...].astype(BF16), preferred_element_type=F32)
    u = jnp.dot(a, wu_ref[...].astype(BF16), preferred_element_type=F32)
    return jnp.dot((_silu(g) * u).astype(BF16), wd_ref[...].astype(BF16), preferred_element_type=F32)


def _moe_ffn(xs, blk_expert, nvalid, w_gate, w_up, w_down, e_base, tb):
    n_slots = xs.shape[0]
    d_ff = w_gate.shape[-1]
    tf = 256
    nf = d_ff // tf

    def body(be_ref, nv_ref, x_ref, wg_ref, wu_ref, wd_ref, out_ref, a_ref):
        b = pl.program_id(0)
        f = pl.program_id(1)
        used = b < nv_ref[0]

        @pl.when(jnp.logical_and(used, f == 0))
        def _():
            a_ref[...] = x_ref[...].astype(BF16)
            out_ref[...] = jnp.zeros_like(out_ref)

        @pl.when(used)
        def _():
            out_ref[...] += _swiglu_chunk(a_ref[...], wg_ref, wu_ref, wd_ref)

        @pl.when(jnp.logical_and(jnp.logical_not(used), f == 0))
        def _():
            out_ref[...] = jnp.zeros_like(out_ref)

    def f_idx(b, f, nv):
        return jnp.where(b < nv[0], f, nf - 1)

    once = pl.Buffered(1)
    rows = lambda b, f, be, nv: (jnp.minimum(b, nv[0] - 1), 0)
    return pl.pallas_call(
        body,
        grid_spec=pltpu.PrefetchScalarGridSpec(
            num_scalar_prefetch=2,
            grid=(n_slots // tb, nf),
            in_specs=[
                pl.BlockSpec((tb, D_MODEL), rows, pipeline_mode=once),
                pl.BlockSpec((None, D_MODEL, tf), lambda b, f, be, nv: (e_base + be[b], 0, f_idx(b, f, nv))),
                pl.BlockSpec((None, D_MODEL, tf), lambda b, f, be, nv: (e_base + be[b], 0, f_idx(b, f, nv))),
                pl.BlockSpec((None, tf, D_MODEL), lambda b, f, be, nv: (e_base + be[b], f_idx(b, f, nv), 0)),
            ],
            out_specs=pl.BlockSpec((tb, D_MODEL), lambda b, f, be, nv: (b, 0), pipeline_mode=once),
            scratch_shapes=[pltpu.VMEM((tb, D_MODEL), BF16)],
        ),
        out_shape=jax.ShapeDtypeStruct((n_slots, D_MODEL), F32),
        compiler_params=_params(("arbitrary", "arbitrary")),
        name="moe_ffn",
    )(blk_expert, nvalid, xs, w_gate, w_up, w_down)


def _combine(h, ys, slab, pos1, pos2, layer, mod):
    n_rows = h.shape[0]
    tm = CTX_LEN

    def body(p1_ref, p2_ref, h_ref, slab_ref, m_ref, ys_ref, out_ref, y1_ref, y2_ref, sem):
        i = pl.program_id(0)

        def row_copy(p, dst_ref, t):
            return pltpu.make_async_copy(ys_ref.at[pl.ds(p, 1)], dst_ref.at[pl.ds(t, 1)], sem)

        def issue(t, carry):
            row_copy(p1_ref[i * tm + t], y1_ref, t).start()
            row_copy(p2_ref[i * tm + t], y2_ref, t).start()
            return carry

        def drain(t, carry):
            row_copy(0, y1_ref, 0).wait()
            row_copy(0, y2_ref, 0).wait()
            return carry

        lax.fori_loop(0, tm, issue, 0)
        lax.fori_loop(0, tm, drain, 0)
        w1 = slab_ref[:, SLAB_W1:SLAB_W1 + 1]
        w2 = slab_ref[:, SLAB_W2:SLAB_W2 + 1]
        moe = w1 * y1_ref[...] + w2 * y2_ref[...]
        out_ref[...] = h_ref[...] + _mod_row(m_ref, 5, _is_ctx(i, tm)) * moe

    return pl.pallas_call(
        body,
        grid_spec=pltpu.PrefetchScalarGridSpec(
            num_scalar_prefetch=2,
            grid=(n_rows // tm,),
            in_specs=[
                pl.BlockSpec((tm, D_MODEL), lambda i, p1, p2: (i, 0)),
                pl.BlockSpec((tm, LANES), lambda i, p1, p2: (i, 0)),
                pl.BlockSpec((None, 2, N_MOD, D_MODEL), lambda i, p1, p2: (layer, 0, 0, 0)),
                pl.BlockSpec(memory_space=pl.ANY),
            ],
            out_specs=pl.BlockSpec((tm, D_MODEL), lambda i, p1, p2: (i, 0)),
            scratch_shapes=[pltpu.VMEM((tm, D_MODEL), F32), pltpu.VMEM((tm, D_MODEL), F32),
                            pltpu.SemaphoreType.DMA],
        ),
        out_shape=jax.ShapeDtypeStruct((n_rows, D_MODEL), F32),
        compiler_params=_params(("arbitrary",)),
        name="moe_combine",
    )(pos1, pos2, h, slab, mod, ys)


def _moe_layer(h, layer, k, mod, norm_g, w_router, w_gate, w_up, w_down):
    n_rows = h.shape[0]
    d_ff = w_gate.shape[-1]
    tb = 1024 if n_rows >= 4096 else 256
    nblk = -(-(TOP_K * n_rows + N_EXPERTS * (tb - 1)) // tb)
    hn, slab, counts = _router(h, layer, k, mod, norm_g, w_router)
    pos1, pos2, blk_expert, nvalid = _routing_tables(slab, counts, tb, nblk)
    xs = _dispatch(hn, pos1, pos2, nblk * tb)
    ys = _moe_ffn(xs, blk_expert, nvalid, w_gate.reshape(-1, D_MODEL, d_ff), w_up.reshape(-1, D_MODEL, d_ff),
                  w_down.reshape(-1, d_ff, D_MODEL), k * N_EXPERTS, tb)
    return _combine(h, ys, slab, pos1, pos2, layer, mod)


def _ffn(h, layer, mod, norm_g, w_gate, w_up, w_down, k):
    n_rows = h.shape[0]
    d_ff = w_gate.shape[-1]
    tm = _pick(n_rows, (1056, 768, 256))
    rc = _row_chunk(tm)
    tf = 256
    nf = d_ff // tf

    def body(h_ref, g_ref, m_ref, wg_ref, wu_ref, wd_ref, out_ref, a_ref):
        i = pl.program_id(0)
        f = pl.program_id(1)

        @pl.when(f == 0)
        def _():
            def chunk(r, carry):
                rows = pl.ds(pl.multiple_of(r * rc, rc), rc)
                a_ref[rows, :] = _norm_mod_rows(i * tm + r * rc, h_ref[rows, :], g_ref[...], m_ref, 3).astype(BF16)
                return carry

            lax.fori_loop(0, tm // rc, chunk, 0)
            out_ref[...] = jnp.zeros_like(out_ref)

        out_ref[...] += _swiglu_chunk(a_ref[...], wg_ref, wu_ref, wd_ref)

        @pl.when(f == nf - 1)
        def _():
            def chunk(r, carry):
                rows = pl.ds(pl.multiple_of(r * rc, rc), rc)
                is_ctx = (i * tm + r * rc + lax.broadcasted_iota(jnp.int32, (rc, 1), 0)) < CTX_LEN
                out_ref[rows, :] = h_ref[rows, :] + _mod_row(m_ref, 5, is_ctx) * out_ref[rows, :]
                return carry

            lax.fori_loop(0, tm // rc, chunk, 0)

    once = pl.Buffered(1)
    return pl.pallas_call(
        body,
        grid=(n_rows // tm, nf),
        in_specs=[
            pl.BlockSpec((tm, D_MODEL), lambda i, f: (i, 0), pipeline_mode=once),
            pl.BlockSpec((None, 1, D_MODEL), lambda i, f: (layer, 0, 0)),
            pl.BlockSpec((None, 2, N_MOD, D_MODEL), lambda i, f: (layer, 0, 0, 0)),
            pl.BlockSpec((None, D_MODEL, tf), lambda i, f: (k, 0, f)),
            pl.BlockSpec((None, D_MODEL, tf), lambda i, f: (k, 0, f)),
            pl.BlockSpec((None, tf, D_MODEL), lambda i, f: (k, f, 0)),
        ],
        out_specs=pl.BlockSpec((tm, D_MODEL), lambda i, f: (i, 0), pipeline_mode=once),
        out_shape=jax.ShapeDtypeStruct((n_rows, D_MODEL), F32),
        scratch_shapes=[pltpu.VMEM((tm, D_MODEL), BF16)],
        compiler_params=_params(("arbitrary", "arbitrary")),
        name="ffn",
    )(h, norm_g, mod, w_gate, w_up, w_down)


def _final_norm(h, final_g):
    n_rows = h.shape[0]
    tm = CTX_LEN

    def body(h_ref, g_ref, out_ref):
        x = h_ref[...]
        out_ref[...] = x * lax.rsqrt(jnp.mean(x * x, axis=-1, keepdims=True) + EPS) * g_ref[...]

    return pl.pallas_call(
        body,
        grid=((n_rows - CTX_LEN) // tm,),
        in_specs=[pl.BlockSpec((tm, D_MODEL), lambda i: (i + 1, 0)),
                  pl.BlockSpec((1, D_MODEL), lambda i: (0, 0))],
        out_specs=pl.BlockSpec((tm, D_MODEL), lambda i: (i, 0)),
        out_shape=jax.ShapeDtypeStruct((n_rows - CTX_LEN, D_MODEL), F32),
        compiler_params=_params(("arbitrary",)),
        name="final_norm",
    )(h, final_g.reshape(1, D_MODEL))


def _sincos_tables(rows):
    quarter = D_MODEL // 4
    omega = 1.0 / (POS_BASE ** (jnp.arange(quarter, dtype=F32) / quarter))

    def emb(pos):
        ang = pos[:, None] * omega[None, :]
        return jnp.concatenate([jnp.sin(ang), jnp.cos(ang)], axis=-1)

    return emb(jnp.arange(rows, dtype=F32)), emb(jnp.arange(GRID_W, dtype=F32))


def kernel(x, c, ctx, c_ctx, ada_w, ada_b, norm_mix_g, norm_ffn_g, final_g, s5_w_in, s5_lam_re, s5_lam_im, s5_log_dt, s5_b_re, s5_b_im, s5_c_re, s5_c_im, s5_d, s5_w_out, sg_w_in, sg_ln_g, sg_ln_b, sg_w_s, sg_b_s, sg_w_out, cv_w_pw1, cv_dw_w, cv_dw_b, cv_ln_g, cv_ln_b, cv_w_pw2, ff_w_gate, ff_w_up, ff_w_down, moe_w_router, moe_w_gate, moe_w_up, moe_w_down):
    bsz, seq, dim = x.shape
    assert bsz == 1 and dim == D_MODEL and ctx.shape == (1, CTX_LEN, D_MODEL)
    p = dict(s5_w_in=s5_w_in, s5_lam_re=s5_lam_re, s5_lam_im=s5_lam_im, s5_log_dt=s5_log_dt,
             s5_b_re=s5_b_re, s5_b_im=s5_b_im, s5_c_re=s5_c_re, s5_c_im=s5_c_im, s5_d=s5_d,
             s5_w_out=s5_w_out, sg_w_in=sg_w_in, sg_ln_g=sg_ln_g, sg_ln_b=sg_ln_b, sg_w_s=sg_w_s,
             sg_b_s=sg_b_s, sg_w_out=sg_w_out, cv_w_pw1=cv_w_pw1, cv_dw_w=cv_dw_w, cv_dw_b=cv_dw_b,
             cv_ln_g=cv_ln_g, cv_ln_b=cv_ln_b, cv_w_pw2=cv_w_pw2)
    row_tab, col_tab = _sincos_tables(seq // GRID_W)
    h = _embed(x[0], ctx[0], row_tab, col_tab)
    cond8 = jnp.concatenate([c, c_ctx[None, :], jnp.zeros((6, D_MODEL), F32)], axis=0)
    mod = _modulation(cond8, ada_w, ada_b)[:, :2, :].reshape(DEPTH, 2, N_MOD, D_MODEL)
    g_mix = norm_mix_g.reshape(DEPTH, 1, D_MODEL)
    g_ffn = norm_ffn_g.reshape(DEPTH, 1, D_MODEL)
    for layer in range(DEPTH):
        kind, j = layer % N_MIXERS, layer // N_MIXERS
        if kind == 0:
            h = _s5_layer(h, layer, j, mod, g_mix, p)
        elif kind == 1:
            h = _sg_layer(h, layer, j, mod, g_mix, p)
        else:
            h = _conv_layer(h, layer, j, mod, g_mix, p)
        k = layer // 2
        if layer % 2 == 0:
            h = _ffn(h, layer, mod, g_ffn, ff_w_gate, ff_w_up, ff_w_down, k)
        else:
            h = _moe_layer(h, layer, k, mod, g_ffn, moe_w_router, moe_w_gate, moe_w_up, moe_w_down)
    return _final_norm(h, final_g)[None]
```

```python
import functools
import math

import jax
import jax.numpy as jnp
from jax import lax
from jax.experimental import pallas as pl
from jax.experimental.pallas import tpu as pltpu

F32 = jnp.float32
BF16 = jnp.bfloat16
HIGHEST = lax.Precision.HIGHEST

D_MODEL = 2048
DEPTH = 4
GRID_W = 64
CTX_LEN = 256
POS_BASE = 10000.0
N_MIXERS = 3
N_MOD = 6
EPS = 1e-6
S5_GROUP = 16
S5_GROUPS = D_MODEL // S5_GROUP
S5_STATE = 64
S5_CHUNK = 16
S5_CW = S5_CHUNK * S5_GROUP
SG_CHUNK = 128
SG_HEADS = 8
SG_HEAD_DIM = D_MODEL // SG_HEADS
CONV_WIDTH = 31
CONV_HALO = 16
N_EXPERTS = 8
TOP_K = 2
LANES = 128
MOD_CHUNK = 128
FFN_CHUNK = 512
MIX_ROW_TILES = (1408, 768, 256)
VMEM_LIMIT = 56 * 1024 * 1024


def _pick(n, candidates):
    for c in candidates:
        if n % c == 0:
            return c
    raise ValueError(f"no tile in {candidates} divides {n}")


def _params(sem):
    return pltpu.CompilerParams(dimension_semantics=sem, vmem_limit_bytes=VMEM_LIMIT)


def _sigmoid(x):
    return 1.0 / (1.0 + jnp.exp(-x))


def _silu(x):
    return x * _sigmoid(x)


def _gelu_tanh(x):
    return 0.5 * x * (1.0 + jnp.tanh(math.sqrt(2.0 / math.pi) * (x + 0.044715 * (x * x * x))))


def _is_ctx(i, tm):
    rows = i * tm + lax.broadcasted_iota(jnp.int32, (tm, 1), 0)
    return rows < CTX_LEN


def _mod_row(mod_ref, k, is_ctx):
    return jnp.where(is_ctx, mod_ref[1, k:k + 1, :], mod_ref[0, k:k + 1, :])


def _norm_mod_rows(row0, h, g, mod_ref, k_shift):
    inv = lax.rsqrt(jnp.mean(h * h, axis=-1, keepdims=True) + EPS)
    hn = (h * inv) * g
    is_ctx = (row0 + lax.broadcasted_iota(jnp.int32, (h.shape[0], 1), 0)) < CTX_LEN
    return hn * (1.0 + _mod_row(mod_ref, k_shift + 1, is_ctx)) + _mod_row(mod_ref, k_shift, is_ctx)


def _norm_mod(i, tm, h_ref, g_ref, mod_ref, k_shift):
    return _norm_mod_rows(i * tm, h_ref[...], g_ref[...], mod_ref, k_shift)


def _norm_mod_into(a_ref, i, tm, h_ref, g_ref, mod_ref, k_shift):
    assert tm % MOD_CHUNK == 0 and CTX_LEN % MOD_CHUNK == 0
    g = g_ref[...]

    def chunk(r, carry):
        rows = pl.ds(pl.multiple_of(r * MOD_CHUNK, MOD_CHUNK), MOD_CHUNK)
        m = jnp.where(i * tm + r * MOD_CHUNK < CTX_LEN, 1, 0)
        h = h_ref[rows, :]
        inv = lax.rsqrt(jnp.mean(h * h, axis=-1, keepdims=True) + EPS)
        scale = mod_ref[m, pl.ds(k_shift + 1, 1), :]
        shift = mod_ref[m, pl.ds(k_shift, 1), :]
        a_ref[rows, :] = (((h * inv) * g) * (1.0 + scale) + shift).astype(a_ref.dtype)
        return carry

    lax.fori_loop(0, tm // MOD_CHUNK, chunk, 0)


def _layer_norm(v, g, b):
    mu = jnp.mean(v, axis=-1, keepdims=True)
    vc = v - mu
    var = jnp.mean(vc * vc, axis=-1, keepdims=True)
    return vc * lax.rsqrt(var + EPS) * g + b


def _fused_matmul(*, name, n_rows, k_dim, n_out, tm, tn, pro_args, pro_specs, pro_fn,
                  w, w_layer, w_col_offsets, epi_args, epi_specs, epi_fn, out_dtype):
    n_pro, n_w, n_epi = len(pro_args), len(w_col_offsets), len(epi_args)
    nj = n_out // tn

    def body(*refs):
        pro_refs = refs[:n_pro]
        w_refs = refs[n_pro:n_pro + n_w]
        epi_refs = refs[n_pro + n_w:n_pro + n_w + n_epi]
        out_ref, a_ref = refs[-2], refs[-1]
        i = pl.program_id(0)
        j = pl.program_id(1)

        @pl.when(j == 0)
        def _():
            pro_fn(i, a_ref, *pro_refs)

        a = a_ref[...]
        ys = [jnp.dot(a, wr[...].astype(BF16), preferred_element_type=F32) for wr in w_refs]
        out_ref[...] = epi_fn(i, j, ys, *epi_refs).astype(out_dtype)

    w_specs = [
        pl.BlockSpec((None, k_dim, tn), functools.partial(lambda i, j, o: (w_layer, 0, j + o), o=off // tn))
        for off in w_col_offsets
    ]
    return pl.pallas_call(
        body,
        grid=(n_rows // tm, nj),
        in_specs=list(pro_specs) + w_specs + list(epi_specs),
        out_specs=pl.BlockSpec((tm, tn), lambda i, j: (i, j)),
        out_shape=jax.ShapeDtypeStruct((n_rows, n_out), out_dtype),
        scratch_shapes=[pltpu.VMEM((tm, k_dim), BF16)],
        compiler_params=_params(("arbitrary", "arbitrary")),
        name=name,
    )(*pro_args, *([w] * n_w), *epi_args)


def _norm_specs(layer, tm):
    return [
        pl.BlockSpec((tm, D_MODEL), lambda i, j: (i, 0), pipeline_mode=pl.Buffered(1)),
        pl.BlockSpec((None, 1, D_MODEL), lambda i, j: (layer, 0, 0)),
        pl.BlockSpec((None, 2, N_MOD, D_MODEL), lambda i, j: (layer, 0, 0, 0)),
    ]


def _res_specs(layer, tm, tn):
    return [
        pl.BlockSpec((tm, tn), lambda i, j: (i, j)),
        pl.BlockSpec((None, 2, N_MOD, tn), lambda i, j: (layer, 0, 0, j)),
    ]


def _gated_residual(i, tm, k_gate, y, h_ref, mod_ref):
    return h_ref[...] + _mod_row(mod_ref, k_gate, _is_ctx(i, tm)) * y


def _embed(x2, ctx2, row_tab, col_tab):
    seq = x2.shape[0]
    tm = 4 * GRID_W
    half = D_MODEL // 2
    n_tiles = (CTX_LEN + seq) // tm
    rows_per_tile = tm // GRID_W
    row_tab3 = row_tab.reshape(seq // tm, rows_per_tile, half)

    def body(ctx_ref, x_ref, row_ref, col_ref, out_ref):
        i = pl.program_id(0)

        @pl.when(i == 0)
        def _():
            out_ref[...] = ctx_ref[...]

        @pl.when(i > 0)
        def _():
            rt = row_ref[0]
            row_part = jnp.concatenate(
                [jnp.broadcast_to(rt[r:r + 1, :], (GRID_W, half)) for r in range(rows_per_tile)], axis=0)
            col_part = jnp.concatenate([col_ref[...]] * rows_per_tile, axis=0)
            out_ref[:, :half] = x_ref[:, :half] + row_part
            out_ref[:, half:] = x_ref[:, half:] + col_part

    assert CTX_LEN == tm
    return pl.pallas_call(
        body,
        grid=(n_tiles,),
        in_specs=[
            pl.BlockSpec((tm, D_MODEL), lambda i: (0, 0)),
            pl.BlockSpec((tm, D_MODEL), lambda i: (jnp.maximum(i - 1, 0), 0)),
            pl.BlockSpec((1, rows_per_tile, half), lambda i: (jnp.maximum(i - 1, 0), 0, 0)),
            pl.BlockSpec((GRID_W, half), lambda i: (0, 0)),
        ],
        out_specs=pl.BlockSpec((tm, D_MODEL), lambda i: (i, 0)),
        out_shape=jax.ShapeDtypeStruct((CTX_LEN + seq, D_MODEL), F32),
        compiler_params=_params(("arbitrary",)),
        name="embed",
    )(ctx2, x2, row_tab3, col_tab)


def _modulation(cond8, ada_w, ada_b):
    n = N_MOD * D_MODEL
    tn = 1024

    def body(c_ref, w_ref, b_ref, out_ref):
        a = _silu(c_ref[...]).astype(BF16)
        out_ref[...] = jnp.dot(a, w_ref[...].astype(BF16), preferred_element_type=F32) + b_ref[...]

    return pl.pallas_call(
        body,
        grid=(DEPTH, n // tn),
        in_specs=[
            pl.BlockSpec((8, D_MODEL), lambda l, j: (0, 0)),
            pl.BlockSpec((None, D_MODEL, tn), lambda l, j: (l, 0, j)),
            pl.BlockSpec((None, 1, tn), lambda l, j: (l, 0, j)),
        ],
        out_specs=pl.BlockSpec((None, 8, tn), lambda l, j: (l, 0, j)),
        out_shape=jax.ShapeDtypeStruct((DEPTH, 8, n), F32),
        compiler_params=_params(("arbitrary", "arbitrary")),
        name="modulation",
    )(cond8, ada_w, ada_b.reshape(DEPTH, 1, n))


def _s5_operators(lam_re, lam_im, log_dt, bt_re, bt_im, c_re, c_im):
    G, T, H, P, CW = S5_GROUPS, S5_CHUNK, S5_GROUP, S5_STATE, S5_CW

    def body(lre_ref, lim_ref, ldt_ref, btr_ref, bti_ref, cr_ref, ci_ref,
             mt_ref, we_ref, wof_ref, wob_ref, at_ref):
        lane_blk = lax.broadcasted_iota(jnp.int32, (CW, CW), 1) // H
        zeros64 = jnp.zeros((CW, P), F32)
        mt = jnp.zeros((CW, CW), F32)
        we_parts, wo_parts, at_parts = [], [], []
        for d in range(2):
            lr = jnp.minimum(lre_ref[d, 0], -1e-4)
            li = lim_ref[d, 0]
            dt = jnp.exp(ldt_ref[d, 0])
            mag = jnp.exp(lr * dt)
            ar = mag * jnp.cos(li * dt)
            ai = mag * jnp.sin(li * dt)
            xr = ar - 1.0
            xi = ai
            den = lr * lr + li * li
            kr = (xr * lr + xi * li) / den
            ki = (xi * lr - xr * li) / den
            btr, bti = btr_ref[d, 0], bti_ref[d, 0]
            bbr = kr * btr - ki * bti
            bbi = kr * bti + ki * btr
            cr, ci = cr_ref[d, 0], ci_ref[d, 0]
            pr = [jnp.ones((1, P), F32)]
            pi = [jnp.zeros((1, P), F32)]
            for _ in range(T):
                pr.append(pr[-1] * ar - pi[-1] * ai)
                pi.append(pr[-2] * ai + pi[-1] * ar)
            ba_re = [bbr * pr[l] - bbi * pi[l] for l in range(T)]
            ba_im = [bbr * pi[l] + bbi * pr[l] for l in range(T)]
            order = list(range(T - 1, -1, -1)) if d == 0 else list(range(T))
            e_re = jnp.concatenate([ba_re[l] for l in order], axis=0)
            e_im = jnp.concatenate([ba_im[l] for l in order], axis=0)
            cmat = jnp.concatenate([cr, -ci], axis=1)
            cmat_t = jnp.concatenate([cmat] * T, axis=0)
            kw = lax.dot_general(jnp.concatenate([e_re, e_im], axis=1), cmat_t,
                                 (((1,), (1,)), ((), ())), precision=HIGHEST,
                                 preferred_element_type=F32)
            for t in range(T):
                if d == 0:
                    sh = H * (T - 1 - t)
                    shifted = kw if sh == 0 else jnp.concatenate(
                        [kw[sh:, :], jnp.zeros((sh, CW), F32)], axis=0)
                else:
                    sh = H * t
                    shifted = kw if sh == 0 else jnp.concatenate(
                        [jnp.zeros((sh, CW), F32), kw[:CW - sh, :]], axis=0)
                mt = mt + jnp.where(lane_blk == t, shifted, 0.0)
            lags = [t + 1 for t in range(T)] if d == 0 else [T - t for t in range(T)]
            ca_re = jnp.concatenate([cr * pr[l] - ci * pi[l] for l in lags], axis=0)
            ca_im = jnp.concatenate([cr * pi[l] + ci * pr[l] for l in lags], axis=0)
            we_parts.append((e_re, e_im))
            wo_parts.append((ca_re, -ca_im))
            at_parts.append((pr[T], pi[T]))
        mt_ref[0] = mt.astype(BF16)
        we_ref[0] = jnp.concatenate(
            [we_parts[0][0], we_parts[1][0], we_parts[0][1], we_parts[1][1]], axis=1).astype(BF16)
        wof_ref[0] = jnp.concatenate(
            [wo_parts[0][0], zeros64, wo_parts[0][1], zeros64], axis=1).astype(BF16)
        wob_ref[0] = jnp.concatenate(
            [zeros64, wo_parts[1][0], zeros64, wo_parts[1][1]], axis=1).astype(BF16)
        at_ref[0] = jnp.concatenate(
            [at_parts[0][0], at_parts[1][0], at_parts[0][1], at_parts[1][1]], axis=1)

    vec_spec = pl.BlockSpec((2, 1, 1, P), lambda g: (0, g, 0, 0))
    mat_spec = pl.BlockSpec((2, 1, H, P), lambda g: (0, g, 0, 0))
    op_spec = pl.BlockSpec((1, CW, CW), lambda g: (g, 0, 0))
    op_shape = jax.ShapeDtypeStruct((G, CW, CW), BF16)
    return pl.pallas_call(
        body,
        grid=(G,),
        in_specs=[vec_spec, vec_spec, pl.BlockSpec((2, 1, 1, 1), lambda g: (0, g, 0, 0)),
                  mat_spec, mat_spec, mat_spec, mat_spec],
        out_specs=[op_spec, op_spec, op_spec, op_spec, pl.BlockSpec((1, 1, CW), lambda g: (g, 0, 0))],
        out_shape=[op_shape, op_shape, op_shape, op_shape, jax.ShapeDtypeStruct((G, 1, CW), F32)],
        compiler_params=_params(("arbitrary",)),
        name="s5_operators",
    )(lam_re.reshape(2, G, 1, P), lam_im.reshape(2, G, 1, P), log_dt.reshape(2, G, 1, 1),
      bt_re, bt_im, c_re, c_im)


S5_GB = 8


def _s5_local_state(u_r, we):
    G, nc, cw = u_r.shape

    def body(u_ref, we_ref, out_ref):
        for gi in range(S5_GB):
            out_ref[:, gi, :] = jnp.dot(u_ref[gi], we_ref[gi], preferred_element_type=F32)

    return pl.pallas_call(
        body,
        grid=(G // S5_GB,),
        in_specs=[pl.BlockSpec((S5_GB, nc, cw), lambda g: (g, 0, 0)),
                  pl.BlockSpec((S5_GB, cw, cw), lambda g: (g, 0, 0))],
        out_specs=pl.BlockSpec((nc, S5_GB, cw), lambda g: (0, g, 0)),
        out_shape=jax.ShapeDtypeStruct((nc, G, cw), F32),
        compiler_params=_params(("arbitrary",)),
        name="s5_local_state",
    )(u_r, we)


def _s5_carry(sloc_t, at):
    nc, G, cw = sloc_t.shape
    cb = CTX_LEN // S5_CHUNK
    nb = nc // cb
    half = cw // 2

    def rev_block(s):
        return jnp.where(s == 0, 0, nb - s)

    def body(sf_ref, sb_ref, at_ref, pf_ref, pb_ref, st_ref):
        s = pl.program_id(0)

        @pl.when(s == 0)
        def _():
            st_ref[...] = jnp.zeros_like(st_ref)

        is_fwd = (lax.broadcasted_iota(jnp.int32, (G, half), 1) < S5_STATE)
        a_re, a_im = at_ref[:, :half], at_ref[:, half:]
        s_re, s_im = st_ref[:, :half], st_ref[:, half:]
        for k in range(cb):
            kb = cb - 1 - k
            pf_ref[k, :, :half] = s_re
            pf_ref[k, :, half:] = s_im
            pb_ref[kb, :, :half] = s_re
            pb_ref[kb, :, half:] = s_im
            l_re = jnp.where(is_fwd, sf_ref[k, :, :half], sb_ref[kb, :, :half])
            l_im = jnp.where(is_fwd, sf_ref[k, :, half:], sb_ref[kb, :, half:])
            n_re = a_re * s_re - a_im * s_im + l_re
            n_im = a_re * s_im + a_im * s_re + l_im
            s_re, s_im = n_re, n_im
        st_ref[:, :half] = s_re
        st_ref[:, half:] = s_im

    blk = (cb, G, cw)
    return pl.pallas_call(
        body,
        grid=(nb,),
        in_specs=[pl.BlockSpec(blk, lambda s: (s, 0, 0)),
                  pl.BlockSpec(blk, lambda s: (rev_block(s), 0, 0)),
                  pl.BlockSpec((G, cw), lambda s: (0, 0))],
        out_specs=[pl.BlockSpec(blk, lambda s: (s, 0, 0)),
                   pl.BlockSpec(blk, lambda s: (rev_block(s), 0, 0))],
        out_shape=[jax.ShapeDtypeStruct((nc, G, cw), F32)] * 2,
        scratch_shapes=[pltpu.VMEM((G, cw), F32)],
        compiler_params=_params(("arbitrary",)),
        name="s5_carry",
    )(sloc_t, sloc_t, at)


def _s5_output(u_r, mt, pf, pb, wof, wob, d_tile):
    G, nc, cw = u_r.shape
    nt = (((1,), (1,)), ((), ()))

    def body(u_ref, mt_ref, pf_ref, pb_ref, wof_ref, wob_ref, d_ref, y_ref):
        for gi in range(S5_GB):
            u = u_ref[gi]
            y = jnp.dot(u, mt_ref[gi], preferred_element_type=F32)
            y = y + lax.dot_general(pf_ref[:, gi, :].astype(BF16), wof_ref[gi], nt, preferred_element_type=F32)
            y = y + lax.dot_general(pb_ref[:, gi, :].astype(BF16), wob_ref[gi], nt, preferred_element_type=F32)
            y_ref[gi] = (y + d_ref[gi] * u.astype(F32)).astype(y_ref.dtype)

    act = pl.BlockSpec((S5_GB, nc, cw), lambda g: (g, 0, 0))
    state = pl.BlockSpec((nc, S5_GB, cw), lambda g: (0, g, 0))
    op = pl.BlockSpec((S5_GB, cw, cw), lambda g: (g, 0, 0))
    return pl.pallas_call(
        body,
        grid=(G // S5_GB,),
        in_specs=[act, op, state, state, op, op, pl.BlockSpec((S5_GB, 1, cw), lambda g: (g, 0, 0))],
        out_specs=act,
        out_shape=jax.ShapeDtypeStruct((G, nc, cw), BF16),
        compiler_params=_params(("arbitrary",)),
        name="s5_output",
    )(u_r, mt, pf, pb, wof, wob, d_tile)


def _s5_layer(h, layer, j, mod, norm_g, p):
    n_rows = h.shape[0]
    G, T, H = S5_GROUPS, S5_CHUNK, S5_GROUP
    nc = n_rows // T
    tm = _pick(n_rows, MIX_ROW_TILES)
    tn = 512
    u = _fused_matmul(
        name="s5_in", n_rows=n_rows, k_dim=D_MODEL, n_out=D_MODEL, tm=tm, tn=tn,
        pro_args=(h, norm_g, mod), pro_specs=_norm_specs(layer, tm),
        pro_fn=lambda i, a_ref, h_ref, g_ref, m_ref: _norm_mod_into(a_ref, i, tm, h_ref, g_ref, m_ref, 0),
        w=p['s5_w_in'], w_layer=j, w_col_offsets=(0,),
        epi_args=(), epi_specs=(), epi_fn=lambda i, jj, ys: ys[0], out_dtype=BF16)
    u_r = u.reshape(nc, T, G, H).transpose(2, 0, 1, 3).reshape(G, nc, T * H)
    mt, we, wof, wob, at = _s5_operators(
        p['s5_lam_re'][j], p['s5_lam_im'][j], p['s5_log_dt'][j],
        jnp.swapaxes(p['s5_b_re'][j], -1, -2), jnp.swapaxes(p['s5_b_im'][j], -1, -2),
        p['s5_c_re'][j], p['s5_c_im'][j])
    sloc = _s5_local_state(u_r, we)
    pf, pb = _s5_carry(sloc, at.reshape(G, T * H))
    d_tile = jnp.tile(p['s5_d'][j].reshape(G, 1, H), (1, 1, T))
    y_r = _s5_output(u_r, mt, pf, pb, wof, wob, d_tile)
    y = y_r.reshape(G, nc, T, H).transpose(1, 2, 0, 3).reshape(n_rows, D_MODEL)

    def gelu_into(i, a_ref, y_ref):
        a_ref[...] = _gelu_tanh(y_ref[...].astype(F32)).astype(BF16)

    return _fused_matmul(
        name="s5_out", n_rows=n_rows, k_dim=D_MODEL, n_out=D_MODEL, tm=tm, tn=tn,
        pro_args=(y,), pro_specs=[pl.BlockSpec((tm, D_MODEL), lambda i, jj: (i, 0), pipeline_mode=pl.Buffered(1))],
        pro_fn=gelu_into,
        w=p['s5_w_out'], w_layer=j, w_col_offsets=(0, D_MODEL),
        epi_args=(h, mod), epi_specs=_res_specs(layer, tm, tn),
        epi_fn=lambda i, jj, ys, h_ref, m_ref: _gated_residual(
            i, tm, 2, ys[0] * _sigmoid(ys[1]), h_ref, m_ref),
        out_dtype=F32)


def _sg_layer(h, layer, j, mod, norm_g, p):
    n_rows = h.shape[0]
    tm = _pick(n_rows, MIX_ROW_TILES)
    tn = 512
    z = _fused_matmul(
        name="sg_in", n_rows=n_rows, k_dim=D_MODEL, n_out=2 * D_MODEL, tm=tm, tn=tn,
        pro_args=(h, norm_g, mod), pro_specs=_norm_specs(layer, tm),
        pro_fn=lambda i, a_ref, h_ref, g_ref, m_ref: _norm_mod_into(a_ref, i, tm, h_ref, g_ref, m_ref, 0),
        w=p['sg_w_in'], w_layer=j, w_col_offsets=(0,),
        epi_args=(), epi_specs=(), epi_fn=lambda i, jj, ys: _gelu_tanh(ys[0]), out_dtype=BF16)

    def gate(i, a_ref, u_ref, v_ref, g_ref, b_ref, ws_ref, bs_ref):
        ln_g, ln_b = g_ref[...], b_ref[...]

        def chunk(c, carry):
            rows = pl.ds(pl.multiple_of(c * SG_CHUNK, SG_CHUNK), SG_CHUNK)
            v = _layer_norm(v_ref[rows, :].astype(F32), ln_g, ln_b).astype(BF16)
            for hd in range(SG_HEADS):
                cols = slice(hd * SG_HEAD_DIM, (hd + 1) * SG_HEAD_DIM)
                sv = jnp.dot(ws_ref[hd].astype(BF16), v[:, cols], preferred_element_type=F32)
                gated = u_ref[rows, cols].astype(F32) * (sv + bs_ref[:, hd:hd + 1])
                a_ref[rows, cols] = gated.astype(BF16)
            return carry

        lax.fori_loop(0, tm // SG_CHUNK, chunk, 0)

    return _fused_matmul(
        name="sg_out", n_rows=n_rows, k_dim=D_MODEL, n_out=D_MODEL, tm=tm, tn=tn,
        pro_args=(z, z, p['sg_ln_g'], p['sg_ln_b'], p['sg_w_s'], jnp.swapaxes(p['sg_b_s'], -1, -2)),
        pro_specs=[
            pl.BlockSpec((tm, D_MODEL), lambda i, jj: (i, 0), pipeline_mode=pl.Buffered(1)),
            pl.BlockSpec((tm, D_MODEL), lambda i, jj: (i, 1), pipeline_mode=pl.Buffered(1)),
            pl.BlockSpec((1, D_MODEL), lambda i, jj: (j, 0)),
            pl.BlockSpec((1, D_MODEL), lambda i, jj: (j, 0)),
            pl.BlockSpec((None, SG_HEADS, SG_CHUNK, SG_CHUNK), lambda i, jj: (j, 0, 0, 0)),
            pl.BlockSpec((None, SG_CHUNK, SG_HEADS), lambda i, jj: (j, 0, 0)),
        ],
        pro_fn=gate,
        w=p['sg_w_out'], w_layer=j, w_col_offsets=(0,),
        epi_args=(h, mod), epi_specs=_res_specs(layer, tm, tn),
        epi_fn=lambda i, jj, ys, h_ref, m_ref: _gated_residual(i, tm, 2, ys[0], h_ref, m_ref),
        out_dtype=F32)


def _conv_norm(z, dw_w, dw_b, ln_g, ln_b, j):
    n_rows = z.shape[0]
    tm = CTX_LEN
    n_tiles = n_rows // tm
    hb = tm // CONV_HALO
    half = CONV_WIDTH // 2
    rb, cbw = 64, LANES

    def body(prev_ref, cur_ref, next_ref, w_ref, b_ref, g_ref, beta_ref, out_ref, zp_ref, acc_ref):
        i = pl.program_id(0)
        prev_ok = i >= 2
        next_ok = jnp.logical_and(i >= 1, i < n_tiles - 1)
        zp_ref[0:CONV_HALO, :] = jnp.where(prev_ok, prev_ref[...], 0.0)
        zp_ref[CONV_HALO:CONV_HALO + tm, :] = cur_ref[...]
        zp_ref[CONV_HALO + tm:, :] = jnp.where(next_ok, next_ref[...], 0.0)

        def col_block(cb, carry):
            c0 = pl.multiple_of(cb * cbw, cbw)
            w = w_ref[:, pl.ds(c0, cbw)]
            for r in range(tm // rb):
                acc = jnp.zeros((rb, cbw), F32)
                for k in range(CONV_WIDTH):
                    r0 = r * rb + CONV_HALO - half + k
                    acc = acc + zp_ref[r0:r0 + rb, pl.ds(c0, cbw)] * w[k:k + 1, :]
                acc_ref[r * rb:(r + 1) * rb, pl.ds(c0, cbw)] = acc
            return carry

        lax.fori_loop(0, D_MODEL // cbw, col_block, 0)
        y = _layer_norm(acc_ref[...] + b_ref[...], g_ref[...], beta_ref[...])
        out_ref[...] = _silu(y).astype(BF16)

    vec = pl.BlockSpec((1, D_MODEL), lambda i: (j, 0))
    return pl.pallas_call(
        body,
        grid=(n_tiles,),
        in_specs=[
            pl.BlockSpec((CONV_HALO, D_MODEL), lambda i: (jnp.maximum(i * hb - 1, 0), 0)),
            pl.BlockSpec((tm, D_MODEL), lambda i: (i, 0)),
            pl.BlockSpec((CONV_HALO, D_MODEL), lambda i: (jnp.minimum((i + 1) * hb, n_tiles * hb - 1), 0)),
            pl.BlockSpec((None, CONV_WIDTH, D_MODEL), lambda i: (j, 0, 0)),
            vec, vec, vec,
        ],
        out_specs=pl.BlockSpec((tm, D_MODEL), lambda i: (i, 0)),
        out_shape=jax.ShapeDtypeStruct((n_rows, D_MODEL), BF16),
        scratch_shapes=[pltpu.VMEM((tm + 2 * CONV_HALO, D_MODEL), F32), pltpu.VMEM((tm, D_MODEL), F32)],
        compiler_params=_params(("arbitrary",)),
        name="conv_norm",
    )(z, z, z, dw_w, dw_b, ln_g, ln_b)


def _conv_layer(h, layer, j, mod, norm_g, p):
    n_rows = h.shape[0]
    tm = _pick(n_rows, MIX_ROW_TILES)
    tn = 512
    z = _fused_matmul(
        name="cv_pw1", n_rows=n_rows, k_dim=D_MODEL, n_out=D_MODEL, tm=tm, tn=tn,
        pro_args=(h, norm_g, mod), pro_specs=_norm_specs(layer, tm),
        pro_fn=lambda i, a_ref, h_ref, g_ref, m_ref: _norm_mod_into(a_ref, i, tm, h_ref, g_ref, m_ref, 0),
        w=p['cv_w_pw1'], w_layer=j, w_col_offsets=(0, D_MODEL),
        epi_args=(), epi_specs=(), epi_fn=lambda i, jj, ys: ys[0] * _sigmoid(ys[1]), out_dtype=F32)
    zc = _conv_norm(z, p['cv_dw_w'], p['cv_dw_b'], p['cv_ln_g'], p['cv_ln_b'], j)

    def copy_into(i, a_ref, z_ref):
        a_ref[...] = z_ref[...]

    return _fused_matmul(
        name="cv_pw2", n_rows=n_rows, k_dim=D_MODEL, n_out=D_MODEL, tm=tm, tn=tn,
        pro_args=(zc,), pro_specs=[pl.BlockSpec((tm, D_MODEL), lambda i, jj: (i, 0), pipeline_mode=pl.Buffered(1))],
        pro_fn=copy_into,
        w=p['cv_w_pw2'], w_layer=j, w_col_offsets=(0,),
        epi_args=(h, mod), epi_specs=_res_specs(layer, tm, tn),
        epi_fn=lambda i, jj, ys, h_ref, m_ref: _gated_residual(i, tm, 2, ys[0], h_ref, m_ref),
        out_dtype=F32)


SLAB_RANK1, SLAB_RANK2, SLAB_EXP1, SLAB_EXP2, SLAB_W1, SLAB_W2 = range(6)


def _router(h, layer, k, mod, norm_g, w_router):
    n_rows = h.shape[0]
    tm = _pick(n_rows, (768, 256))
    wr = jnp.pad(w_router[k], ((0, 0), (0, LANES - N_EXPERTS)))

    def body(h_ref, g_ref, m_ref, wr_ref, hn_ref, slab_ref, cnt_ref, base_ref):
        i = pl.program_id(0)

        @pl.when(i == 0)
        def _():
            base_ref[...] = jnp.zeros_like(base_ref)

        hn = _norm_mod(i, tm, h_ref, g_ref, m_ref, 3)
        hn_ref[...] = hn
        logits = jnp.dot(hn, wr_ref[...], precision=HIGHEST, preferred_element_type=F32)
        lane = lax.broadcasted_iota(jnp.int32, (tm, LANES), 1).astype(F32)
        neg = jnp.float32(-jnp.inf)
        lg = jnp.where(lane < N_EXPERTS, logits, neg)
        m1 = jnp.max(lg, axis=1, keepdims=True)
        i1 = jnp.min(jnp.where(lg == m1, lane, float(LANES)), axis=1, keepdims=True)
        lg2 = jnp.where(lane == i1, neg, lg)
        m2 = jnp.max(lg2, axis=1, keepdims=True)
        i2 = jnp.min(jnp.where(lg2 == m2, lane, float(LANES)), axis=1, keepdims=True)
        e = jnp.exp(m2 - m1)
        w1 = 1.0 / (1.0 + e)
        w2 = e / (1.0 + e)
        onehot = jnp.where(lane == i1, 1.0, 0.0) + jnp.where(lane == i2, 1.0, 0.0)
        tri = jnp.where(lax.broadcasted_iota(jnp.int32, (tm, tm), 1)
                        < lax.broadcasted_iota(jnp.int32, (tm, tm), 0), 1.0, 0.0).astype(BF16)
        base = base_ref[0:1, :]
        excl = jnp.dot(tri, onehot.astype(BF16), preferred_element_type=F32) + base
        r1 = jnp.sum(jnp.where(lane == i1, excl, 0.0), axis=1, keepdims=True)
        r2 = jnp.sum(jnp.where(lane == i2, excl, 0.0), axis=1, keepdims=True)
        total = base + jnp.sum(onehot, axis=0, keepdims=True)
        base_ref[...] = jnp.broadcast_to(total, base_ref.shape)
        cnt_ref[...] = jnp.broadcast_to(total, cnt_ref.shape)
        slab = jnp.zeros((tm, LANES), F32)
        for ln, val in ((SLAB_RANK1, r1), (SLAB_RANK2, r2), (SLAB_EXP1, i1), (SLAB_EXP2, i2),
                        (SLAB_W1, w1), (SLAB_W2, w2)):
            slab = jnp.where(lane == float(ln), val, slab)
        slab_ref[...] = slab

    return pl.pallas_call(
        body,
        grid=(n_rows // tm,),
        in_specs=[
            pl.BlockSpec((tm, D_MODEL), lambda i: (i, 0)),
            pl.BlockSpec((None, 1, D_MODEL), lambda i: (layer, 0, 0)),
            pl.BlockSpec((None, 2, N_MOD, D_MODEL), lambda i: (layer, 0, 0, 0)),
            pl.BlockSpec((D_MODEL, LANES), lambda i: (0, 0)),
        ],
        out_specs=[pl.BlockSpec((tm, D_MODEL), lambda i: (i, 0)),
                   pl.BlockSpec((tm, LANES), lambda i: (i, 0)),
                   pl.BlockSpec((8, LANES), lambda i: (0, 0))],
        out_shape=[jax.ShapeDtypeStruct((n_rows, D_MODEL), F32),
                   jax.ShapeDtypeStruct((n_rows, LANES), F32),
                   jax.ShapeDtypeStruct((8, LANES), F32)],
        scratch_shapes=[pltpu.VMEM((8, LANES), F32)],
        compiler_params=_params(("arbitrary",)),
        name="router",
    )(h, norm_g, mod, wr)


def _routing_tables(slab, counts, tb, nblk):
    cnt = counts[0, :N_EXPERTS].astype(jnp.int32)
    padded = ((cnt + tb - 1) // tb) * tb
    ends = jnp.cumsum(padded)
    starts = ends - padded
    pos1 = starts[slab[:, SLAB_EXP1].astype(jnp.int32)] + slab[:, SLAB_RANK1].astype(jnp.int32)
    pos2 = starts[slab[:, SLAB_EXP2].astype(jnp.int32)] + slab[:, SLAB_RANK2].astype(jnp.int32)
    nvalid = ends[-1] // tb
    blk = jnp.minimum(jnp.arange(nblk, dtype=jnp.int32), nvalid - 1)
    blk_expert = jnp.minimum(jnp.searchsorted(ends, blk * tb, side='right'), N_EXPERTS - 1)
    return pos1, pos2, blk_expert.astype(jnp.int32), nvalid.reshape(1).astype(jnp.int32)


def _dispatch(hn, pos1, pos2, n_slots):
    n_rows = hn.shape[0]
    tm = CTX_LEN

    def body(p1_ref, p2_ref, hn_ref, init_ref, xs_ref, sem):
        i = pl.program_id(0)

        def row_copy(t, p):
            return pltpu.make_async_copy(hn_ref.at[pl.ds(t, 1)], xs_ref.at[pl.ds(p, 1)], sem)

        def issue(t, carry):
            row_copy(t, p1_ref[i * tm + t]).start()
            row_copy(t, p2_ref[i * tm + t]).start()
            return carry

        def drain(t, carry):
            row_copy(0, 0).wait()
            row_copy(0, 0).wait()
            return carry

        lax.fori_loop(0, tm, issue, 0)
        lax.fori_loop(0, tm, drain, 0)

    return pl.pallas_call(
        body,
        grid_spec=pltpu.PrefetchScalarGridSpec(
            num_scalar_prefetch=2,
            grid=(n_rows // tm,),
            in_specs=[pl.BlockSpec((tm, D_MODEL), lambda i, p1, p2: (i, 0)),
                      pl.BlockSpec(memory_space=pl.ANY)],
            out_specs=pl.BlockSpec(memory_space=pl.ANY),
            scratch_shapes=[pltpu.SemaphoreType.DMA],
        ),
        out_shape=jax.ShapeDtypeStruct((n_slots, D_MODEL), F32),
        input_output_aliases={3: 0},
        compiler_params=_params(("arbitrary",)),
        name="moe_dispatch",
    )(pos1, pos2, hn, jnp.zeros((n_slots, D_MODEL), F32))


def _swiglu_chunk(a, wg_ref, wu_ref, wd_ref):
    g = jnp.dot(a, wg_ref[...].astype(BF16), preferred_element_type=F32)
    u = jnp.dot(a, wu_ref[...].astype(BF16), preferred_element_type=F32)
    return jnp.dot((_silu(g) * u).astype(BF16), wd_ref[...].astype(BF16), preferred_element_type=F32)


def _moe_ffn(xs, blk_expert, nvalid, w_gate, w_up, w_down, e_base, tb):
    n_slots = xs.shape[0]
    d_ff = w_gate.shape[-1]
    tf = FFN_CHUNK
    nf = d_ff // tf

    def body(be_ref, nv_ref, x_ref, wg_ref, wu_ref, wd_ref, out_ref, a_ref):
        b = pl.program_id(0)
        f = pl.program_id(1)
        used = b < nv_ref[0]

        @pl.when(jnp.logical_and(used, f == 0))
        def _():
            a_ref[...] = x_ref[...].astype(BF16)
            out_ref[...] = jnp.zeros_like(out_ref)

        @pl.when(used)
        def _():
            out_ref[...] += _swiglu_chunk(a_ref[...], wg_ref, wu_ref, wd_ref)

        @pl.when(jnp.logical_and(jnp.logical_not(used), f == 0))
        def _():
            out_ref[...] = jnp.zeros_like(out_ref)

    def f_idx(b, f, nv):
        return jnp.where(b < nv[0], f, nf - 1)

    once = pl.Buffered(1)
    rows = lambda b, f, be, nv: (jnp.minimum(b, nv[0] - 1), 0)
    return pl.pallas_call(
        body,
        grid_spec=pltpu.PrefetchScalarGridSpec(
            num_scalar_prefetch=2,
            grid=(n_slots // tb, nf),
            in_specs=[
                pl.BlockSpec((tb, D_MODEL), rows, pipeline_mode=once),
                pl.BlockSpec((None, D_MODEL, tf), lambda b, f, be, nv: (e_base + be[b], 0, f_idx(b, f, nv))),
                pl.BlockSpec((None, D_MODEL, tf), lambda b, f, be, nv: (e_base + be[b], 0, f_idx(b, f, nv))),
                pl.BlockSpec((None, tf, D_MODEL), lambda b, f, be, nv: (e_base + be[b], f_idx(b, f, nv), 0)),
            ],
            out_specs=pl.BlockSpec((tb, D_MODEL), lambda b, f, be, nv: (b, 0), pipeline_mode=once),
            scratch_shapes=[pltpu.VMEM((tb, D_MODEL), BF16)],
        ),
        out_shape=jax.ShapeDtypeStruct((n_slots, D_MODEL), F32),
        compiler_params=_params(("arbitrary", "arbitrary")),
        name="moe_ffn",
    )(blk_expert, nvalid, xs, w_gate, w_up, w_down)


def _combine(h, ys, slab, pos1, pos2, layer, mod):
    n_rows = h.shape[0]
    tm = CTX_LEN

    def body(p1_ref, p2_ref, h_ref, slab_ref, m_ref, ys_ref, out_ref, y1_ref, y2_ref, sem):
        i = pl.program_id(0)

        def row_copy(p, dst_ref, t):
            return pltpu.make_async_copy(ys_ref.at[pl.ds(p, 1)], dst_ref.at[pl.ds(t, 1)], sem)

        def issue(t, carry):
            row_copy(p1_ref[i * tm + t], y1_ref, t).start()
            row_copy(p2_ref[i * tm + t], y2_ref, t).start()
            return carry

        def drain(t, carry):
            row_copy(0, y1_ref, 0).wait()
            row_copy(0, y2_ref, 0).wait()
            return carry

        lax.fori_loop(0, tm, issue, 0)
        lax.fori_loop(0, tm, drain, 0)
        w1 = slab_ref[:, SLAB_W1:SLAB_W1 + 1]
        w2 = slab_ref[:, SLAB_W2:SLAB_W2 + 1]
        moe = w1 * y1_ref[...] + w2 * y2_ref[...]
        out_ref[...] = h_ref[...] + _mod_row(m_ref, 5, _is_ctx(i, tm)) * moe

    return pl.pallas_call(
        body,
        grid_spec=pltpu.PrefetchScalarGridSpec(
            num_scalar_prefetch=2,
            grid=(n_rows // tm,),
            in_specs=[
                pl.BlockSpec((tm, D_MODEL), lambda i, p1, p2: (i, 0)),
                pl.BlockSpec((tm, LANES), lambda i, p1, p2: (i, 0)),
                pl.BlockSpec((None, 2, N_MOD, D_MODEL), lambda i, p1, p2: (layer, 0, 0, 0)),
                pl.BlockSpec(memory_space=pl.ANY),
            ],
            out_specs=pl.BlockSpec((tm, D_MODEL), lambda i, p1, p2: (i, 0)),
            scratch_shapes=[pltpu.VMEM((tm, D_MODEL), F32), pltpu.VMEM((tm, D_MODEL), F32),
                            pltpu.SemaphoreType.DMA],
        ),
        out_shape=jax.ShapeDtypeStruct((n_rows, D_MODEL), F32),
        compiler_params=_params(("arbitrary",)),
        name="moe_combine",
    )(pos1, pos2, h, slab, mod, ys)


def _moe_layer(h, layer, k, mod, norm_g, w_router, w_gate, w_up, w_down):
    n_rows = h.shape[0]
    d_ff = w_gate.shape[-1]
    tb = 1024 if n_rows >= 4096 else 256
    nblk = -(-(TOP_K * n_rows + N_EXPERTS * (tb - 1)) // tb)
    hn, slab, counts = _router(h, layer, k, mod, norm_g, w_router)
    pos1, pos2, blk_expert, nvalid = _routing_tables(slab, counts, tb, nblk)
    xs = _dispatch(hn, pos1, pos2, nblk * tb)
    ys = _moe_ffn(xs, blk_expert, nvalid, w_gate.reshape(-1, D_MODEL, d_ff), w_up.reshape(-1, D_MODEL, d_ff),
                  w_down.reshape(-1, d_ff, D_MODEL), k * N_EXPERTS, tb)
    return _combine(h, ys, slab, pos1, pos2, layer, mod)


def _ffn(h, layer, mod, norm_g, w_gate, w_up, w_down, k):
    n_rows = h.shape[0]
    d_ff = w_gate.shape[-1]
    tm = _pick(n_rows, (768, 256))
    tf = FFN_CHUNK
    nf = d_ff // tf

    def body(h_ref, g_ref, m_ref, wg_ref, wu_ref, wd_ref, out_ref, a_ref):
        i = pl.program_id(0)
        f = pl.program_id(1)

        @pl.when(f == 0)
        def _():
            _norm_mod_into(a_ref, i, tm, h_ref, g_ref, m_ref, 3)
            out_ref[...] = jnp.zeros_like(out_ref)

        out_ref[...] += _swiglu_chunk(a_ref[...], wg_ref, wu_ref, wd_ref)

        @pl.when(f == nf - 1)
        def _():
            def chunk(r, carry):
                rows = pl.ds(pl.multiple_of(r * MOD_CHUNK, MOD_CHUNK), MOD_CHUNK)
                m = jnp.where(i * tm + r * MOD_CHUNK < CTX_LEN, 1, 0)
                out_ref[rows, :] = h_ref[rows, :] + m_ref[m, pl.ds(5, 1), :] * out_ref[rows, :]
                return carry

            lax.fori_loop(0, tm // MOD_CHUNK, chunk, 0)

    once = pl.Buffered(1)
    return pl.pallas_call(
        body,
        grid=(n_rows // tm, nf),
        in_specs=[
            pl.BlockSpec((tm, D_MODEL), lambda i, f: (i, 0), pipeline_mode=once),
            pl.BlockSpec((None, 1, D_MODEL), lambda i, f: (layer, 0, 0)),
            pl.BlockSpec((None, 2, N_MOD, D_MODEL), lambda i, f: (layer, 0, 0, 0)),
            pl.BlockSpec((None, D_MODEL, tf), lambda i, f: (k, 0, f)),
            pl.BlockSpec((None, D_MODEL, tf), lambda i, f: (k, 0, f)),
            pl.BlockSpec((None, tf, D_MODEL), lambda i, f: (k, f, 0)),
        ],
        out_specs=pl.BlockSpec((tm, D_MODEL), lambda i, f: (i, 0), pipeline_mode=once),
        out_shape=jax.ShapeDtypeStruct((n_rows, D_MODEL), F32),
        scratch_shapes=[pltpu.VMEM((tm, D_MODEL), BF16)],
        compiler_params=_params(("arbitrary", "arbitrary")),
        name="ffn",
    )(h, norm_g, mod, w_gate, w_up, w_down)


def _final_norm(h, final_g):
    n_rows = h.shape[0]
    tm = CTX_LEN

    def body(h_ref, g_ref, out_ref):
        x = h_ref[...]
        out_ref[...] = x * lax.rsqrt(jnp.mean(x * x, axis=-1, keepdims=True) + EPS) * g_ref[...]

    return pl.pallas_call(
        body,
        grid=((n_rows - CTX_LEN) // tm,),
        in_specs=[pl.BlockSpec((tm, D_MODEL), lambda i: (i + 1, 0)),
                  pl.BlockSpec((1, D_MODEL), lambda i: (0, 0))],
        out_specs=pl.BlockSpec((tm, D_MODEL), lambda i: (i, 0)),
        out_shape=jax.ShapeDtypeStruct((n_rows - CTX_LEN, D_MODEL), F32),
        compiler_params=_params(("arbitrary",)),
        name="final_norm",
    )(h, final_g.reshape(1, D_MODEL))


def _sincos_tables(rows):
    quarter = D_MODEL // 4
    omega = 1.0 / (POS_BASE ** (jnp.arange(quarter, dtype=F32) / quarter))

    def emb(pos):
        ang = pos[:, None] * omega[None, :]
        return jnp.concatenate([jnp.sin(ang), jnp.cos(ang)], axis=-1)

    return emb(jnp.arange(rows, dtype=F32)), emb(jnp.arange(GRID_W, dtype=F32))


def kernel(x, c, ctx, c_ctx, ada_w, ada_b, norm_mix_g, norm_ffn_g, final_g, s5_w_in, s5_lam_re, s5_lam_im, s5_log_dt, s5_b_re, s5_b_im, s5_c_re, s5_c_im, s5_d, s5_w_out, sg_w_in, sg_ln_g, sg_ln_b, sg_w_s, sg_b_s, sg_w_out, cv_w_pw1, cv_dw_w, cv_dw_b, cv_ln_g, cv_ln_b, cv_w_pw2, ff_w_gate, ff_w_up, ff_w_down, moe_w_router, moe_w_gate, moe_w_up, moe_w_down):
    bsz, seq, dim = x.shape
    assert bsz == 1 and dim == D_MODEL and ctx.shape == (1, CTX_LEN, D_MODEL)
    p = dict(s5_w_in=s5_w_in, s5_lam_re=s5_lam_re, s5_lam_im=s5_lam_im, s5_log_dt=s5_log_dt,
             s5_b_re=s5_b_re, s5_b_im=s5_b_im, s5_c_re=s5_c_re, s5_c_im=s5_c_im, s5_d=s5_d,
             s5_w_out=s5_w_out, sg_w_in=sg_w_in, sg_ln_g=sg_ln_g, sg_ln_b=sg_ln_b, sg_w_s=sg_w_s,
             sg_b_s=sg_b_s, sg_w_out=sg_w_out, cv_w_pw1=cv_w_pw1, cv_dw_w=cv_dw_w, cv_dw_b=cv_dw_b,
             cv_ln_g=cv_ln_g, cv_ln_b=cv_ln_b, cv_w_pw2=cv_w_pw2)
    row_tab, col_tab = _sincos_tables(seq // GRID_W)
    h = _embed(x[0], ctx[0], row_tab, col_tab)
    cond8 = jnp.concatenate([c, c_ctx[None, :], jnp.zeros((6, D_MODEL), F32)], axis=0)
    mod = _modulation(cond8, ada_w, ada_b)[:, :2, :].reshape(DEPTH, 2, N_MOD, D_MODEL)
    g_mix = norm_mix_g.reshape(DEPTH, 1, D_MODEL)
    g_ffn = norm_ffn_g.reshape(DEPTH, 1, D_MODEL)
    for layer in range(DEPTH):
        kind, j = layer % N_MIXERS, layer // N_MIXERS
        if kind == 0:
            h = _s5_layer(h, layer, j, mod, g_mix, p)
        elif kind == 1:
            h = _sg_layer(h, layer, j, mod, g_mix, p)
        else:
            h = _conv_layer(h, layer, j, mod, g_mix, p)
        k = layer // 2
        if layer % 2 == 0:
            h = _ffn(h, layer, mod, g_ffn, ff_w_gate, ff_w_up, ff_w_down, k)
        else:
            h = _moe_layer(h, layer, k, mod, g_ffn, moe_w_router, moe_w_gate, moe_w_up, moe_w_down)
    return _final_norm(h, final_g)[None]
```

```python
import functools
import math

import jax
import jax.numpy as jnp
from jax import lax
from jax.experimental import pallas as pl
from jax.experimental.pallas import tpu as pltpu

F32 = jnp.float32
BF16 = jnp.bfloat16
HIGHEST = lax.Precision.HIGHEST

D_MODEL = 2048
DEPTH = 4
GRID_W = 64
CTX_LEN = 256
POS_BASE = 10000.0
N_MIXERS = 3
N_MOD = 6
EPS = 1e-6
S5_GROUP = 16
S5_GROUPS = D_MODEL // S5_GROUP
S5_STATE = 64
S5_CHUNK = 16
S5_CW = S5_CHUNK * S5_GROUP
SG_CHUNK = 128
SG_HEADS = 8
SG_HEAD_DIM = D_MODEL // SG_HEADS
CONV_WIDTH = 31
CONV_HALO = 16
N_EXPERTS = 8
TOP_K = 2
LANES = 128
SUBLANES = 8
S5_LB_GROUPS = LANES // S5_GROUP
MOD_CHUNK = 128
FFN_CHUNK = 512
MIX_ROW_TILES = (1408, 768, 256)
VMEM_LIMIT = 56 * 1024 * 1024


def _pick(n, candidates):
    for c in candidates:
        if n % c == 0:
            return c
    raise ValueError(f"no tile in {candidates} divides {n}")


def _params(sem):
    return pltpu.CompilerParams(dimension_semantics=sem, vmem_limit_bytes=VMEM_LIMIT)


def _sigmoid(x):
    return 1.0 / (1.0 + jnp.exp(-x))


def _silu(x):
    return x * _sigmoid(x)


def _gelu_tanh(x):
    return 0.5 * x * (1.0 + jnp.tanh(math.sqrt(2.0 / math.pi) * (x + 0.044715 * (x * x * x))))


def _is_ctx(i, tm):
    rows = i * tm + lax.broadcasted_iota(jnp.int32, (tm, 1), 0)
    return rows < CTX_LEN


def _mod_row(mod_ref, k, is_ctx):
    return jnp.where(is_ctx, mod_ref[1, k:k + 1, :], mod_ref[0, k:k + 1, :])


def _norm_mod_rows(row0, h, g, mod_ref, k_shift):
    inv = lax.rsqrt(jnp.mean(h * h, axis=-1, keepdims=True) + EPS)
    hn = (h * inv) * g
    is_ctx = (row0 + lax.broadcasted_iota(jnp.int32, (h.shape[0], 1), 0)) < CTX_LEN
    return hn * (1.0 + _mod_row(mod_ref, k_shift + 1, is_ctx)) + _mod_row(mod_ref, k_shift, is_ctx)


def _norm_mod(i, tm, h_ref, g_ref, mod_ref, k_shift):
    return _norm_mod_rows(i * tm, h_ref[...], g_ref[...], mod_ref, k_shift)


def _norm_mod_into(a_ref, i, tm, h_ref, g_ref, mod_ref, k_shift):
    assert tm % MOD_CHUNK == 0 and CTX_LEN % MOD_CHUNK == 0
    g = g_ref[...]

    def chunk(r, carry):
        rows = pl.ds(pl.multiple_of(r * MOD_CHUNK, MOD_CHUNK), MOD_CHUNK)
        m = jnp.where(i * tm + r * MOD_CHUNK < CTX_LEN, 1, 0)
        h = h_ref[rows, :]
        inv = lax.rsqrt(jnp.mean(h * h, axis=-1, keepdims=True) + EPS)
        scale = mod_ref[m, pl.ds(k_shift + 1, 1), :]
        shift = mod_ref[m, pl.ds(k_shift, 1), :]
        a_ref[rows, :] = (((h * inv) * g) * (1.0 + scale) + shift).astype(a_ref.dtype)
        return carry

    lax.fori_loop(0, tm // MOD_CHUNK, chunk, 0)


def _layer_norm(v, g, b):
    mu = jnp.mean(v, axis=-1, keepdims=True)
    vc = v - mu
    var = jnp.mean(vc * vc, axis=-1, keepdims=True)
    return vc * lax.rsqrt(var + EPS) * g + b


def _fused_matmul(*, name, n_rows, k_dim, n_out, tm, tn, pro_args, pro_specs, pro_fn,
                  w, w_layer, w_col_offsets, epi_args, epi_specs, epi_fn, out_dtype):
    n_pro, n_w, n_epi = len(pro_args), len(w_col_offsets), len(epi_args)
    nj = n_out // tn

    def body(*refs):
        pro_refs = refs[:n_pro]
        w_refs = refs[n_pro:n_pro + n_w]
        epi_refs = refs[n_pro + n_w:n_pro + n_w + n_epi]
        out_ref, a_ref = refs[-2], refs[-1]
        i = pl.program_id(0)
        j = pl.program_id(1)

        @pl.when(j == 0)
        def _():
            pro_fn(i, a_ref, *pro_refs)

        a = a_ref[...]
        ys = [jnp.dot(a, wr[...].astype(BF16), preferred_element_type=F32) for wr in w_refs]
        out_ref[...] = epi_fn(i, j, ys, *epi_refs).astype(out_dtype)

    w_specs = [
        pl.BlockSpec((None, k_dim, tn), functools.partial(lambda i, j, o: (w_layer, 0, j + o), o=off // tn))
        for off in w_col_offsets
    ]
    return pl.pallas_call(
        body,
        grid=(n_rows // tm, nj),
        in_specs=list(pro_specs) + w_specs + list(epi_specs),
        out_specs=pl.BlockSpec((tm, tn), lambda i, j: (i, j)),
        out_shape=jax.ShapeDtypeStruct((n_rows, n_out), out_dtype),
        scratch_shapes=[pltpu.VMEM((tm, k_dim), BF16)],
        compiler_params=_params(("arbitrary", "arbitrary")),
        name=name,
    )(*pro_args, *([w] * n_w), *epi_args)


def _norm_specs(layer, tm):
    return [
        pl.BlockSpec((tm, D_MODEL), lambda i, j: (i, 0), pipeline_mode=pl.Buffered(1)),
        pl.BlockSpec((None, 1, D_MODEL), lambda i, j: (layer, 0, 0)),
        pl.BlockSpec((None, 2, N_MOD, D_MODEL), lambda i, j: (layer, 0, 0, 0)),
    ]


def _res_specs(layer, tm, tn):
    return [
        pl.BlockSpec((tm, tn), lambda i, j: (i, j)),
        pl.BlockSpec((None, 2, N_MOD, tn), lambda i, j: (layer, 0, 0, j)),
    ]


def _gated_residual(i, tm, k_gate, y, h_ref, mod_ref):
    return h_ref[...] + _mod_row(mod_ref, k_gate, _is_ctx(i, tm)) * y


def _embed(x2, ctx2, row_tab, col_tab):
    seq = x2.shape[0]
    tm = 4 * GRID_W
    half = D_MODEL // 2
    n_tiles = (CTX_LEN + seq) // tm
    rows_per_tile = tm // GRID_W
    row_tab3 = row_tab.reshape(seq // tm, rows_per_tile, half)

    def body(ctx_ref, x_ref, row_ref, col_ref, out_ref):
        i = pl.program_id(0)

        @pl.when(i == 0)
        def _():
            out_ref[...] = ctx_ref[...]

        @pl.when(i > 0)
        def _():
            rt = row_ref[0]
            row_part = jnp.concatenate(
                [jnp.broadcast_to(rt[r:r + 1, :], (GRID_W, half)) for r in range(rows_per_tile)], axis=0)
            col_part = jnp.concatenate([col_ref[...]] * rows_per_tile, axis=0)
            out_ref[:, :half] = x_ref[:, :half] + row_part
            out_ref[:, half:] = x_ref[:, half:] + col_part

    assert CTX_LEN == tm
    return pl.pallas_call(
        body,
        grid=(n_tiles,),
        in_specs=[
            pl.BlockSpec((tm, D_MODEL), lambda i: (0, 0)),
            pl.BlockSpec((tm, D_MODEL), lambda i: (jnp.maximum(i - 1, 0), 0)),
            pl.BlockSpec((1, rows_per_tile, half), lambda i: (jnp.maximum(i - 1, 0), 0, 0)),
            pl.BlockSpec((GRID_W, half), lambda i: (0, 0)),
        ],
        out_specs=pl.BlockSpec((tm, D_MODEL), lambda i: (i, 0)),
        out_shape=jax.ShapeDtypeStruct((CTX_LEN + seq, D_MODEL), F32),
        compiler_params=_params(("arbitrary",)),
        name="embed",
    )(ctx2, x2, row_tab3, col_tab)


def _modulation(cond8, ada_w, ada_b):
    n = N_MOD * D_MODEL
    tn = 1024

    def body(c_ref, w_ref, b_ref, out_ref):
        a = _silu(c_ref[...]).astype(BF16)
        out_ref[...] = jnp.dot(a, w_ref[...].astype(BF16), preferred_element_type=F32) + b_ref[...]

    return pl.pallas_call(
        body,
        grid=(DEPTH, n // tn),
        in_specs=[
            pl.BlockSpec((8, D_MODEL), lambda l, j: (0, 0)),
            pl.BlockSpec((None, D_MODEL, tn), lambda l, j: (l, 0, j)),
            pl.BlockSpec((None, 1, tn), lambda l, j: (l, 0, j)),
        ],
        out_specs=pl.BlockSpec((None, 8, tn), lambda l, j: (l, 0, j)),
        out_shape=jax.ShapeDtypeStruct((DEPTH, 8, n), F32),
        compiler_params=_params(("arbitrary", "arbitrary")),
        name="modulation",
    )(cond8, ada_w, ada_b.reshape(DEPTH, 1, n))


def _s5_operators(lam_re, lam_im, log_dt, bt_re, bt_im, c_re, c_im):
    G, T, H, P, CW = S5_GROUPS, S5_CHUNK, S5_GROUP, S5_STATE, S5_CW

    def body(lre_ref, lim_ref, ldt_ref, btr_ref, bti_ref, cr_ref, ci_ref,
             kd_ref, we_ref, wo_ref, at_ref):
        slot = pl.program_id(0) % S5_LB_GROUPS
        row_blk = lax.broadcasted_iota(jnp.int32, (LANES, 2 * P), 0) // H
        kn, we_parts, wo_parts, at_parts = [], [], [], []
        for d in range(2):
            lr = jnp.minimum(lre_ref[d, 0], -1e-4)
            li = lim_ref[d, 0]
            dt = jnp.exp(ldt_ref[d, 0])
            mag = jnp.exp(lr * dt)
            ar = mag * jnp.cos(li * dt)
            ai = mag * jnp.sin(li * dt)
            xr = ar - 1.0
            xi = ai
            den = lr * lr + li * li
            kr = (xr * lr + xi * li) / den
            ki = (xi * lr - xr * li) / den
            btr, bti = btr_ref[d, 0], bti_ref[d, 0]
            bbr = kr * btr - ki * bti
            bbi = kr * bti + ki * btr
            cr, ci = cr_ref[d, 0], ci_ref[d, 0]
            pr = [jnp.ones((1, P), F32)]
            pi = [jnp.zeros((1, P), F32)]
            for _ in range(T):
                pr.append(pr[-1] * ar - pi[-1] * ai)
                pi.append(pr[-2] * ai + pi[-1] * ar)
            ba_re = [bbr * pr[l] - bbi * pi[l] for l in range(T)]
            ba_im = [bbr * pi[l] + bbi * pr[l] for l in range(T)]
            rising = list(range(T))
            falling = rising[::-1]
            e_order = falling if d == 0 else rising
            e_re = jnp.concatenate([ba_re[l] for l in e_order], axis=0)
            e_im = jnp.concatenate([ba_im[l] for l in e_order], axis=0)
            k_order = rising if d == 0 else falling
            k_lhs = jnp.concatenate([jnp.concatenate([ba_re[l] for l in k_order], axis=0),
                                     jnp.concatenate([ba_im[l] for l in k_order], axis=0)], axis=1)
            cmat = jnp.concatenate([cr, -ci], axis=1)
            cmat_slot = jnp.where(row_blk == slot, jnp.concatenate([cmat] * S5_LB_GROUPS, axis=0), 0.0)
            kn.append(lax.dot_general(k_lhs, cmat_slot, (((1,), (1,)), ((), ())), precision=HIGHEST,
                                      preferred_element_type=F32))
            lags = [t + 1 for t in range(T)] if d == 0 else [T - t for t in range(T)]
            ca_re = jnp.concatenate([cr * pr[l] - ci * pi[l] for l in lags], axis=0)
            ca_im = jnp.concatenate([cr * pi[l] + ci * pr[l] for l in lags], axis=0)
            we_parts.append((e_re, e_im))
            wo_parts.append((ca_re, -ca_im))
            at_parts.append((pr[T], pi[T]))
        kf, kb = kn
        kd_ref[0] = jnp.concatenate(
            [kb[:CW - H], kb[CW - H:] + kf[:H], kf[H:], jnp.zeros((H, LANES), F32)], axis=0).astype(BF16)
        we_ref[0] = jnp.concatenate(
            [we_parts[0][0], we_parts[1][0], we_parts[0][1], we_parts[1][1]], axis=1).astype(BF16)
        wo_ref[0] = jnp.concatenate(
            [wo_parts[0][0], wo_parts[1][0], wo_parts[0][1], wo_parts[1][1]], axis=1).astype(BF16)
        at_ref[0] = jnp.concatenate(
            [at_parts[0][0], at_parts[1][0], at_parts[0][1], at_parts[1][1]], axis=1)

    vec_spec = pl.BlockSpec((2, 1, 1, P), lambda g: (0, g, 0, 0))
    mat_spec = pl.BlockSpec((2, 1, H, P), lambda g: (0, g, 0, 0))
    op_spec = pl.BlockSpec((1, CW, CW), lambda g: (g, 0, 0))
    op_shape = jax.ShapeDtypeStruct((G, CW, CW), BF16)
    return pl.pallas_call(
        body,
        grid=(G,),
        in_specs=[vec_spec, vec_spec, pl.BlockSpec((2, 1, 1, 1), lambda g: (0, g, 0, 0)),
                  mat_spec, mat_spec, mat_spec, mat_spec],
        out_specs=[pl.BlockSpec((1, 2 * CW, LANES), lambda g: (g, 0, 0)), op_spec, op_spec,
                   pl.BlockSpec((1, 1, CW), lambda g: (g, 0, 0))],
        out_shape=[jax.ShapeDtypeStruct((G, 2 * CW, LANES), BF16), op_shape, op_shape,
                   jax.ShapeDtypeStruct((G, 1, CW), F32)],
        compiler_params=_params(("arbitrary",)),
        name="s5_operators",
    )(lam_re.reshape(2, G, 1, P), lam_im.reshape(2, G, 1, P), log_dt.reshape(2, G, 1, 1),
      bt_re, bt_im, c_re, c_im)


def _s5_gather_chunks(u2_ref, x_ref, sem, lb):
    copies = [pltpu.make_async_copy(u2_ref.at[:, pl.ds(s * D_MODEL + lb * LANES, LANES)],
                                    x_ref.at[:, pl.ds(s * LANES, LANES)], sem)
              for s in range(S5_CHUNK)]
    for cp in copies:
        cp.start()
    for cp in copies:
        cp.wait()


def _s5_block_diag(dst_ref, src_ref):
    dst_ref[...] = jnp.zeros_like(dst_ref)
    for g in range(S5_LB_GROUPS):
        for s in range(S5_CHUNK):
            dst_ref[s * LANES + g * S5_GROUP:s * LANES + (g + 1) * S5_GROUP, g * S5_CW:(g + 1) * S5_CW] = (
                src_ref[g, s * S5_GROUP:(s + 1) * S5_GROUP, :])


def _s5_local_state(u2, we):
    nc = u2.shape[0]
    k = S5_CHUNK * LANES
    n = S5_LB_GROUPS * S5_CW

    def body(u2_ref, we_ref, out_ref, x_ref, w_ref, sem):
        lb = pl.program_id(0)
        _s5_gather_chunks(u2_ref, x_ref, sem, lb)
        _s5_block_diag(w_ref, we_ref)
        states = jnp.dot(x_ref[...], w_ref[...], preferred_element_type=F32)
        for g in range(S5_LB_GROUPS):
            out_ref[:, g, :] = states[:, g * S5_CW:(g + 1) * S5_CW]

    return pl.pallas_call(
        body,
        grid=(D_MODEL // LANES,),
        in_specs=[pl.BlockSpec(memory_space=pl.ANY),
                  pl.BlockSpec((S5_LB_GROUPS, S5_CW, S5_CW), lambda lb: (lb, 0, 0))],
        out_specs=pl.BlockSpec((nc, S5_LB_GROUPS, S5_CW), lambda lb: (0, lb, 0)),
        out_shape=jax.ShapeDtypeStruct((nc, S5_GROUPS, S5_CW), F32),
        scratch_shapes=[pltpu.VMEM((nc, k), BF16), pltpu.VMEM((k, n), BF16), pltpu.SemaphoreType.DMA],
        compiler_params=_params(("arbitrary",)),
        name="s5_local_state",
    )(u2, we)


def _s5_carry(sloc_t, at):
    nc, G, cw = sloc_t.shape
    cb = CTX_LEN // S5_CHUNK
    nb = nc // cb
    half = cw // 2

    def rev_block(s):
        return jnp.where(s == 0, 0, nb - s)

    def body(sf_ref, sb_ref, at_ref, pf_ref, pb_ref, st_ref):
        s = pl.program_id(0)

        @pl.when(s == 0)
        def _():
            st_ref[...] = jnp.zeros_like(st_ref)

        is_fwd = (lax.broadcasted_iota(jnp.int32, (G, half), 1) < S5_STATE)
        a_re, a_im = at_ref[:, :half], at_ref[:, half:]
        s_re, s_im = st_ref[:, :half], st_ref[:, half:]
        for k in range(cb):
            kb = cb - 1 - k
            pf_ref[k, :, :half] = s_re
            pf_ref[k, :, half:] = s_im
            pb_ref[kb, :, :half] = s_re
            pb_ref[kb, :, half:] = s_im
            l_re = jnp.where(is_fwd, sf_ref[k, :, :half], sb_ref[kb, :, :half])
            l_im = jnp.where(is_fwd, sf_ref[k, :, half:], sb_ref[kb, :, half:])
            n_re = a_re * s_re - a_im * s_im + l_re
            n_im = a_re * s_im + a_im * s_re + l_im
            s_re, s_im = n_re, n_im
        st_ref[:, :half] = s_re
        st_ref[:, half:] = s_im

    blk = (cb, G, cw)
    return pl.pallas_call(
        body,
        grid=(nb,),
        in_specs=[pl.BlockSpec(blk, lambda s: (s, 0, 0)),
                  pl.BlockSpec(blk, lambda s: (rev_block(s), 0, 0)),
                  pl.BlockSpec((G, cw), lambda s: (0, 0))],
        out_specs=[pl.BlockSpec(blk, lambda s: (s, 0, 0)),
                   pl.BlockSpec(blk, lambda s: (rev_block(s), 0, 0))],
        out_shape=[jax.ShapeDtypeStruct((nc, G, cw), F32)] * 2,
        scratch_shapes=[pltpu.VMEM((G, cw), F32)],
        compiler_params=_params(("arbitrary",)),
        name="s5_carry",
    )(sloc_t, sloc_t, at)


def _s5_output(u2, kd, wo, pf, pb, d3, j):
    nc = u2.shape[0]
    T = S5_CHUNK
    k = T * LANES
    n = S5_LB_GROUPS * S5_CW
    nt = (((1,), (1,)), ((), ()))

    def body(u2_ref, kd_ref, wo_ref, pf_ref, pb_ref, d_ref, y2_ref,
             x_ref, toep_ref, w_ref, bd_ref, y_ref, sem_in, sem_out):
        lb = pl.program_id(0)
        _s5_gather_chunks(u2_ref, x_ref, sem_in, lb)
        for lag in range(2 * T - 1):
            for g in range(S5_LB_GROUPS):
                bd_ref[lag * LANES + g * S5_GROUP:lag * LANES + (g + 1) * S5_GROUP, :] = (
                    kd_ref[g, lag * S5_GROUP:(lag + 1) * S5_GROUP, :])
        for s in range(T):
            for t in range(T):
                lag = t - s + T - 1
                toep_ref[s * LANES:(s + 1) * LANES, t * LANES:(t + 1) * LANES] = (
                    bd_ref[lag * LANES:(lag + 1) * LANES, :])
        _s5_block_diag(w_ref, wo_ref)
        is_fwd = (lax.broadcasted_iota(jnp.int32, (nc, n), 1) % LANES) < S5_STATE
        p = jnp.where(is_fwd, pf_ref[...], pb_ref[...]).astype(BF16)
        x = x_ref[...]
        y = jnp.dot(x, toep_ref[...], preferred_element_type=F32)
        y = y + lax.dot_general(p, w_ref[...], nt, preferred_element_type=F32)
        y = y + jnp.concatenate([d_ref[...]] * T, axis=1) * x.astype(F32)
        y_ref[...] = y.astype(BF16)
        copies = [pltpu.make_async_copy(y_ref.at[:, pl.ds(t * LANES, LANES)],
                                        y2_ref.at[:, pl.ds(t * D_MODEL + lb * LANES, LANES)], sem_out)
                  for t in range(T)]
        for cp in copies:
            cp.start()
        for cp in copies:
            cp.wait()

    once = pl.Buffered(1)
    state = pl.BlockSpec((nc, n), lambda lb: (0, lb), pipeline_mode=once)
    return pl.pallas_call(
        body,
        grid=(D_MODEL // LANES,),
        in_specs=[pl.BlockSpec(memory_space=pl.ANY),
                  pl.BlockSpec((S5_LB_GROUPS, 2 * S5_CW, LANES), lambda lb: (lb, 0, 0)),
                  pl.BlockSpec((S5_LB_GROUPS, S5_CW, S5_CW), lambda lb: (lb, 0, 0)),
                  state, state,
                  pl.BlockSpec((None, 1, LANES), lambda lb: (j, 0, lb))],
        out_specs=pl.BlockSpec(memory_space=pl.ANY),
        out_shape=jax.ShapeDtypeStruct((nc, T * D_MODEL), BF16),
        scratch_shapes=[pltpu.VMEM((nc, k), BF16), pltpu.VMEM((k, k), BF16), pltpu.VMEM((k, n), BF16),
                        pltpu.VMEM((2 * T * LANES, LANES), BF16), pltpu.VMEM((nc, k), BF16),
                        pltpu.SemaphoreType.DMA, pltpu.SemaphoreType.DMA],
        compiler_params=_params(("arbitrary",)),
        name="s5_output",
    )(u2, kd, wo, pf, pb, d3)


def _s5_layer(h, layer, j, mod, norm_g, p):
    n_rows = h.shape[0]
    G, T, H = S5_GROUPS, S5_CHUNK, S5_GROUP
    nc = n_rows // T
    tm = _pick(n_rows, MIX_ROW_TILES)
    tn = 512
    u = _fused_matmul(
        name="s5_in", n_rows=n_rows, k_dim=D_MODEL, n_out=D_MODEL, tm=tm, tn=tn,
        pro_args=(h, norm_g, mod), pro_specs=_norm_specs(layer, tm),
        pro_fn=lambda i, a_ref, h_ref, g_ref, m_ref: _norm_mod_into(a_ref, i, tm, h_ref, g_ref, m_ref, 0),
        w=p['s5_w_in'], w_layer=j, w_col_offsets=(0,),
        epi_args=(), epi_specs=(), epi_fn=lambda i, jj, ys: ys[0], out_dtype=BF16)
    u2 = u.reshape(nc, T * D_MODEL)
    kd, we, wo, at = _s5_operators(
        p['s5_lam_re'][j], p['s5_lam_im'][j], p['s5_log_dt'][j],
        jnp.swapaxes(p['s5_b_re'][j], -1, -2), jnp.swapaxes(p['s5_b_im'][j], -1, -2),
        p['s5_c_re'][j], p['s5_c_im'][j])
    sloc = _s5_local_state(u2, we)
    pf, pb = _s5_carry(sloc, at.reshape(G, T * H))
    d3 = p['s5_d'].reshape(-1, 1, D_MODEL)
    y2 = _s5_output(u2, kd, wo, pf.reshape(nc, G * T * H), pb.reshape(nc, G * T * H), d3, j)
    y = y2.reshape(n_rows, D_MODEL)

    def gelu_into(i, a_ref, y_ref):
        a_ref[...] = _gelu_tanh(y_ref[...].astype(F32)).astype(BF16)

    return _fused_matmul(
        name="s5_out", n_rows=n_rows, k_dim=D_MODEL, n_out=D_MODEL, tm=tm, tn=tn,
        pro_args=(y,), pro_specs=[pl.BlockSpec((tm, D_MODEL), lambda i, jj: (i, 0), pipeline_mode=pl.Buffered(1))],
        pro_fn=gelu_into,
        w=p['s5_w_out'], w_layer=j, w_col_offsets=(0, D_MODEL),
        epi_args=(h, mod), epi_specs=_res_specs(layer, tm, tn),
        epi_fn=lambda i, jj, ys, h_ref, m_ref: _gated_residual(
            i, tm, 2, ys[0] * _sigmoid(ys[1]), h_ref, m_ref),
        out_dtype=F32)


def _sg_layer(h, layer, j, mod, norm_g, p):
    n_rows = h.shape[0]
    tm = _pick(n_rows, MIX_ROW_TILES)
    tn = 512
    z = _fused_matmul(
        name="sg_in", n_rows=n_rows, k_dim=D_MODEL, n_out=2 * D_MODEL, tm=tm, tn=tn,
        pro_args=(h, norm_g, mod), pro_specs=_norm_specs(layer, tm),
        pro_fn=lambda i, a_ref, h_ref, g_ref, m_ref: _norm_mod_into(a_ref, i, tm, h_ref, g_ref, m_ref, 0),
        w=p['sg_w_in'], w_layer=j, w_col_offsets=(0,),
        epi_args=(), epi_specs=(), epi_fn=lambda i, jj, ys: _gelu_tanh(ys[0]), out_dtype=BF16)

    def gate(i, a_ref, u_ref, v_ref, g_ref, b_ref, ws_ref, bs_ref):
        ln_g, ln_b = g_ref[...], b_ref[...]

        def chunk(c, carry):
            rows = pl.ds(pl.multiple_of(c * SG_CHUNK, SG_CHUNK), SG_CHUNK)
            v = _layer_norm(v_ref[rows, :].astype(F32), ln_g, ln_b).astype(BF16)
            for hd in range(SG_HEADS):
                cols = slice(hd * SG_HEAD_DIM, (hd + 1) * SG_HEAD_DIM)
                sv = jnp.dot(ws_ref[hd].astype(BF16), v[:, cols], preferred_element_type=F32)
                gated = u_ref[rows, cols].astype(F32) * (sv + bs_ref[:, hd:hd + 1])
                a_ref[rows, cols] = gated.astype(BF16)
            return carry

        lax.fori_loop(0, tm // SG_CHUNK, chunk, 0)

    return _fused_matmul(
        name="sg_out", n_rows=n_rows, k_dim=D_MODEL, n_out=D_MODEL, tm=tm, tn=tn,
        pro_args=(z, z, p['sg_ln_g'], p['sg_ln_b'], p['sg_w_s'], jnp.swapaxes(p['sg_b_s'], -1, -2)),
        pro_specs=[
            pl.BlockSpec((tm, D_MODEL), lambda i, jj: (i, 0), pipeline_mode=pl.Buffered(1)),
            pl.BlockSpec((tm, D_MODEL), lambda i, jj: (i, 1), pipeline_mode=pl.Buffered(1)),
            pl.BlockSpec((1, D_MODEL), lambda i, jj: (j, 0)),
            pl.BlockSpec((1, D_MODEL), lambda i, jj: (j, 0)),
            pl.BlockSpec((None, SG_HEADS, SG_CHUNK, SG_CHUNK), lambda i, jj: (j, 0, 0, 0)),
            pl.BlockSpec((None, SG_CHUNK, SG_HEADS), lambda i, jj: (j, 0, 0)),
        ],
        pro_fn=gate,
        w=p['sg_w_out'], w_layer=j, w_col_offsets=(0,),
        epi_args=(h, mod), epi_specs=_res_specs(layer, tm, tn),
        epi_fn=lambda i, jj, ys, h_ref, m_ref: _gated_residual(i, tm, 2, ys[0], h_ref, m_ref),
        out_dtype=F32)


def _conv_norm(z, dw_w, dw_b, ln_g, ln_b, j):
    n_rows = z.shape[0]
    tm = CTX_LEN
    n_tiles = n_rows // tm
    hb = tm // CONV_HALO
    half = CONV_WIDTH // 2
    rb, cbw = 64, LANES

    def body(prev_ref, cur_ref, next_ref, w_ref, b_ref, g_ref, beta_ref, out_ref, zp_ref, acc_ref):
        i = pl.program_id(0)
        prev_ok = i >= 2
        next_ok = jnp.logical_and(i >= 1, i < n_tiles - 1)
        zp_ref[0:CONV_HALO, :] = jnp.where(prev_ok, prev_ref[...], 0.0)
        zp_ref[CONV_HALO:CONV_HALO + tm, :] = cur_ref[...]
        zp_ref[CONV_HALO + tm:, :] = jnp.where(next_ok, next_ref[...], 0.0)

        def col_block(cb, carry):
            c0 = pl.multiple_of(cb * cbw, cbw)
            w = w_ref[:, pl.ds(c0, cbw)]
            for r in range(tm // rb):
                win = zp_ref[r * rb:r * rb + rb + 2 * CONV_HALO, pl.ds(c0, cbw)]
                span = rb + 2 * CONV_HALO - SUBLANES
                phases = [win[ph:ph + span, :] for ph in range(SUBLANES)]
                acc = jnp.zeros((rb, cbw), F32)
                for k in range(CONV_WIDTH):
                    off = CONV_HALO - half + k
                    base = off - off % SUBLANES
                    acc = acc + phases[off % SUBLANES][base:base + rb, :] * w[k:k + 1, :]
                acc_ref[r * rb:(r + 1) * rb, pl.ds(c0, cbw)] = acc
            return carry

        lax.fori_loop(0, D_MODEL // cbw, col_block, 0)
        y = _layer_norm(acc_ref[...] + b_ref[...], g_ref[...], beta_ref[...])
        out_ref[...] = _silu(y).astype(BF16)

    vec = pl.BlockSpec((1, D_MODEL), lambda i: (j, 0))
    return pl.pallas_call(
        body,
        grid=(n_tiles,),
        in_specs=[
            pl.BlockSpec((CONV_HALO, D_MODEL), lambda i: (jnp.maximum(i * hb - 1, 0), 0)),
            pl.BlockSpec((tm, D_MODEL), lambda i: (i, 0)),
            pl.BlockSpec((CONV_HALO, D_MODEL), lambda i: (jnp.minimum((i + 1) * hb, n_tiles * hb - 1), 0)),
            pl.BlockSpec((None, CONV_WIDTH, D_MODEL), lambda i: (j, 0, 0)),
            vec, vec, vec,
        ],
        out_specs=pl.BlockSpec((tm, D_MODEL), lambda i: (i, 0)),
        out_shape=jax.ShapeDtypeStruct((n_rows, D_MODEL), BF16),
        scratch_shapes=[pltpu.VMEM((tm + 2 * CONV_HALO, D_MODEL), F32), pltpu.VMEM((tm, D_MODEL), F32)],
        compiler_params=_params(("arbitrary",)),
        name="conv_norm",
    )(z, z, z, dw_w, dw_b, ln_g, ln_b)


def _conv_layer(h, layer, j, mod, norm_g, p):
    n_rows = h.shape[0]
    tm = _pick(n_rows, MIX_ROW_TILES)
    tn = 512
    z = _fused_matmul(
        name="cv_pw1", n_rows=n_rows, k_dim=D_MODEL, n_out=D_MODEL, tm=tm, tn=tn,
        pro_args=(h, norm_g, mod), pro_specs=_norm_specs(layer, tm),
        pro_fn=lambda i, a_ref, h_ref, g_ref, m_ref: _norm_mod_into(a_ref, i, tm, h_ref, g_ref, m_ref, 0),
        w=p['cv_w_pw1'], w_layer=j, w_col_offsets=(0, D_MODEL),
        epi_args=(), epi_specs=(), epi_fn=lambda i, jj, ys: ys[0] * _sigmoid(ys[1]), out_dtype=F32)
    zc = _conv_norm(z, p['cv_dw_w'], p['cv_dw_b'], p['cv_ln_g'], p['cv_ln_b'], j)

    def copy_into(i, a_ref, z_ref):
        a_ref[...] = z_ref[...]

    return _fused_matmul(
        name="cv_pw2", n_rows=n_rows, k_dim=D_MODEL, n_out=D_MODEL, tm=tm, tn=tn,
        pro_args=(zc,), pro_specs=[pl.BlockSpec((tm, D_MODEL), lambda i, jj: (i, 0), pipeline_mode=pl.Buffered(1))],
        pro_fn=copy_into,
        w=p['cv_w_pw2'], w_layer=j, w_col_offsets=(0,),
        epi_args=(h, mod), epi_specs=_res_specs(layer, tm, tn),
        epi_fn=lambda i, jj, ys, h_ref, m_ref: _gated_residual(i, tm, 2, ys[0], h_ref, m_ref),
        out_dtype=F32)


SLAB_RANK1, SLAB_RANK2, SLAB_EXP1, SLAB_EXP2, SLAB_W1, SLAB_W2 = range(6)


def _router(h, layer, k, mod, norm_g, w_router):
    n_rows = h.shape[0]
    tm = _pick(n_rows, (768, 256))
    wr = jnp.pad(w_router[k], ((0, 0), (0, LANES - N_EXPERTS)))

    def body(h_ref, g_ref, m_ref, wr_ref, hn_ref, slab_ref, cnt_ref, base_ref):
        i = pl.program_id(0)

        @pl.when(i == 0)
        def _():
            base_ref[...] = jnp.zeros_like(base_ref)

        hn = _norm_mod(i, tm, h_ref, g_ref, m_ref, 3)
        hn_ref[...] = hn
        logits = jnp.dot(hn, wr_ref[...], precision=HIGHEST, preferred_element_type=F32)
        lane = lax.broadcasted_iota(jnp.int32, (tm, LANES), 1).astype(F32)
        neg = jnp.float32(-jnp.inf)
        lg = jnp.where(lane < N_EXPERTS, logits, neg)
        m1 = jnp.max(lg, axis=1, keepdims=True)
        i1 = jnp.min(jnp.where(lg == m1, lane, float(LANES)), axis=1, keepdims=True)
        lg2 = jnp.where(lane == i1, neg, lg)
        m2 = jnp.max(lg2, axis=1, keepdims=True)
        i2 = jnp.min(jnp.where(lg2 == m2, lane, float(LANES)), axis=1, keepdims=True)
        e = jnp.exp(m2 - m1)
        w1 = 1.0 / (1.0 + e)
        w2 = e / (1.0 + e)
        onehot = jnp.where(lane == i1, 1.0, 0.0) + jnp.where(lane == i2, 1.0, 0.0)
        tri = jnp.where(lax.broadcasted_iota(jnp.int32, (tm, tm), 1)
                        < lax.broadcasted_iota(jnp.int32, (tm, tm), 0), 1.0, 0.0).astype(BF16)
        base = base_ref[0:1, :]
        excl = jnp.dot(tri, onehot.astype(BF16), preferred_element_type=F32) + base
        r1 = jnp.sum(jnp.where(lane == i1, excl, 0.0), axis=1, keepdims=True)
        r2 = jnp.sum(jnp.where(lane == i2, excl, 0.0), axis=1, keepdims=True)
        total = base + jnp.sum(onehot, axis=0, keepdims=True)
        base_ref[...] = jnp.broadcast_to(total, base_ref.shape)
        cnt_ref[...] = jnp.broadcast_to(total, cnt_ref.shape)
        slab = jnp.zeros((tm, LANES), F32)
        for ln, val in ((SLAB_RANK1, r1), (SLAB_RANK2, r2), (SLAB_EXP1, i1), (SLAB_EXP2, i2),
                        (SLAB_W1, w1), (SLAB_W2, w2)):
            slab = jnp.where(lane == float(ln), val, slab)
        slab_ref[...] = slab

    return pl.pallas_call(
        body,
        grid=(n_rows // tm,),
        in_specs=[
            pl.BlockSpec((tm, D_MODEL), lambda i: (i, 0)),
            pl.BlockSpec((None, 1, D_MODEL), lambda i: (layer, 0, 0)),
            pl.BlockSpec((None, 2, N_MOD, D_MODEL), lambda i: (layer, 0, 0, 0)),
            pl.BlockSpec((D_MODEL, LANES), lambda i: (0, 0)),
        ],
        out_specs=[pl.BlockSpec((tm, D_MODEL), lambda i: (i, 0)),
                   pl.BlockSpec((tm, LANES), lambda i: (i, 0)),
                   pl.BlockSpec((8, LANES), lambda i: (0, 0))],
        out_shape=[jax.ShapeDtypeStruct((n_rows, D_MODEL), F32),
                   jax.ShapeDtypeStruct((n_rows, LANES), F32),
                   jax.ShapeDtypeStruct((8, LANES), F32)],
        scratch_shapes=[pltpu.VMEM((8, LANES), F32)],
        compiler_params=_params(("arbitrary",)),
        name="router",
    )(h, norm_g, mod, wr)


def _routing_tables(slab, counts, tb, nblk):
    cnt = counts[0, :N_EXPERTS].astype(jnp.int32)
    padded = ((cnt + tb - 1) // tb) * tb
    ends = jnp.cumsum(padded)
    starts = ends - padded
    pos1 = starts[slab[:, SLAB_EXP1].astype(jnp.int32)] + slab[:, SLAB_RANK1].astype(jnp.int32)
    pos2 = starts[slab[:, SLAB_EXP2].astype(jnp.int32)] + slab[:, SLAB_RANK2].astype(jnp.int32)
    nvalid = ends[-1] // tb
    blk = jnp.minimum(jnp.arange(nblk, dtype=jnp.int32), nvalid - 1)
    blk_expert = jnp.minimum(jnp.searchsorted(ends, blk * tb, side='right'), N_EXPERTS - 1)
    return pos1, pos2, blk_expert.astype(jnp.int32), nvalid.reshape(1).astype(jnp.int32)


def _dispatch(hn, pos1, pos2, n_slots):
    n_rows = hn.shape[0]
    tm = CTX_LEN

    def body(p1_ref, p2_ref, hn_ref, init_ref, xs_ref, sem):
        i = pl.program_id(0)

        def row_copy(t, p):
            return pltpu.make_async_copy(hn_ref.at[pl.ds(t, 1)], xs_ref.at[pl.ds(p, 1)], sem)

        def issue(t, carry):
            row_copy(t, p1_ref[i * tm + t]).start()
            row_copy(t, p2_ref[i * tm + t]).start()
            return carry

        def drain(t, carry):
            row_copy(0, 0).wait()
            row_copy(0, 0).wait()
            return carry

        lax.fori_loop(0, tm, issue, 0)
        lax.fori_loop(0, tm, drain, 0)

    return pl.pallas_call(
        body,
        grid_spec=pltpu.PrefetchScalarGridSpec(
            num_scalar_prefetch=2,
            grid=(n_rows // tm,),
            in_specs=[pl.BlockSpec((tm, D_MODEL), lambda i, p1, p2: (i, 0)),
                      pl.BlockSpec(memory_space=pl.ANY)],
            out_specs=pl.BlockSpec(memory_space=pl.ANY),
            scratch_shapes=[pltpu.SemaphoreType.DMA],
        ),
        out_shape=jax.ShapeDtypeStruct((n_slots, D_MODEL), F32),
        input_output_aliases={3: 0},
        compiler_params=_params(("arbitrary",)),
        name="moe_dispatch",
    )(pos1, pos2, hn, jnp.zeros((n_slots, D_MODEL), F32))


def _swiglu_chunk(a, wg_ref, wu_ref, wd_ref):
    g = jnp.dot(a, wg_ref[...].astype(BF16), preferred_element_type=F32)
    u = jnp.dot(a, wu_ref[...].astype(BF16), preferred_element_type=F32)
    return jnp.dot((_silu(g) * u).astype(BF16), wd_ref[...].astype(BF16), preferred_element_type=F32)


def _moe_ffn(xs, blk_expert, nvalid, w_gate, w_up, w_down, e_base, tb):
    n_slots = xs.shape[0]
    d_ff = w_gate.shape[-1]
    tf = FFN_CHUNK
    nf = d_ff // tf

    def body(be_ref, nv_ref, x_ref, wg_ref, wu_ref, wd_ref, out_ref, a_ref):
        b = pl.program_id(0)
        f = pl.program_id(1)
        used = b < nv_ref[0]

        @pl.when(jnp.logical_and(used, f == 0))
        def _():
            a_ref[...] = x_ref[...].astype(BF16)
            out_ref[...] = jnp.zeros_like(out_ref)

        @pl.when(used)
        def _():
            out_ref[...] += _swiglu_chunk(a_ref[...], wg_ref, wu_ref, wd_ref)

        @pl.when(jnp.logical_and(jnp.logical_not(used), f == 0))
        def _():
            out_ref[...] = jnp.zeros_like(out_ref)

    def f_idx(b, f, nv):
        return jnp.where(b < nv[0], f, nf - 1)

    once = pl.Buffered(1)
    rows = lambda b, f, be, nv: (jnp.minimum(b, nv[0] - 1), 0)
    return pl.pallas_call(
        body,
        grid_spec=pltpu.PrefetchScalarGridSpec(
            num_scalar_prefetch=2,
            grid=(n_slots // tb, nf),
            in_specs=[
                pl.BlockSpec((tb, D_MODEL), rows, pipeline_mode=once),
                pl.BlockSpec((None, D_MODEL, tf), lambda b, f, be, nv: (e_base + be[b], 0, f_idx(b, f, nv))),
                pl.BlockSpec((None, D_MODEL, tf), lambda b, f, be, nv: (e_base + be[b], 0, f_idx(b, f, nv))),
                pl.BlockSpec((None, tf, D_MODEL), lambda b, f, be, nv: (e_base + be[b], f_idx(b, f, nv), 0)),
            ],
            out_specs=pl.BlockSpec((tb, D_MODEL), lambda b, f, be, nv: (b, 0), pipeline_mode=once),
            scratch_shapes=[pltpu.VMEM((tb, D_MODEL), BF16)],
        ),
        out_shape=jax.ShapeDtypeStruct((n_slots, D_MODEL), F32),
        compiler_params=_params(("arbitrary", "arbitrary")),
        name="moe_ffn",
    )(blk_expert, nvalid, xs, w_gate, w_up, w_down)


def _combine(h, ys, slab, pos1, pos2, layer, mod):
    n_rows = h.shape[0]
    tm = CTX_LEN

    def body(p1_ref, p2_ref, h_ref, slab_ref, m_ref, ys_ref, out_ref, y1_ref, y2_ref, sem):
        i = pl.program_id(0)

        def row_copy(p, dst_ref, t):
            return pltpu.make_async_copy(ys_ref.at[pl.ds(p, 1)], dst_ref.at[pl.ds(t, 1)], sem)

        def issue(t, carry):
            row_copy(p1_ref[i * tm + t], y1_ref, t).start()
            row_copy(p2_ref[i * tm + t], y2_ref, t).start()
            return carry

        def drain(t, carry):
            row_copy(0, y1_ref, 0).wait()
            row_copy(0, y2_ref, 0).wait()
            return carry

        lax.fori_loop(0, tm, issue, 0)
        lax.fori_loop(0, tm, drain, 0)
        w1 = slab_ref[:, SLAB_W1:SLAB_W1 + 1]
        w2 = slab_ref[:, SLAB_W2:SLAB_W2 + 1]
        moe = w1 * y1_ref[...] + w2 * y2_ref[...]
        out_ref[...] = h_ref[...] + _mod_row(m_ref, 5, _is_ctx(i, tm)) * moe

    return pl.pallas_call(
        body,
        grid_spec=pltpu.PrefetchScalarGridSpec(
            num_scalar_prefetch=2,
            grid=(n_rows // tm,),
            in_specs=[
                pl.BlockSpec((tm, D_MODEL), lambda i, p1, p2: (i, 0)),
                pl.BlockSpec((tm, LANES), lambda i, p1, p2: (i, 0)),
                pl.BlockSpec((None, 2, N_MOD, D_MODEL), lambda i, p1, p2: (layer, 0, 0, 0)),
                pl.BlockSpec(memory_space=pl.ANY),
            ],
            out_specs=pl.BlockSpec((tm, D_MODEL), lambda i, p1, p2: (i, 0)),
            scratch_shapes=[pltpu.VMEM((tm, D_MODEL), F32), pltpu.VMEM((tm, D_MODEL), F32),
                            pltpu.SemaphoreType.DMA],
        ),
        out_shape=jax.ShapeDtypeStruct((n_rows, D_MODEL), F32),
        compiler_params=_params(("arbitrary",)),
        name="moe_combine",
    )(pos1, pos2, h, slab, mod, ys)


def _moe_layer(h, layer, k, mod, norm_g, w_router, w_gate, w_up, w_down):
    n_rows = h.shape[0]
    d_ff = w_gate.shape[-1]
    tb = 1024 if n_rows >= 4096 else 256
    nblk = -(-(TOP_K * n_rows + N_EXPERTS * (tb - 1)) // tb)
    hn, slab, counts = _router(h, layer, k, mod, norm_g, w_router)
    pos1, pos2, blk_expert, nvalid = _routing_tables(slab, counts, tb, nblk)
    xs = _dispatch(hn, pos1, pos2, nblk * tb)
    ys = _moe_ffn(xs, blk_expert, nvalid, w_gate.reshape(-1, D_MODEL, d_ff), w_up.reshape(-1, D_MODEL, d_ff),
                  w_down.reshape(-1, d_ff, D_MODEL), k * N_EXPERTS, tb)
    return _combine(h, ys, slab, pos1, pos2, layer, mod)


def _ffn(h, layer, mod, norm_g, w_gate, w_up, w_down, k):
    n_rows = h.shape[0]
    d_ff = w_gate.shape[-1]
    tm = _pick(n_rows, (768, 256))
    tf = FFN_CHUNK
    nf = d_ff // tf

    def body(h_ref, g_ref, m_ref, wg_ref, wu_ref, wd_ref, out_ref, a_ref):
        i = pl.program_id(0)
        f = pl.program_id(1)

        @pl.when(f == 0)
        def _():
            _norm_mod_into(a_ref, i, tm, h_ref, g_ref, m_ref, 3)
            out_ref[...] = jnp.zeros_like(out_ref)

        out_ref[...] += _swiglu_chunk(a_ref[...], wg_ref, wu_ref, wd_ref)

        @pl.when(f == nf - 1)
        def _():
            def chunk(r, carry):
                rows = pl.ds(pl.multiple_of(r * MOD_CHUNK, MOD_CHUNK), MOD_CHUNK)
                m = jnp.where(i * tm + r * MOD_CHUNK < CTX_LEN, 1, 0)
                out_ref[rows, :] = h_ref[rows, :] + m_ref[m, pl.ds(5, 1), :] * out_ref[rows, :]
                return carry

            lax.fori_loop(0, tm // MOD_CHUNK, chunk, 0)

    once = pl.Buffered(1)
    return pl.pallas_call(
        body,
        grid=(n_rows // tm, nf),
        in_specs=[
            pl.BlockSpec((tm, D_MODEL), lambda i, f: (i, 0), pipeline_mode=once),
            pl.BlockSpec((None, 1, D_MODEL), lambda i, f: (layer, 0, 0)),
            pl.BlockSpec((None, 2, N_MOD, D_MODEL), lambda i, f: (layer, 0, 0, 0)),
            pl.BlockSpec((None, D_MODEL, tf), lambda i, f: (k, 0, f)),
            pl.BlockSpec((None, D_MODEL, tf), lambda i, f: (k, 0, f)),
            pl.BlockSpec((None, tf, D_MODEL), lambda i, f: (k, f, 0)),
        ],
        out_specs=pl.BlockSpec((tm, D_MODEL), lambda i, f: (i, 0), pipeline_mode=once),
        out_shape=jax.ShapeDtypeStruct((n_rows, D_MODEL), F32),
        scratch_shapes=[pltpu.VMEM((tm, D_MODEL), BF16)],
        compiler_params=_params(("arbitrary", "arbitrary")),
        name="ffn",
    )(h, norm_g, mod, w_gate, w_up, w_down)


def _final_norm(h, final_g):
    n_rows = h.shape[0]
    tm = CTX_LEN

    def body(h_ref, g_ref, out_ref):
        x = h_ref[...]
        out_ref[...] = x * lax.rsqrt(jnp.mean(x * x, axis=-1, keepdims=True) + EPS) * g_ref[...]

    return pl.pallas_call(
        body,
        grid=((n_rows - CTX_LEN) // tm,),
        in_specs=[pl.BlockSpec((tm, D_MODEL), lambda i: (i + 1, 0)),
                  pl.BlockSpec((1, D_MODEL), lambda i: (0, 0))],
        out_specs=pl.BlockSpec((tm, D_MODEL), lambda i: (i, 0)),
        out_shape=jax.ShapeDtypeStruct((n_rows - CTX_LEN, D_MODEL), F32),
        compiler_params=_params(("arbitrary",)),
        name="final_norm",
    )(h, final_g.reshape(1, D_MODEL))


def _sincos_tables(rows):
    quarter = D_MODEL // 4
    omega = 1.0 / (POS_BASE ** (jnp.arange(quarter, dtype=F32) / quarter))

    def emb(pos):
        ang = pos[:, None] * omega[None, :]
        return jnp.concatenate([jnp.sin(ang), jnp.cos(ang)], axis=-1)

    return emb(jnp.arange(rows, dtype=F32)), emb(jnp.arange(GRID_W, dtype=F32))


def kernel(x, c, ctx, c_ctx, ada_w, ada_b, norm_mix_g, norm_ffn_g, final_g, s5_w_in, s5_lam_re, s5_lam_im, s5_log_dt, s5_b_re, s5_b_im, s5_c_re, s5_c_im, s5_d, s5_w_out, sg_w_in, sg_ln_g, sg_ln_b, sg_w_s, sg_b_s, sg_w_out, cv_w_pw1, cv_dw_w, cv_dw_b, cv_ln_g, cv_ln_b, cv_w_pw2, ff_w_gate, ff_w_up, ff_w_down, moe_w_router, moe_w_gate, moe_w_up, moe_w_down):
    bsz, seq, dim = x.shape
    assert bsz == 1 and dim == D_MODEL and ctx.shape == (1, CTX_LEN, D_MODEL)
    p = dict(s5_w_in=s5_w_in, s5_lam_re=s5_lam_re, s5_lam_im=s5_lam_im, s5_log_dt=s5_log_dt,
             s5_b_re=s5_b_re, s5_b_im=s5_b_im, s5_c_re=s5_c_re, s5_c_im=s5_c_im, s5_d=s5_d,
             s5_w_out=s5_w_out, sg_w_in=sg_w_in, sg_ln_g=sg_ln_g, sg_ln_b=sg_ln_b, sg_w_s=sg_w_s,
             sg_b_s=sg_b_s, sg_w_out=sg_w_out, cv_w_pw1=cv_w_pw1, cv_dw_w=cv_dw_w, cv_dw_b=cv_dw_b,
             cv_ln_g=cv_ln_g, cv_ln_b=cv_ln_b, cv_w_pw2=cv_w_pw2)
    row_tab, col_tab = _sincos_tables(seq // GRID_W)
    h = _embed(x[0], ctx[0], row_tab, col_tab)
    cond8 = jnp.concatenate([c, c_ctx[None, :], jnp.zeros((6, D_MODEL), F32)], axis=0)
    mod = _modulation(cond8, ada_w, ada_b)[:, :2, :].reshape(DEPTH, 2, N_MOD, D_MODEL)
    g_mix = norm_mix_g.reshape(DEPTH, 1, D_MODEL)
    g_ffn = norm_ffn_g.reshape(DEPTH, 1, D_MODEL)
    for layer in range(DEPTH):
        kind, j = layer % N_MIXERS, layer // N_MIXERS
        if kind == 0:
            h = _s5_layer(h, layer, j, mod, g_mix, p)
        elif kind == 1:
            h = _sg_layer(h, layer, j, mod, g_mix, p)
        else:
            h = _conv_layer(h, layer, j, mod, g_mix, p)
        k = layer // 2
        if layer % 2 == 0:
            h = _ffn(h, layer, mod, g_ffn, ff_w_gate, ff_w_up, ff_w_down, k)
        else:
            h = _moe_layer(h, layer, k, mod, g_ffn, moe_w_router, moe_w_gate, moe_w_up, moe_w_down)
    return _final_norm(h, final_g)[None]
```

```python
import functools
import math

import jax
import jax.numpy as jnp
from jax import lax
from jax.experimental import pallas as pl
from jax.experimental.pallas import tpu as pltpu

F32 = jnp.float32
BF16 = jnp.bfloat16
HIGHEST = lax.Precision.HIGHEST

D_MODEL = 2048
DEPTH = 4
GRID_W = 64
CTX_LEN = 256
POS_BASE = 10000.0
N_MIXERS = 3
N_MOD = 6
EPS = 1e-6
S5_GROUP = 16
S5_GROUPS = D_MODEL // S5_GROUP
S5_STATE = 64
S5_CHUNK = 16
S5_CW = S5_CHUNK * S5_GROUP
SG_CHUNK = 128
SG_HEADS = 8
SG_HEAD_DIM = D_MODEL // SG_HEADS
CONV_WIDTH = 31
CONV_HALO = 16
N_EXPERTS = 8
TOP_K = 2
LANES = 128
SUBLANES = 8
S5_LB_GROUPS = LANES // S5_GROUP
MOD_CHUNK = 128
FFN_CHUNK = 512
MOE_FFN_CHUNK = 256
MOE_SUB = 256
MIX_ROW_TILES = (1408, 768, 256)
VMEM_LIMIT = 56 * 1024 * 1024


def _pick(n, candidates):
    for c in candidates:
        if n % c == 0:
            return c
    raise ValueError(f"no tile in {candidates} divides {n}")


def _params(sem):
    return pltpu.CompilerParams(dimension_semantics=sem, vmem_limit_bytes=VMEM_LIMIT)


def _sigmoid(x):
    return 1.0 / (1.0 + jnp.exp(-x))


def _silu(x):
    return x * _sigmoid(x)


def _gelu_tanh(x):
    return 0.5 * x * (1.0 + jnp.tanh(math.sqrt(2.0 / math.pi) * (x + 0.044715 * (x * x * x))))


def _is_ctx(i, tm):
    rows = i * tm + lax.broadcasted_iota(jnp.int32, (tm, 1), 0)
    return rows < CTX_LEN


def _mod_row(mod_ref, k, is_ctx):
    return jnp.where(is_ctx, mod_ref[1, k:k + 1, :], mod_ref[0, k:k + 1, :])


def _norm_mod_rows(row0, h, g, mod_ref, k_shift):
    inv = lax.rsqrt(jnp.mean(h * h, axis=-1, keepdims=True) + EPS)
    hn = (h * inv) * g
    is_ctx = (row0 + lax.broadcasted_iota(jnp.int32, (h.shape[0], 1), 0)) < CTX_LEN
    return hn * (1.0 + _mod_row(mod_ref, k_shift + 1, is_ctx)) + _mod_row(mod_ref, k_shift, is_ctx)


def _norm_mod(i, tm, h_ref, g_ref, mod_ref, k_shift):
    return _norm_mod_rows(i * tm, h_ref[...], g_ref[...], mod_ref, k_shift)


def _norm_mod_into(a_ref, i, tm, h_ref, g_ref, mod_ref, k_shift):
    assert tm % MOD_CHUNK == 0 and CTX_LEN % MOD_CHUNK == 0
    g = g_ref[...]

    def chunk(r, carry):
        rows = pl.ds(pl.multiple_of(r * MOD_CHUNK, MOD_CHUNK), MOD_CHUNK)
        m = jnp.where(i * tm + r * MOD_CHUNK < CTX_LEN, 1, 0)
        h = h_ref[rows, :]
        inv = lax.rsqrt(jnp.mean(h * h, axis=-1, keepdims=True) + EPS)
        scale = mod_ref[m, pl.ds(k_shift + 1, 1), :]
        shift = mod_ref[m, pl.ds(k_shift, 1), :]
        a_ref[rows, :] = (((h * inv) * g) * (1.0 + scale) + shift).astype(a_ref.dtype)
        return carry

    lax.fori_loop(0, tm // MOD_CHUNK, chunk, 0)


def _layer_norm(v, g, b):
    mu = jnp.mean(v, axis=-1, keepdims=True)
    vc = v - mu
    var = jnp.mean(vc * vc, axis=-1, keepdims=True)
    return vc * lax.rsqrt(var + EPS) * g + b


def _fused_matmul(*, name, n_rows, k_dim, n_out, tm, tn, pro_args, pro_specs, pro_fn,
                  w, w_layer, w_col_offsets, epi_args, epi_specs, epi_fn, out_dtype):
    n_pro, n_w, n_epi = len(pro_args), len(w_col_offsets), len(epi_args)
    nj = n_out // tn

    def body(*refs):
        pro_refs = refs[:n_pro]
        w_refs = refs[n_pro:n_pro + n_w]
        epi_refs = refs[n_pro + n_w:n_pro + n_w + n_epi]
        out_ref, a_ref = refs[-2], refs[-1]
        i = pl.program_id(0)
        j = pl.program_id(1)

        @pl.when(j == 0)
        def _():
            pro_fn(i, a_ref, *pro_refs)

        a = a_ref[...]
        ys = [jnp.dot(a, wr[...].astype(BF16), preferred_element_type=F32) for wr in w_refs]
        out_ref[...] = epi_fn(i, j, ys, *epi_refs).astype(out_dtype)

    w_specs = [
        pl.BlockSpec((None, k_dim, tn), functools.partial(lambda i, j, o: (w_layer, 0, j + o), o=off // tn))
        for off in w_col_offsets
    ]
    return pl.pallas_call(
        body,
        grid=(n_rows // tm, nj),
        in_specs=list(pro_specs) + w_specs + list(epi_specs),
        out_specs=pl.BlockSpec((tm, tn), lambda i, j: (i, j)),
        out_shape=jax.ShapeDtypeStruct((n_rows, n_out), out_dtype),
        scratch_shapes=[pltpu.VMEM((tm, k_dim), BF16)],
        compiler_params=_params(("arbitrary", "arbitrary")),
        name=name,
    )(*pro_args, *([w] * n_w), *epi_args)


def _norm_specs(layer, tm):
    return [
        pl.BlockSpec((tm, D_MODEL), lambda i, j: (i, 0), pipeline_mode=pl.Buffered(1)),
        pl.BlockSpec((None, 1, D_MODEL), lambda i, j: (layer, 0, 0)),
        pl.BlockSpec((None, 2, N_MOD, D_MODEL), lambda i, j: (layer, 0, 0, 0)),
    ]


def _res_specs(layer, tm, tn):
    return [
        pl.BlockSpec((tm, tn), lambda i, j: (i, j)),
        pl.BlockSpec((None, 2, N_MOD, tn), lambda i, j: (layer, 0, 0, j)),
    ]


def _gated_residual(i, tm, k_gate, y, h_ref, mod_ref):
    return h_ref[...] + _mod_row(mod_ref, k_gate, _is_ctx(i, tm)) * y


def _embed(x2, ctx2, row_tab, col_tab):
    seq = x2.shape[0]
    tm = 4 * GRID_W
    half = D_MODEL // 2
    n_tiles = (CTX_LEN + seq) // tm
    rows_per_tile = tm // GRID_W
    row_tab3 = row_tab.reshape(seq // tm, rows_per_tile, half)

    def body(ctx_ref, x_ref, row_ref, col_ref, out_ref):
        i = pl.program_id(0)

        @pl.when(i == 0)
        def _():
            out_ref[...] = ctx_ref[...]

        @pl.when(i > 0)
        def _():
            rt = row_ref[0]
            row_part = jnp.concatenate(
                [jnp.broadcast_to(rt[r:r + 1, :], (GRID_W, half)) for r in range(rows_per_tile)], axis=0)
            col_part = jnp.concatenate([col_ref[...]] * rows_per_tile, axis=0)
            out_ref[:, :half] = x_ref[:, :half] + row_part
            out_ref[:, half:] = x_ref[:, half:] + col_part

    assert CTX_LEN == tm
    return pl.pallas_call(
        body,
        grid=(n_tiles,),
        in_specs=[
            pl.BlockSpec((tm, D_MODEL), lambda i: (0, 0)),
            pl.BlockSpec((tm, D_MODEL), lambda i: (jnp.maximum(i - 1, 0), 0)),
            pl.BlockSpec((1, rows_per_tile, half), lambda i: (jnp.maximum(i - 1, 0), 0, 0)),
            pl.BlockSpec((GRID_W, half), lambda i: (0, 0)),
        ],
        out_specs=pl.BlockSpec((tm, D_MODEL), lambda i: (i, 0)),
        out_shape=jax.ShapeDtypeStruct((CTX_LEN + seq, D_MODEL), F32),
        compiler_params=_params(("arbitrary",)),
        name="embed",
    )(ctx2, x2, row_tab3, col_tab)


def _modulation(cond8, ada_w, ada_b):
    n = N_MOD * D_MODEL
    tn = 1024

    def body(c_ref, w_ref, b_ref, out_ref):
        a = _silu(c_ref[...]).astype(BF16)
        out_ref[...] = jnp.dot(a, w_ref[...].astype(BF16), preferred_element_type=F32) + b_ref[...]

    return pl.pallas_call(
        body,
        grid=(DEPTH, n // tn),
        in_specs=[
            pl.BlockSpec((8, D_MODEL), lambda l, j: (0, 0)),
            pl.BlockSpec((None, D_MODEL, tn), lambda l, j: (l, 0, j)),
            pl.BlockSpec((None, 1, tn), lambda l, j: (l, 0, j)),
        ],
        out_specs=pl.BlockSpec((None, 8, tn), lambda l, j: (l, 0, j)),
        out_shape=jax.ShapeDtypeStruct((DEPTH, 8, n), F32),
        compiler_params=_params(("arbitrary", "arbitrary")),
        name="modulation",
    )(cond8, ada_w, ada_b.reshape(DEPTH, 1, n))


def _s5_operators(lam_re, lam_im, log_dt, bt_re, bt_im, c_re, c_im):
    G, T, H, P, CW = S5_GROUPS, S5_CHUNK, S5_GROUP, S5_STATE, S5_CW

    def body(lre_ref, lim_ref, ldt_ref, btr_ref, bti_ref, cr_ref, ci_ref,
             kd_ref, we_ref, wo_ref, at_ref):
        slot = pl.program_id(0) % S5_LB_GROUPS
        row_blk = lax.broadcasted_iota(jnp.int32, (LANES, 2 * P), 0) // H
        kn, we_parts, wo_parts, at_parts = [], [], [], []
        for d in range(2):
            lr = jnp.minimum(lre_ref[d, 0], -1e-4)
            li = lim_ref[d, 0]
            dt = jnp.exp(ldt_ref[d, 0])
            mag = jnp.exp(lr * dt)
            ar = mag * jnp.cos(li * dt)
            ai = mag * jnp.sin(li * dt)
            xr = ar - 1.0
            xi = ai
            den = lr * lr + li * li
            kr = (xr * lr + xi * li) / den
            ki = (xi * lr - xr * li) / den
            btr, bti = btr_ref[d, 0], bti_ref[d, 0]
            bbr = kr * btr - ki * bti
            bbi = kr * bti + ki * btr
            cr, ci = cr_ref[d, 0], ci_ref[d, 0]
            pr = [jnp.ones((1, P), F32)]
            pi = [jnp.zeros((1, P), F32)]
            for _ in range(T):
                pr.append(pr[-1] * ar - pi[-1] * ai)
                pi.append(pr[-2] * ai + pi[-1] * ar)
            ba_re = [bbr * pr[l] - bbi * pi[l] for l in range(T)]
            ba_im = [bbr * pi[l] + bbi * pr[l] for l in range(T)]
            rising = list(range(T))
            falling = rising[::-1]
            e_order = falling if d == 0 else rising
            e_re = jnp.concatenate([ba_re[l] for l in e_order], axis=0)
            e_im = jnp.concatenate([ba_im[l] for l in e_order], axis=0)
            k_order = rising if d == 0 else falling
            k_lhs = jnp.concatenate([jnp.concatenate([ba_re[l] for l in k_order], axis=0),
                                     jnp.concatenate([ba_im[l] for l in k_order], axis=0)], axis=1)
            cmat = jnp.concatenate([cr, -ci], axis=1)
            cmat_slot = jnp.where(row_blk == slot, jnp.concatenate([cmat] * S5_LB_GROUPS, axis=0), 0.0)
            kn.append(lax.dot_general(k_lhs, cmat_slot, (((1,), (1,)), ((), ())), precision=HIGHEST,
                                      preferred_element_type=F32))
            lags = [t + 1 for t in range(T)] if d == 0 else [T - t for t in range(T)]
            ca_re = jnp.concatenate([cr * pr[l] - ci * pi[l] for l in lags], axis=0)
            ca_im = jnp.concatenate([cr * pi[l] + ci * pr[l] for l in lags], axis=0)
            we_parts.append((e_re, e_im))
            wo_parts.append((ca_re, -ca_im))
            at_parts.append((pr[T], pi[T]))
        kf, kb = kn
        kd_ref[0] = jnp.concatenate(
            [kb[:CW - H], kb[CW - H:] + kf[:H], kf[H:], jnp.zeros((H, LANES), F32)], axis=0).astype(BF16)
        we_ref[0] = jnp.concatenate(
            [we_parts[0][0], we_parts[1][0], we_parts[0][1], we_parts[1][1]], axis=1).astype(BF16)
        wo_ref[0] = jnp.concatenate(
            [wo_parts[0][0], wo_parts[1][0], wo_parts[0][1], wo_parts[1][1]], axis=1).astype(BF16)
        at_ref[0] = jnp.concatenate(
            [at_parts[0][0], at_parts[1][0], at_parts[0][1], at_parts[1][1]], axis=1)

    vec_spec = pl.BlockSpec((2, 1, 1, P), lambda g: (0, g, 0, 0))
    mat_spec = pl.BlockSpec((2, 1, H, P), lambda g: (0, g, 0, 0))
    op_spec = pl.BlockSpec((1, CW, CW), lambda g: (g, 0, 0))
    op_shape = jax.ShapeDtypeStruct((G, CW, CW), BF16)
    return pl.pallas_call(
        body,
        grid=(G,),
        in_specs=[vec_spec, vec_spec, pl.BlockSpec((2, 1, 1, 1), lambda g: (0, g, 0, 0)),
                  mat_spec, mat_spec, mat_spec, mat_spec],
        out_specs=[pl.BlockSpec((1, 2 * CW, LANES), lambda g: (g, 0, 0)), op_spec, op_spec,
                   pl.BlockSpec((1, 1, CW), lambda g: (g, 0, 0))],
        out_shape=[jax.ShapeDtypeStruct((G, 2 * CW, LANES), BF16), op_shape, op_shape,
                   jax.ShapeDtypeStruct((G, 1, CW), F32)],
        compiler_params=_params(("arbitrary",)),
        name="s5_operators",
    )(lam_re.reshape(2, G, 1, P), lam_im.reshape(2, G, 1, P), log_dt.reshape(2, G, 1, 1),
      bt_re, bt_im, c_re, c_im)


def _s5_gather_chunks(u2_ref, x_ref, sem, lb):
    copies = [pltpu.make_async_copy(u2_ref.at[:, pl.ds(s * D_MODEL + lb * LANES, LANES)],
                                    x_ref.at[:, pl.ds(s * LANES, LANES)], sem)
              for s in range(S5_CHUNK)]
    for cp in copies:
        cp.start()
    for cp in copies:
        cp.wait()


def _s5_block_diag(dst_ref, src_ref):
    dst_ref[...] = jnp.zeros_like(dst_ref)
    for g in range(S5_LB_GROUPS):
        for s in range(S5_CHUNK):
            dst_ref[s * LANES + g * S5_GROUP:s * LANES + (g + 1) * S5_GROUP, g * S5_CW:(g + 1) * S5_CW] = (
                src_ref[g, s * S5_GROUP:(s + 1) * S5_GROUP, :])


def _s5_local_state(u2, we):
    nc = u2.shape[0]
    k = S5_CHUNK * LANES
    n = S5_LB_GROUPS * S5_CW

    def body(u2_ref, we_ref, out_ref, x_ref, w_ref, sem):
        lb = pl.program_id(0)
        _s5_gather_chunks(u2_ref, x_ref, sem, lb)
        _s5_block_diag(w_ref, we_ref)
        states = jnp.dot(x_ref[...], w_ref[...], preferred_element_type=F32)
        for g in range(S5_LB_GROUPS):
            out_ref[:, g, :] = states[:, g * S5_CW:(g + 1) * S5_CW]

    return pl.pallas_call(
        body,
        grid=(D_MODEL // LANES,),
        in_specs=[pl.BlockSpec(memory_space=pl.ANY),
                  pl.BlockSpec((S5_LB_GROUPS, S5_CW, S5_CW), lambda lb: (lb, 0, 0))],
        out_specs=pl.BlockSpec((nc, S5_LB_GROUPS, S5_CW), lambda lb: (0, lb, 0)),
        out_shape=jax.ShapeDtypeStruct((nc, S5_GROUPS, S5_CW), F32),
        scratch_shapes=[pltpu.VMEM((nc, k), BF16), pltpu.VMEM((k, n), BF16), pltpu.SemaphoreType.DMA],
        compiler_params=_params(("arbitrary",)),
        name="s5_local_state",
    )(u2, we)


def _s5_carry(sloc_t, at):
    nc, G, cw = sloc_t.shape
    cb = CTX_LEN // S5_CHUNK
    nb = nc // cb
    half = cw // 2

    def rev_block(s):
        return jnp.where(s == 0, 0, nb - s)

    def body(sf_ref, sb_ref, at_ref, pf_ref, pb_ref, st_ref):
        s = pl.program_id(0)

        @pl.when(s == 0)
        def _():
            st_ref[...] = jnp.zeros_like(st_ref)

        is_fwd = (lax.broadcasted_iota(jnp.int32, (G, half), 1) < S5_STATE)
        a_re, a_im = at_ref[:, :half], at_ref[:, half:]
        s_re, s_im = st_ref[:, :half], st_ref[:, half:]
        for k in range(cb):
            kb = cb - 1 - k
            pf_ref[k, :, :half] = s_re
            pf_ref[k, :, half:] = s_im
            pb_ref[kb, :, :half] = s_re
            pb_ref[kb, :, half:] = s_im
            l_re = jnp.where(is_fwd, sf_ref[k, :, :half], sb_ref[kb, :, :half])
            l_im = jnp.where(is_fwd, sf_ref[k, :, half:], sb_ref[kb, :, half:])
            n_re = a_re * s_re - a_im * s_im + l_re
            n_im = a_re * s_im + a_im * s_re + l_im
            s_re, s_im = n_re, n_im
        st_ref[:, :half] = s_re
        st_ref[:, half:] = s_im

    blk = (cb, G, cw)
    return pl.pallas_call(
        body,
        grid=(nb,),
        in_specs=[pl.BlockSpec(blk, lambda s: (s, 0, 0)),
                  pl.BlockSpec(blk, lambda s: (rev_block(s), 0, 0)),
                  pl.BlockSpec((G, cw), lambda s: (0, 0))],
        out_specs=[pl.BlockSpec(blk, lambda s: (s, 0, 0)),
                   pl.BlockSpec(blk, lambda s: (rev_block(s), 0, 0))],
        out_shape=[jax.ShapeDtypeStruct((nc, G, cw), F32)] * 2,
        scratch_shapes=[pltpu.VMEM((G, cw), F32)],
        compiler_params=_params(("arbitrary",)),
        name="s5_carry",
    )(sloc_t, sloc_t, at)


def _s5_output(u2, kd, wo, pf, pb, d3, j):
    nc = u2.shape[0]
    T = S5_CHUNK
    k = T * LANES
    n = S5_LB_GROUPS * S5_CW
    nt = (((1,), (1,)), ((), ()))

    def body(u2_ref, kd_ref, wo_ref, pf_ref, pb_ref, d_ref, y2_ref,
             x_ref, toep_ref, w_ref, bd_ref, y_ref, sem_in, sem_out):
        lb = pl.program_id(0)
        _s5_gather_chunks(u2_ref, x_ref, sem_in, lb)
        for lag in range(2 * T - 1):
            for g in range(S5_LB_GROUPS):
                bd_ref[lag * LANES + g * S5_GROUP:lag * LANES + (g + 1) * S5_GROUP, :] = (
                    kd_ref[g, lag * S5_GROUP:(lag + 1) * S5_GROUP, :])
        for s in range(T):
            for t in range(T):
                lag = t - s + T - 1
                toep_ref[s * LANES:(s + 1) * LANES, t * LANES:(t + 1) * LANES] = (
                    bd_ref[lag * LANES:(lag + 1) * LANES, :])
        _s5_block_diag(w_ref, wo_ref)
        is_fwd = (lax.broadcasted_iota(jnp.int32, (nc, n), 1) % LANES) < S5_STATE
        p = jnp.where(is_fwd, pf_ref[...], pb_ref[...]).astype(BF16)
        x = x_ref[...]
        y = jnp.dot(x, toep_ref[...], preferred_element_type=F32)
        y = y + lax.dot_general(p, w_ref[...], nt, preferred_element_type=F32)
        y = y + jnp.concatenate([d_ref[...]] * T, axis=1) * x.astype(F32)
        y_ref[...] = y.astype(BF16)
        copies = [pltpu.make_async_copy(y_ref.at[:, pl.ds(t * LANES, LANES)],
                                        y2_ref.at[:, pl.ds(t * D_MODEL + lb * LANES, LANES)], sem_out)
                  for t in range(T)]
        for cp in copies:
            cp.start()
        for cp in copies:
            cp.wait()

    once = pl.Buffered(1)
    state = pl.BlockSpec((nc, n), lambda lb: (0, lb), pipeline_mode=once)
    return pl.pallas_call(
        body,
        grid=(D_MODEL // LANES,),
        in_specs=[pl.BlockSpec(memory_space=pl.ANY),
                  pl.BlockSpec((S5_LB_GROUPS, 2 * S5_CW, LANES), lambda lb: (lb, 0, 0)),
                  pl.BlockSpec((S5_LB_GROUPS, S5_CW, S5_CW), lambda lb: (lb, 0, 0)),
                  state, state,
                  pl.BlockSpec((None, 1, LANES), lambda lb: (j, 0, lb))],
        out_specs=pl.BlockSpec(memory_space=pl.ANY),
        out_shape=jax.ShapeDtypeStruct((nc, T * D_MODEL), BF16),
        scratch_shapes=[pltpu.VMEM((nc, k), BF16), pltpu.VMEM((k, k), BF16), pltpu.VMEM((k, n), BF16),
                        pltpu.VMEM((2 * T * LANES, LANES), BF16), pltpu.VMEM((nc, k), BF16),
                        pltpu.SemaphoreType.DMA, pltpu.SemaphoreType.DMA],
        compiler_params=_params(("arbitrary",)),
        name="s5_output",
    )(u2, kd, wo, pf, pb, d3)


def _s5_layer(h, layer, j, mod, norm_g, p):
    n_rows = h.shape[0]
    G, T, H = S5_GROUPS, S5_CHUNK, S5_GROUP
    nc = n_rows // T
    tm = _pick(n_rows, MIX_ROW_TILES)
    tn = 512
    u = _fused_matmul(
        name="s5_in", n_rows=n_rows, k_dim=D_MODEL, n_out=D_MODEL, tm=tm, tn=tn,
        pro_args=(h, norm_g, mod), pro_specs=_norm_specs(layer, tm),
        pro_fn=lambda i, a_ref, h_ref, g_ref, m_ref: _norm_mod_into(a_ref, i, tm, h_ref, g_ref, m_ref, 0),
        w=p['s5_w_in'], w_layer=j, w_col_offsets=(0,),
        epi_args=(), epi_specs=(), epi_fn=lambda i, jj, ys: ys[0], out_dtype=BF16)
    u2 = u.reshape(nc, T * D_MODEL)
    kd, we, wo, at = _s5_operators(
        p['s5_lam_re'][j], p['s5_lam_im'][j], p['s5_log_dt'][j],
        jnp.swapaxes(p['s5_b_re'][j], -1, -2), jnp.swapaxes(p['s5_b_im'][j], -1, -2),
        p['s5_c_re'][j], p['s5_c_im'][j])
    sloc = _s5_local_state(u2, we)
    pf, pb = _s5_carry(sloc, at.reshape(G, T * H))
    d3 = p['s5_d'].reshape(-1, 1, D_MODEL)
    y2 = _s5_output(u2, kd, wo, pf.reshape(nc, G * T * H), pb.reshape(nc, G * T * H), d3, j)
    y = y2.reshape(n_rows, D_MODEL)

    def gelu_into(i, a_ref, y_ref):
        a_ref[...] = _gelu_tanh(y_ref[...].astype(F32)).astype(BF16)

    return _fused_matmul(
        name="s5_out", n_rows=n_rows, k_dim=D_MODEL, n_out=D_MODEL, tm=tm, tn=tn,
        pro_args=(y,), pro_specs=[pl.BlockSpec((tm, D_MODEL), lambda i, jj: (i, 0), pipeline_mode=pl.Buffered(1))],
        pro_fn=gelu_into,
        w=p['s5_w_out'], w_layer=j, w_col_offsets=(0, D_MODEL),
        epi_args=(h, mod), epi_specs=_res_specs(layer, tm, tn),
        epi_fn=lambda i, jj, ys, h_ref, m_ref: _gated_residual(
            i, tm, 2, ys[0] * _sigmoid(ys[1]), h_ref, m_ref),
        out_dtype=F32)


def _sg_layer(h, layer, j, mod, norm_g, p):
    n_rows = h.shape[0]
    tm = _pick(n_rows, MIX_ROW_TILES)
    tn = 512
    z = _fused_matmul(
        name="sg_in", n_rows=n_rows, k_dim=D_MODEL, n_out=2 * D_MODEL, tm=tm, tn=tn,
        pro_args=(h, norm_g, mod), pro_specs=_norm_specs(layer, tm),
        pro_fn=lambda i, a_ref, h_ref, g_ref, m_ref: _norm_mod_into(a_ref, i, tm, h_ref, g_ref, m_ref, 0),
        w=p['sg_w_in'], w_layer=j, w_col_offsets=(0,),
        epi_args=(), epi_specs=(), epi_fn=lambda i, jj, ys: _gelu_tanh(ys[0]), out_dtype=BF16)

    def gate(i, a_ref, u_ref, v_ref, g_ref, b_ref, ws_ref, bs_ref):
        ln_g, ln_b = g_ref[...], b_ref[...]

        def chunk(c, carry):
            rows = pl.ds(pl.multiple_of(c * SG_CHUNK, SG_CHUNK), SG_CHUNK)
            v = _layer_norm(v_ref[rows, :].astype(F32), ln_g, ln_b).astype(BF16)
            for hd in range(SG_HEADS):
                cols = slice(hd * SG_HEAD_DIM, (hd + 1) * SG_HEAD_DIM)
                sv = jnp.dot(ws_ref[hd].astype(BF16), v[:, cols], preferred_element_type=F32)
                gated = u_ref[rows, cols].astype(F32) * (sv + bs_ref[:, hd:hd + 1])
                a_ref[rows, cols] = gated.astype(BF16)
            return carry

        lax.fori_loop(0, tm // SG_CHUNK, chunk, 0)

    return _fused_matmul(
        name="sg_out", n_rows=n_rows, k_dim=D_MODEL, n_out=D_MODEL, tm=tm, tn=tn,
        pro_args=(z, z, p['sg_ln_g'], p['sg_ln_b'], p['sg_w_s'], jnp.swapaxes(p['sg_b_s'], -1, -2)),
        pro_specs=[
            pl.BlockSpec((tm, D_MODEL), lambda i, jj: (i, 0), pipeline_mode=pl.Buffered(1)),
            pl.BlockSpec((tm, D_MODEL), lambda i, jj: (i, 1), pipeline_mode=pl.Buffered(1)),
            pl.BlockSpec((1, D_MODEL), lambda i, jj: (j, 0)),
            pl.BlockSpec((1, D_MODEL), lambda i, jj: (j, 0)),
            pl.BlockSpec((None, SG_HEADS, SG_CHUNK, SG_CHUNK), lambda i, jj: (j, 0, 0, 0)),
            pl.BlockSpec((None, SG_CHUNK, SG_HEADS), lambda i, jj: (j, 0, 0)),
        ],
        pro_fn=gate,
        w=p['sg_w_out'], w_layer=j, w_col_offsets=(0,),
        epi_args=(h, mod), epi_specs=_res_specs(layer, tm, tn),
        epi_fn=lambda i, jj, ys, h_ref, m_ref: _gated_residual(i, tm, 2, ys[0], h_ref, m_ref),
        out_dtype=F32)


def _conv_norm(z, dw_w, dw_b, ln_g, ln_b, j):
    n_rows = z.shape[0]
    tm = CTX_LEN
    n_tiles = n_rows // tm
    hb = tm // CONV_HALO
    half = CONV_WIDTH // 2
    rb, cbw = 64, LANES

    def body(prev_ref, cur_ref, next_ref, w_ref, b_ref, g_ref, beta_ref, out_ref, zp_ref, acc_ref):
        i = pl.program_id(0)
        prev_ok = i >= 2
        next_ok = jnp.logical_and(i >= 1, i < n_tiles - 1)
        zp_ref[0:CONV_HALO, :] = jnp.where(prev_ok, prev_ref[...], 0.0)
        zp_ref[CONV_HALO:CONV_HALO + tm, :] = cur_ref[...]
        zp_ref[CONV_HALO + tm:, :] = jnp.where(next_ok, next_ref[...], 0.0)

        def col_block(cb, carry):
            c0 = pl.multiple_of(cb * cbw, cbw)
            w = w_ref[:, pl.ds(c0, cbw)]
            for r in range(tm // rb):
                win = zp_ref[r * rb:r * rb + rb + 2 * CONV_HALO, pl.ds(c0, cbw)]
                span = rb + 2 * CONV_HALO - SUBLANES
                phases = [win[ph:ph + span, :] for ph in range(SUBLANES)]
                acc = jnp.zeros((rb, cbw), F32)
                for k in range(CONV_WIDTH):
                    off = CONV_HALO - half + k
                    base = off - off % SUBLANES
                    acc = acc + phases[off % SUBLANES][base:base + rb, :] * w[k:k + 1, :]
                acc_ref[r * rb:(r + 1) * rb, pl.ds(c0, cbw)] = acc
            return carry

        lax.fori_loop(0, D_MODEL // cbw, col_block, 0)
        y = _layer_norm(acc_ref[...] + b_ref[...], g_ref[...], beta_ref[...])
        out_ref[...] = _silu(y).astype(BF16)

    vec = pl.BlockSpec((1, D_MODEL), lambda i: (j, 0))
    return pl.pallas_call(
        body,
        grid=(n_tiles,),
        in_specs=[
            pl.BlockSpec((CONV_HALO, D_MODEL), lambda i: (jnp.maximum(i * hb - 1, 0), 0)),
            pl.BlockSpec((tm, D_MODEL), lambda i: (i, 0)),
            pl.BlockSpec((CONV_HALO, D_MODEL), lambda i: (jnp.minimum((i + 1) * hb, n_tiles * hb - 1), 0)),
            pl.BlockSpec((None, CONV_WIDTH, D_MODEL), lambda i: (j, 0, 0)),
            vec, vec, vec,
        ],
        out_specs=pl.BlockSpec((tm, D_MODEL), lambda i: (i, 0)),
        out_shape=jax.ShapeDtypeStruct((n_rows, D_MODEL), BF16),
        scratch_shapes=[pltpu.VMEM((tm + 2 * CONV_HALO, D_MODEL), F32), pltpu.VMEM((tm, D_MODEL), F32)],
        compiler_params=_params(("arbitrary",)),
        name="conv_norm",
    )(z, z, z, dw_w, dw_b, ln_g, ln_b)


def _conv_layer(h, layer, j, mod, norm_g, p):
    n_rows = h.shape[0]
    tm = _pick(n_rows, MIX_ROW_TILES)
    tn = 512
    z = _fused_matmul(
        name="cv_pw1", n_rows=n_rows, k_dim=D_MODEL, n_out=D_MODEL, tm=tm, tn=tn,
        pro_args=(h, norm_g, mod), pro_specs=_norm_specs(layer, tm),
        pro_fn=lambda i, a_ref, h_ref, g_ref, m_ref: _norm_mod_into(a_ref, i, tm, h_ref, g_ref, m_ref, 0),
        w=p['cv_w_pw1'], w_layer=j, w_col_offsets=(0, D_MODEL),
        epi_args=(), epi_specs=(), epi_fn=lambda i, jj, ys: ys[0] * _sigmoid(ys[1]), out_dtype=F32)
    zc = _conv_norm(z, p['cv_dw_w'], p['cv_dw_b'], p['cv_ln_g'], p['cv_ln_b'], j)

    def copy_into(i, a_ref, z_ref):
        a_ref[...] = z_ref[...]

    return _fused_matmul(
        name="cv_pw2", n_rows=n_rows, k_dim=D_MODEL, n_out=D_MODEL, tm=tm, tn=tn,
        pro_args=(zc,), pro_specs=[pl.BlockSpec((tm, D_MODEL), lambda i, jj: (i, 0), pipeline_mode=pl.Buffered(1))],
        pro_fn=copy_into,
        w=p['cv_w_pw2'], w_layer=j, w_col_offsets=(0,),
        epi_args=(h, mod), epi_specs=_res_specs(layer, tm, tn),
        epi_fn=lambda i, jj, ys, h_ref, m_ref: _gated_residual(i, tm, 2, ys[0], h_ref, m_ref),
        out_dtype=F32)


SLAB_RANK1, SLAB_RANK2, SLAB_EXP1, SLAB_EXP2, SLAB_W1, SLAB_W2 = range(6)


def _router(h, layer, k, mod, norm_g, w_router):
    n_rows = h.shape[0]
    tm = _pick(n_rows, (768, 256))
    wr = jnp.pad(w_router[k], ((0, 0), (0, LANES - N_EXPERTS)))

    def body(h_ref, g_ref, m_ref, wr_ref, hn_ref, slab_ref, cnt_ref, base_ref):
        i = pl.program_id(0)

        @pl.when(i == 0)
        def _():
            base_ref[...] = jnp.zeros_like(base_ref)

        hn = _norm_mod(i, tm, h_ref, g_ref, m_ref, 3)
        hn_ref[...] = hn
        logits = jnp.dot(hn, wr_ref[...], precision=HIGHEST, preferred_element_type=F32)
        lane = lax.broadcasted_iota(jnp.int32, (tm, LANES), 1).astype(F32)
        neg = jnp.float32(-jnp.inf)
        lg = jnp.where(lane < N_EXPERTS, logits, neg)
        m1 = jnp.max(lg, axis=1, keepdims=True)
        i1 = jnp.min(jnp.where(lg == m1, lane, float(LANES)), axis=1, keepdims=True)
        lg2 = jnp.where(lane == i1, neg, lg)
        m2 = jnp.max(lg2, axis=1, keepdims=True)
        i2 = jnp.min(jnp.where(lg2 == m2, lane, float(LANES)), axis=1, keepdims=True)
        e = jnp.exp(m2 - m1)
        w1 = 1.0 / (1.0 + e)
        w2 = e / (1.0 + e)
        onehot = jnp.where(lane == i1, 1.0, 0.0) + jnp.where(lane == i2, 1.0, 0.0)
        tri = jnp.where(lax.broadcasted_iota(jnp.int32, (tm, tm), 1)
                        < lax.broadcasted_iota(jnp.int32, (tm, tm), 0), 1.0, 0.0).astype(BF16)
        base = base_ref[0:1, :]
        excl = jnp.dot(tri, onehot.astype(BF16), preferred_element_type=F32) + base
        r1 = jnp.sum(jnp.where(lane == i1, excl, 0.0), axis=1, keepdims=True)
        r2 = jnp.sum(jnp.where(lane == i2, excl, 0.0), axis=1, keepdims=True)
        total = base + jnp.sum(onehot, axis=0, keepdims=True)
        base_ref[...] = jnp.broadcast_to(total, base_ref.shape)
        cnt_ref[...] = jnp.broadcast_to(total, cnt_ref.shape)
        slab = jnp.zeros((tm, LANES), F32)
        for ln, val in ((SLAB_RANK1, r1), (SLAB_RANK2, r2), (SLAB_EXP1, i1), (SLAB_EXP2, i2),
                        (SLAB_W1, w1), (SLAB_W2, w2)):
            slab = jnp.where(lane == float(ln), val, slab)
        slab_ref[...] = slab

    return pl.pallas_call(
        body,
        grid=(n_rows // tm,),
        in_specs=[
            pl.BlockSpec((tm, D_MODEL), lambda i: (i, 0)),
            pl.BlockSpec((None, 1, D_MODEL), lambda i: (layer, 0, 0)),
            pl.BlockSpec((None, 2, N_MOD, D_MODEL), lambda i: (layer, 0, 0, 0)),
            pl.BlockSpec((D_MODEL, LANES), lambda i: (0, 0)),
        ],
        out_specs=[pl.BlockSpec((tm, D_MODEL), lambda i: (i, 0)),
                   pl.BlockSpec((tm, LANES), lambda i: (i, 0)),
                   pl.BlockSpec((8, LANES), lambda i: (0, 0))],
        out_shape=[jax.ShapeDtypeStruct((n_rows, D_MODEL), F32),
                   jax.ShapeDtypeStruct((n_rows, LANES), F32),
                   jax.ShapeDtypeStruct((8, LANES), F32)],
        scratch_shapes=[pltpu.VMEM((8, LANES), F32)],
        compiler_params=_params(("arbitrary",)),
        name="router",
    )(h, norm_g, mod, wr)


def _routing_tables(slab, counts, tb, nblk):
    cnt = counts[0, :N_EXPERTS].astype(jnp.int32)
    padded = ((cnt + tb - 1) // tb) * tb
    ends = jnp.cumsum(padded)
    starts = ends - padded
    pos1 = starts[slab[:, SLAB_EXP1].astype(jnp.int32)] + slab[:, SLAB_RANK1].astype(jnp.int32)
    pos2 = starts[slab[:, SLAB_EXP2].astype(jnp.int32)] + slab[:, SLAB_RANK2].astype(jnp.int32)
    nvalid = ends[-1] // tb
    blk = jnp.minimum(jnp.arange(nblk, dtype=jnp.int32), nvalid - 1)
    blk_expert = jnp.minimum(jnp.searchsorted(ends, blk * tb, side='right'), N_EXPERTS - 1).astype(jnp.int32)
    left = cnt[blk_expert] - (blk * tb - starts[blk_expert])
    blk_rows = ((jnp.clip(left, 0, tb) + MOE_SUB - 1) // MOE_SUB) * MOE_SUB
    return pos1, pos2, blk_expert, blk_rows.astype(jnp.int32), nvalid.reshape(1).astype(jnp.int32)


def _dispatch(hn, pos1, pos2, n_slots):
    n_rows = hn.shape[0]
    tm = CTX_LEN

    def body(p1_ref, p2_ref, hn_ref, init_ref, xs_ref, sem):
        i = pl.program_id(0)

        def row_copy(t, p):
            return pltpu.make_async_copy(hn_ref.at[pl.ds(t, 1)], xs_ref.at[pl.ds(p, 1)], sem)

        def issue(t, carry):
            row_copy(t, p1_ref[i * tm + t]).start()
            row_copy(t, p2_ref[i * tm + t]).start()
            return carry

        def drain(t, carry):
            row_copy(0, 0).wait()
            row_copy(0, 0).wait()
            return carry

        lax.fori_loop(0, tm, issue, 0, unroll=8)
        lax.fori_loop(0, tm, drain, 0, unroll=8)

    return pl.pallas_call(
        body,
        grid_spec=pltpu.PrefetchScalarGridSpec(
            num_scalar_prefetch=2,
            grid=(n_rows // tm,),
            in_specs=[pl.BlockSpec((tm, D_MODEL), lambda i, p1, p2: (i, 0)),
                      pl.BlockSpec(memory_space=pl.ANY)],
            out_specs=pl.BlockSpec(memory_space=pl.ANY),
            scratch_shapes=[pltpu.SemaphoreType.DMA],
        ),
        out_shape=jax.ShapeDtypeStruct((n_slots, D_MODEL), F32),
        input_output_aliases={3: 0},
        compiler_params=_params(("arbitrary",)),
        name="moe_dispatch",
    )(pos1, pos2, hn, jnp.zeros((n_slots, D_MODEL), F32))


def _swiglu_chunk(a, wg_ref, wu_ref, wd_ref):
    g = jnp.dot(a, wg_ref[...].astype(BF16), preferred_element_type=F32)
    u = jnp.dot(a, wu_ref[...].astype(BF16), preferred_element_type=F32)
    return jnp.dot((_silu(g) * u).astype(BF16), wd_ref[...].astype(BF16), preferred_element_type=F32)


def _moe_ffn(xs, blk_expert, blk_rows, nvalid, w_gate, w_up, w_down, e_base, tb):
    n_slots = xs.shape[0]
    d_ff = w_gate.shape[-1]
    tf = MOE_FFN_CHUNK
    nf = d_ff // tf

    def body(be_ref, br_ref, nv_ref, x_ref, wg_ref, wu_ref, wd_ref, out_ref, a_ref, wg_bf, wu_bf, wd_bf):
        b = pl.program_id(0)
        f = pl.program_id(1)
        used = b < nv_ref[0]

        @pl.when(jnp.logical_and(used, f == 0))
        def _():
            a_ref[...] = x_ref[...].astype(BF16)
            out_ref[...] = jnp.zeros_like(out_ref)

        @pl.when(used)
        def _():
            wg_bf[...] = wg_ref[...].astype(BF16)
            wu_bf[...] = wu_ref[...].astype(BF16)
            wd_bf[...] = wd_ref[...].astype(BF16)

        for sub in range(tb // MOE_SUB):
            @pl.when(jnp.logical_and(used, sub * MOE_SUB < br_ref[b]))
            def _():
                rows = slice(sub * MOE_SUB, (sub + 1) * MOE_SUB)
                a = a_ref[rows, :]
                g = jnp.dot(a, wg_bf[...], preferred_element_type=F32)
                u = jnp.dot(a, wu_bf[...], preferred_element_type=F32)
                out_ref[rows, :] += jnp.dot((_silu(g) * u).astype(BF16), wd_bf[...], preferred_element_type=F32)

        @pl.when(jnp.logical_and(jnp.logical_not(used), f == 0))
        def _():
            out_ref[...] = jnp.zeros_like(out_ref)

    def f_idx(b, f, nv):
        return jnp.where(b < nv[0], f, nf - 1)

    once = pl.Buffered(1)
    rows = lambda b, f, be, br, nv: (jnp.minimum(b, nv[0] - 1), 0)
    return pl.pallas_call(
        body,
        grid_spec=pltpu.PrefetchScalarGridSpec(
            num_scalar_prefetch=3,
            grid=(n_slots // tb, nf),
            in_specs=[
                pl.BlockSpec((tb, D_MODEL), rows, pipeline_mode=once),
                pl.BlockSpec((None, D_MODEL, tf), lambda b, f, be, br, nv: (e_base + be[b], 0, f_idx(b, f, nv))),
                pl.BlockSpec((None, D_MODEL, tf), lambda b, f, be, br, nv: (e_base + be[b], 0, f_idx(b, f, nv))),
                pl.BlockSpec((None, tf, D_MODEL), lambda b, f, be, br, nv: (e_base + be[b], f_idx(b, f, nv), 0)),
            ],
            out_specs=pl.BlockSpec((tb, D_MODEL), lambda b, f, be, br, nv: (b, 0), pipeline_mode=once),
            scratch_shapes=[pltpu.VMEM((tb, D_MODEL), BF16), pltpu.VMEM((D_MODEL, tf), BF16),
                            pltpu.VMEM((D_MODEL, tf), BF16), pltpu.VMEM((tf, D_MODEL), BF16)],
        ),
        out_shape=jax.ShapeDtypeStruct((n_slots, D_MODEL), F32),
        compiler_params=_params(("arbitrary", "arbitrary")),
        name="moe_ffn",
    )(blk_expert, blk_rows, nvalid, xs, w_gate, w_up, w_down)


def _combine(h, ys, slab, pos1, pos2, layer, mod):
    n_rows = h.shape[0]
    tm = CTX_LEN

    def body(p1_ref, p2_ref, h_ref, slab_ref, m_ref, ys_ref, out_ref, y1_ref, y2_ref, sem):
        i = pl.program_id(0)

        def row_copy(p, dst_ref, t):
            return pltpu.make_async_copy(ys_ref.at[pl.ds(p, 1)], dst_ref.at[pl.ds(t, 1)], sem)

        def issue(t, carry):
            row_copy(p1_ref[i * tm + t], y1_ref, t).start()
            row_copy(p2_ref[i * tm + t], y2_ref, t).start()
            return carry

        def drain(t, carry):
            row_copy(0, y1_ref, 0).wait()
            row_copy(0, y2_ref, 0).wait()
            return carry

        lax.fori_loop(0, tm, issue, 0, unroll=8)
        lax.fori_loop(0, tm, drain, 0, unroll=8)
        w1 = slab_ref[:, SLAB_W1:SLAB_W1 + 1]
        w2 = slab_ref[:, SLAB_W2:SLAB_W2 + 1]
        moe = w1 * y1_ref[...] + w2 * y2_ref[...]
        out_ref[...] = h_ref[...] + _mod_row(m_ref, 5, _is_ctx(i, tm)) * moe

    return pl.pallas_call(
        body,
        grid_spec=pltpu.PrefetchScalarGridSpec(
            num_scalar_prefetch=2,
            grid=(n_rows // tm,),
            in_specs=[
                pl.BlockSpec((tm, D_MODEL), lambda i, p1, p2: (i, 0)),
                pl.BlockSpec((tm, LANES), lambda i, p1, p2: (i, 0)),
                pl.BlockSpec((None, 2, N_MOD, D_MODEL), lambda i, p1, p2: (layer, 0, 0, 0)),
                pl.BlockSpec(memory_space=pl.ANY),
            ],
            out_specs=pl.BlockSpec((tm, D_MODEL), lambda i, p1, p2: (i, 0)),
            scratch_shapes=[pltpu.VMEM((tm, D_MODEL), F32), pltpu.VMEM((tm, D_MODEL), F32),
                            pltpu.SemaphoreType.DMA],
        ),
        out_shape=jax.ShapeDtypeStruct((n_rows, D_MODEL), F32),
        compiler_params=_params(("arbitrary",)),
        name="moe_combine",
    )(pos1, pos2, h, slab, mod, ys)


def _moe_layer(h, layer, k, mod, norm_g, w_router, w_gate, w_up, w_down):
    n_rows = h.shape[0]
    d_ff = w_gate.shape[-1]
    tb = 1024 if n_rows >= 4096 else 256
    nblk = -(-(TOP_K * n_rows + N_EXPERTS * (tb - 1)) // tb)
    hn, slab, counts = _router(h, layer, k, mod, norm_g, w_router)
    pos1, pos2, blk_expert, blk_rows, nvalid = _routing_tables(slab, counts, tb, nblk)
    xs = _dispatch(hn, pos1, pos2, nblk * tb)
    ys = _moe_ffn(xs, blk_expert, blk_rows, nvalid, w_gate.reshape(-1, D_MODEL, d_ff),
                  w_up.reshape(-1, D_MODEL, d_ff), w_down.reshape(-1, d_ff, D_MODEL), k * N_EXPERTS, tb)
    return _combine(h, ys, slab, pos1, pos2, layer, mod)


def _ffn(h, layer, mod, norm_g, w_gate, w_up, w_down, k):
    n_rows = h.shape[0]
    d_ff = w_gate.shape[-1]
    tm = _pick(n_rows, (768, 256))
    tf = FFN_CHUNK
    nf = d_ff // tf

    def body(h_ref, g_ref, m_ref, wg_ref, wu_ref, wd_ref, out_ref, a_ref):
        i = pl.program_id(0)
        f = pl.program_id(1)

        @pl.when(f == 0)
        def _():
            _norm_mod_into(a_ref, i, tm, h_ref, g_ref, m_ref, 3)
            out_ref[...] = jnp.zeros_like(out_ref)

        out_ref[...] += _swiglu_chunk(a_ref[...], wg_ref, wu_ref, wd_ref)

        @pl.when(f == nf - 1)
        def _():
            def chunk(r, carry):
                rows = pl.ds(pl.multiple_of(r * MOD_CHUNK, MOD_CHUNK), MOD_CHUNK)
                m = jnp.where(i * tm + r * MOD_CHUNK < CTX_LEN, 1, 0)
                out_ref[rows, :] = h_ref[rows, :] + m_ref[m, pl.ds(5, 1), :] * out_ref[rows, :]
                return carry

            lax.fori_loop(0, tm // MOD_CHUNK, chunk, 0)

    once = pl.Buffered(1)
    return pl.pallas_call(
        body,
        grid=(n_rows // tm, nf),
        in_specs=[
            pl.BlockSpec((tm, D_MODEL), lambda i, f: (i, 0), pipeline_mode=once),
            pl.BlockSpec((None, 1, D_MODEL), lambda i, f: (layer, 0, 0)),
            pl.BlockSpec((None, 2, N_MOD, D_MODEL), lambda i, f: (layer, 0, 0, 0)),
            pl.BlockSpec((None, D_MODEL, tf), lambda i, f: (k, 0, f)),
            pl.BlockSpec((None, D_MODEL, tf), lambda i, f: (k, 0, f)),
            pl.BlockSpec((None, tf, D_MODEL), lambda i, f: (k, f, 0)),
        ],
        out_specs=pl.BlockSpec((tm, D_MODEL), lambda i, f: (i, 0), pipeline_mode=once),
        out_shape=jax.ShapeDtypeStruct((n_rows, D_MODEL), F32),
        scratch_shapes=[pltpu.VMEM((tm, D_MODEL), BF16)],
        compiler_params=_params(("arbitrary", "arbitrary")),
        name="ffn",
    )(h, norm_g, mod, w_gate, w_up, w_down)


def _final_norm(h, final_g):
    n_rows = h.shape[0]
    tm = CTX_LEN

    def body(h_ref, g_ref, out_ref):
        x = h_ref[...]
        out_ref[...] = x * lax.rsqrt(jnp.mean(x * x, axis=-1, keepdims=True) + EPS) * g_ref[...]

    return pl.pallas_call(
        body,
        grid=((n_rows - CTX_LEN) // tm,),
        in_specs=[pl.BlockSpec((tm, D_MODEL), lambda i: (i + 1, 0)),
                  pl.BlockSpec((1, D_MODEL), lambda i: (0, 0))],
        out_specs=pl.BlockSpec((tm, D_MODEL), lambda i: (i, 0)),
        out_shape=jax.ShapeDtypeStruct((n_rows - CTX_LEN, D_MODEL), F32),
        compiler_params=_params(("arbitrary",)),
        name="final_norm",
    )(h, final_g.reshape(1, D_MODEL))


def _sincos_tables(rows):
    quarter = D_MODEL // 4
    omega = 1.0 / (POS_BASE ** (jnp.arange(quarter, dtype=F32) / quarter))

    def emb(pos):
        ang = pos[:, None] * omega[None, :]
        return jnp.concatenate([jnp.sin(ang), jnp.cos(ang)], axis=-1)

    return emb(jnp.arange(rows, dtype=F32)), emb(jnp.arange(GRID_W, dtype=F32))


def kernel(x, c, ctx, c_ctx, ada_w, ada_b, norm_mix_g, norm_ffn_g, final_g, s5_w_in, s5_lam_re, s5_lam_im, s5_log_dt, s5_b_re, s5_b_im, s5_c_re, s5_c_im, s5_d, s5_w_out, sg_w_in, sg_ln_g, sg_ln_b, sg_w_s, sg_b_s, sg_w_out, cv_w_pw1, cv_dw_w, cv_dw_b, cv_ln_g, cv_ln_b, cv_w_pw2, ff_w_gate, ff_w_up, ff_w_down, moe_w_router, moe_w_gate, moe_w_up, moe_w_down):
    bsz, seq, dim = x.shape
    assert bsz == 1 and dim == D_MODEL and ctx.shape == (1, CTX_LEN, D_MODEL)
    p = dict(s5_w_in=s5_w_in, s5_lam_re=s5_lam_re, s5_lam_im=s5_lam_im, s5_log_dt=s5_log_dt,
             s5_b_re=s5_b_re, s5_b_im=s5_b_im, s5_c_re=s5_c_re, s5_c_im=s5_c_im, s5_d=s5_d,
             s5_w_out=s5_w_out, sg_w_in=sg_w_in, sg_ln_g=sg_ln_g, sg_ln_b=sg_ln_b, sg_w_s=sg_w_s,
             sg_b_s=sg_b_s, sg_w_out=sg_w_out, cv_w_pw1=cv_w_pw1, cv_dw_w=cv_dw_w, cv_dw_b=cv_dw_b,
             cv_ln_g=cv_ln_g, cv_ln_b=cv_ln_b, cv_w_pw2=cv_w_pw2)
    row_tab, col_tab = _sincos_tables(seq // GRID_W)
    h = _embed(x[0], ctx[0], row_tab, col_tab)
    cond8 = jnp.concatenate([c, c_ctx[None, :], jnp.zeros((6, D_MODEL), F32)], axis=0)
    mod = _modulation(cond8, ada_w, ada_b)[:, :2, :].reshape(DEPTH, 2, N_MOD, D_MODEL)
    g_mix = norm_mix_g.reshape(DEPTH, 1, D_MODEL)
    g_ffn = norm_ffn_g.reshape(DEPTH, 1, D_MODEL)
    for layer in range(DEPTH):
        kind, j = layer % N_MIXERS, layer // N_MIXERS
        if kind == 0:
            h = _s5_layer(h, layer, j, mod, g_mix, p)
        elif kind == 1:
            h = _sg_layer(h, layer, j, mod, g_mix, p)
        else:
            h = _conv_layer(h, layer, j, mod, g_mix, p)
        k = layer // 2
        if layer % 2 == 0:
            h = _ffn(h, layer, mod, g_ffn, ff_w_gate, ff_w_up, ff_w_down, k)
        else:
            h = _moe_layer(h, layer, k, mod, g_ffn, moe_w_router, moe_w_gate, moe_w_up, moe_w_down)
    return _final_norm(h, final_g)[None]
```

```python
import functools
import math

import jax
import jax.numpy as jnp
from jax import lax
from jax.experimental import pallas as pl
from jax.experimental.pallas import tpu as pltpu

F32 = jnp.float32
BF16 = jnp.bfloat16
HIGHEST = lax.Precision.HIGHEST

D_MODEL = 2048
DEPTH = 4
GRID_W = 64
CTX_LEN = 256
POS_BASE = 10000.0
N_MIXERS = 3
N_MOD = 6
EPS = 1e-6
S5_GROUP = 16
S5_GROUPS = D_MODEL // S5_GROUP
S5_STATE = 64
S5_CHUNK = 16
S5_CW = S5_CHUNK * S5_GROUP
SG_CHUNK = 128
SG_HEADS = 8
SG_HEAD_DIM = D_MODEL // SG_HEADS
CONV_WIDTH = 31
CONV_HALO = 16
N_EXPERTS = 8
TOP_K = 2
LANES = 128
SUBLANES = 8
S5_LB_GROUPS = LANES // S5_GROUP
MOD_CHUNK = 128
FFN_CHUNK = 512
MOE_SUB = 256
MIX_ROW_TILES = (1408, 768, 256)
VMEM_LIMIT = 56 * 1024 * 1024


def _pick(n, candidates):
    for c in candidates:
        if n % c == 0:
            return c
    raise ValueError(f"no tile in {candidates} divides {n}")


def _params(sem):
    return pltpu.CompilerParams(dimension_semantics=sem, vmem_limit_bytes=VMEM_LIMIT)


def _sigmoid(x):
    return 1.0 / (1.0 + jnp.exp(-x))


def _silu(x):
    return x * _sigmoid(x)


def _gelu_tanh(x):
    return 0.5 * x * (1.0 + jnp.tanh(math.sqrt(2.0 / math.pi) * (x + 0.044715 * (x * x * x))))


def _is_ctx(i, tm):
    rows = i * tm + lax.broadcasted_iota(jnp.int32, (tm, 1), 0)
    return rows < CTX_LEN


def _mod_row(mod_ref, k, is_ctx):
    return jnp.where(is_ctx, mod_ref[1, k:k + 1, :], mod_ref[0, k:k + 1, :])


def _norm_mod_rows(row0, h, g, mod_ref, k_shift):
    inv = lax.rsqrt(jnp.mean(h * h, axis=-1, keepdims=True) + EPS)
    hn = (h * inv) * g
    is_ctx = (row0 + lax.broadcasted_iota(jnp.int32, (h.shape[0], 1), 0)) < CTX_LEN
    return hn * (1.0 + _mod_row(mod_ref, k_shift + 1, is_ctx)) + _mod_row(mod_ref, k_shift, is_ctx)


def _norm_mod(i, tm, h_ref, g_ref, mod_ref, k_shift):
    return _norm_mod_rows(i * tm, h_ref[...], g_ref[...], mod_ref, k_shift)


def _norm_mod_into(a_ref, i, tm, h_ref, g_ref, mod_ref, k_shift):
    assert tm % MOD_CHUNK == 0 and CTX_LEN % MOD_CHUNK == 0
    g = g_ref[...]

    def chunk(r, carry):
        rows = pl.ds(pl.multiple_of(r * MOD_CHUNK, MOD_CHUNK), MOD_CHUNK)
        m = jnp.where(i * tm + r * MOD_CHUNK < CTX_LEN, 1, 0)
        h = h_ref[rows, :]
        inv = lax.rsqrt(jnp.mean(h * h, axis=-1, keepdims=True) + EPS)
        scale = mod_ref[m, pl.ds(k_shift + 1, 1), :]
        shift = mod_ref[m, pl.ds(k_shift, 1), :]
        a_ref[rows, :] = (((h * inv) * g) * (1.0 + scale) + shift).astype(a_ref.dtype)
        return carry

    lax.fori_loop(0, tm // MOD_CHUNK, chunk, 0)


def _layer_norm(v, g, b):
    mu = jnp.mean(v, axis=-1, keepdims=True)
    vc = v - mu
    var = jnp.mean(vc * vc, axis=-1, keepdims=True)
    return vc * lax.rsqrt(var + EPS) * g + b


def _fused_matmul(*, name, n_rows, k_dim, n_out, tm, tn, pro_args, pro_specs, pro_fn,
                  w, w_layer, w_col_offsets, epi_args, epi_specs, epi_fn, out_dtype):
    n_pro, n_w, n_epi = len(pro_args), len(w_col_offsets), len(epi_args)
    nj = n_out // tn

    def body(*refs):
        pro_refs = refs[:n_pro]
        w_refs = refs[n_pro:n_pro + n_w]
        epi_refs = refs[n_pro + n_w:n_pro + n_w + n_epi]
        out_ref, a_ref = refs[-2], refs[-1]
        i = pl.program_id(0)
        j = pl.program_id(1)

        @pl.when(j == 0)
        def _():
            pro_fn(i, a_ref, *pro_refs)

        a = a_ref[...]
        ys = [jnp.dot(a, wr[...].astype(BF16), preferred_element_type=F32) for wr in w_refs]
        out_ref[...] = epi_fn(i, j, ys, *epi_refs).astype(out_dtype)

    w_specs = [
        pl.BlockSpec((None, k_dim, tn), functools.partial(lambda i, j, o: (w_layer, 0, j + o), o=off // tn))
        for off in w_col_offsets
    ]
    return pl.pallas_call(
        body,
        grid=(n_rows // tm, nj),
        in_specs=list(pro_specs) + w_specs + list(epi_specs),
        out_specs=pl.BlockSpec((tm, tn), lambda i, j: (i, j)),
        out_shape=jax.ShapeDtypeStruct((n_rows, n_out), out_dtype),
        scratch_shapes=[pltpu.VMEM((tm, k_dim), BF16)],
        compiler_params=_params(("arbitrary", "arbitrary")),
        name=name,
    )(*pro_args, *([w] * n_w), *epi_args)


def _norm_specs(layer, tm):
    return [
        pl.BlockSpec((tm, D_MODEL), lambda i, j: (i, 0), pipeline_mode=pl.Buffered(1)),
        pl.BlockSpec((None, 1, D_MODEL), lambda i, j: (layer, 0, 0)),
        pl.BlockSpec((None, 2, N_MOD, D_MODEL), lambda i, j: (layer, 0, 0, 0)),
    ]


def _res_specs(layer, tm, tn):
    return [
        pl.BlockSpec((tm, tn), lambda i, j: (i, j)),
        pl.BlockSpec((None, 2, N_MOD, tn), lambda i, j: (layer, 0, 0, j)),
    ]


def _gated_residual(i, tm, k_gate, y, h_ref, mod_ref):
    return h_ref[...] + _mod_row(mod_ref, k_gate, _is_ctx(i, tm)) * y


def _embed(x2, ctx2, row_tab, col_tab):
    seq = x2.shape[0]
    tm = 4 * GRID_W
    half = D_MODEL // 2
    n_tiles = (CTX_LEN + seq) // tm
    rows_per_tile = tm // GRID_W
    row_tab3 = row_tab.reshape(seq // tm, rows_per_tile, half)

    def body(ctx_ref, x_ref, row_ref, col_ref, out_ref):
        i = pl.program_id(0)

        @pl.when(i == 0)
        def _():
            out_ref[...] = ctx_ref[...]

        @pl.when(i > 0)
        def _():
            rt = row_ref[0]
            row_part = jnp.concatenate(
                [jnp.broadcast_to(rt[r:r + 1, :], (GRID_W, half)) for r in range(rows_per_tile)], axis=0)
            col_part = jnp.concatenate([col_ref[...]] * rows_per_tile, axis=0)
            out_ref[:, :half] = x_ref[:, :half] + row_part
            out_ref[:, half:] = x_ref[:, half:] + col_part

    assert CTX_LEN == tm
    return pl.pallas_call(
        body,
        grid=(n_tiles,),
        in_specs=[
            pl.BlockSpec((tm, D_MODEL), lambda i: (0, 0)),
            pl.BlockSpec((tm, D_MODEL), lambda i: (jnp.maximum(i - 1, 0), 0)),
            pl.BlockSpec((1, rows_per_tile, half), lambda i: (jnp.maximum(i - 1, 0), 0, 0)),
            pl.BlockSpec((GRID_W, half), lambda i: (0, 0)),
        ],
        out_specs=pl.BlockSpec((tm, D_MODEL), lambda i: (i, 0)),
        out_shape=jax.ShapeDtypeStruct((CTX_LEN + seq, D_MODEL), F32),
        compiler_params=_params(("arbitrary",)),
        name="embed",
    )(ctx2, x2, row_tab3, col_tab)


def _modulation(cond8, ada_w, ada_b):
    n = N_MOD * D_MODEL
    tn = 1024

    def body(c_ref, w_ref, b_ref, out_ref):
        a = _silu(c_ref[...]).astype(BF16)
        out_ref[...] = jnp.dot(a, w_ref[...].astype(BF16), preferred_element_type=F32) + b_ref[...]

    return pl.pallas_call(
        body,
        grid=(DEPTH, n // tn),
        in_specs=[
            pl.BlockSpec((8, D_MODEL), lambda l, j: (0, 0)),
            pl.BlockSpec((None, D_MODEL, tn), lambda l, j: (l, 0, j)),
            pl.BlockSpec((None, 1, tn), lambda l, j: (l, 0, j)),
        ],
        out_specs=pl.BlockSpec((None, 8, tn), lambda l, j: (l, 0, j)),
        out_shape=jax.ShapeDtypeStruct((DEPTH, 8, n), F32),
        compiler_params=_params(("arbitrary", "arbitrary")),
        name="modulation",
    )(cond8, ada_w, ada_b.reshape(DEPTH, 1, n))


def _s5_operators(lam_re, lam_im, log_dt, bt_re, bt_im, c_re, c_im):
    G, T, H, P, CW = S5_GROUPS, S5_CHUNK, S5_GROUP, S5_STATE, S5_CW

    def body(lre_ref, lim_ref, ldt_ref, btr_ref, bti_ref, cr_ref, ci_ref,
             kd_ref, we_ref, wo_ref, at_ref):
        slot = pl.program_id(0) % S5_LB_GROUPS
        row_blk = lax.broadcasted_iota(jnp.int32, (LANES, 2 * P), 0) // H
        kn, we_parts, wo_parts, at_parts = [], [], [], []
        for d in range(2):
            lr = jnp.minimum(lre_ref[d, 0], -1e-4)
            li = lim_ref[d, 0]
            dt = jnp.exp(ldt_ref[d, 0])
            mag = jnp.exp(lr * dt)
            ar = mag * jnp.cos(li * dt)
            ai = mag * jnp.sin(li * dt)
            xr = ar - 1.0
            xi = ai
            den = lr * lr + li * li
            kr = (xr * lr + xi * li) / den
            ki = (xi * lr - xr * li) / den
            btr, bti = btr_ref[d, 0], bti_ref[d, 0]
            bbr = kr * btr - ki * bti
            bbi = kr * bti + ki * btr
            cr, ci = cr_ref[d, 0], ci_ref[d, 0]
            pr = [jnp.ones((1, P), F32)]
            pi = [jnp.zeros((1, P), F32)]
            for _ in range(T):
                pr.append(pr[-1] * ar - pi[-1] * ai)
                pi.append(pr[-2] * ai + pi[-1] * ar)
            ba_re = [bbr * pr[l] - bbi * pi[l] for l in range(T)]
            ba_im = [bbr * pi[l] + bbi * pr[l] for l in range(T)]
            rising = list(range(T))
            falling = rising[::-1]
            e_order = falling if d == 0 else rising
            e_re = jnp.concatenate([ba_re[l] for l in e_order], axis=0)
            e_im = jnp.concatenate([ba_im[l] for l in e_order], axis=0)
            k_order = rising if d == 0 else falling
            k_lhs = jnp.concatenate([jnp.concatenate([ba_re[l] for l in k_order], axis=0),
                                     jnp.concatenate([ba_im[l] for l in k_order], axis=0)], axis=1)
            cmat = jnp.concatenate([cr, -ci], axis=1)
            cmat_slot = jnp.where(row_blk == slot, jnp.concatenate([cmat] * S5_LB_GROUPS, axis=0), 0.0)
            kn.append(lax.dot_general(k_lhs, cmat_slot, (((1,), (1,)), ((), ())), precision=HIGHEST,
                                      preferred_element_type=F32))
            lags = [t + 1 for t in range(T)] if d == 0 else [T - t for t in range(T)]
            ca_re = jnp.concatenate([cr * pr[l] - ci * pi[l] for l in lags], axis=0)
            ca_im = jnp.concatenate([cr * pi[l] + ci * pr[l] for l in lags], axis=0)
            we_parts.append((e_re, e_im))
            wo_parts.append((ca_re, -ca_im))
            at_parts.append((pr[T], pi[T]))
        kf, kb = kn
        kd_ref[0] = jnp.concatenate(
            [kb[:CW - H], kb[CW - H:] + kf[:H], kf[H:], jnp.zeros((H, LANES), F32)], axis=0).astype(BF16)
        we_ref[0] = jnp.concatenate(
            [we_parts[0][0], we_parts[1][0], we_parts[0][1], we_parts[1][1]], axis=1).astype(BF16)
        wo_ref[0] = jnp.concatenate(
            [wo_parts[0][0], wo_parts[1][0], wo_parts[0][1], wo_parts[1][1]], axis=1).astype(BF16)
        at_ref[0] = jnp.concatenate(
            [at_parts[0][0], at_parts[1][0], at_parts[0][1], at_parts[1][1]], axis=1)

    vec_spec = pl.BlockSpec((2, 1, 1, P), lambda g: (0, g, 0, 0))
    mat_spec = pl.BlockSpec((2, 1, H, P), lambda g: (0, g, 0, 0))
    op_spec = pl.BlockSpec((1, CW, CW), lambda g: (g, 0, 0))
    op_shape = jax.ShapeDtypeStruct((G, CW, CW), BF16)
    return pl.pallas_call(
        body,
        grid=(G,),
        in_specs=[vec_spec, vec_spec, pl.BlockSpec((2, 1, 1, 1), lambda g: (0, g, 0, 0)),
                  mat_spec, mat_spec, mat_spec, mat_spec],
        out_specs=[pl.BlockSpec((1, 2 * CW, LANES), lambda g: (g, 0, 0)), op_spec, op_spec,
                   pl.BlockSpec((1, 1, CW), lambda g: (g, 0, 0))],
        out_shape=[jax.ShapeDtypeStruct((G, 2 * CW, LANES), BF16), op_shape, op_shape,
                   jax.ShapeDtypeStruct((G, 1, CW), F32)],
        compiler_params=_params(("arbitrary",)),
        name="s5_operators",
    )(lam_re.reshape(2, G, 1, P), lam_im.reshape(2, G, 1, P), log_dt.reshape(2, G, 1, 1),
      bt_re, bt_im, c_re, c_im)


def _s5_gather_chunks(u2_ref, x_ref, sem, lb):
    copies = [pltpu.make_async_copy(u2_ref.at[:, pl.ds(s * D_MODEL + lb * LANES, LANES)],
                                    x_ref.at[:, pl.ds(s * LANES, LANES)], sem)
              for s in range(S5_CHUNK)]
    for cp in copies:
        cp.start()
    return copies


def _s5_block_diag(dst_ref, src_ref):
    dst_ref[...] = jnp.zeros_like(dst_ref)
    for g in range(S5_LB_GROUPS):
        for s in range(S5_CHUNK):
            dst_ref[s * LANES + g * S5_GROUP:s * LANES + (g + 1) * S5_GROUP, g * S5_CW:(g + 1) * S5_CW] = (
                src_ref[g, s * S5_GROUP:(s + 1) * S5_GROUP, :])


def _s5_local_state(u2, we):
    nc = u2.shape[0]
    k = S5_CHUNK * LANES
    n = S5_LB_GROUPS * S5_CW

    def body(u2_ref, we_ref, out_ref, x_ref, w_ref, sem):
        lb = pl.program_id(0)
        gathers = _s5_gather_chunks(u2_ref, x_ref, sem, lb)
        _s5_block_diag(w_ref, we_ref)
        for cp in gathers:
            cp.wait()
        states = jnp.dot(x_ref[...], w_ref[...], preferred_element_type=F32)
        for g in range(S5_LB_GROUPS):
            out_ref[:, g, :] = states[:, g * S5_CW:(g + 1) * S5_CW]

    return pl.pallas_call(
        body,
        grid=(D_MODEL // LANES,),
        in_specs=[pl.BlockSpec(memory_space=pl.ANY),
                  pl.BlockSpec((S5_LB_GROUPS, S5_CW, S5_CW), lambda lb: (lb, 0, 0))],
        out_specs=pl.BlockSpec((nc, S5_LB_GROUPS, S5_CW), lambda lb: (0, lb, 0)),
        out_shape=jax.ShapeDtypeStruct((nc, S5_GROUPS, S5_CW), F32),
        scratch_shapes=[pltpu.VMEM((nc, k), BF16), pltpu.VMEM((k, n), BF16), pltpu.SemaphoreType.DMA],
        compiler_params=_params(("arbitrary",)),
        name="s5_local_state",
    )(u2, we)


def _s5_carry(sloc_t, at):
    nc, G, cw = sloc_t.shape
    cb = CTX_LEN // S5_CHUNK
    nb = nc // cb
    half = cw // 2

    def rev_block(s):
        return jnp.where(s == 0, 0, nb - s)

    def body(sf_ref, sb_ref, at_ref, pf_ref, pb_ref, st_ref):
        s = pl.program_id(0)

        @pl.when(s == 0)
        def _():
            st_ref[...] = jnp.zeros_like(st_ref)

        is_fwd = (lax.broadcasted_iota(jnp.int32, (G, half), 1) < S5_STATE)
        a_re, a_im = at_ref[:, :half], at_ref[:, half:]
        s_re, s_im = st_ref[:, :half], st_ref[:, half:]
        for k in range(cb):
            kb = cb - 1 - k
            pf_ref[k, :, :half] = s_re
            pf_ref[k, :, half:] = s_im
            pb_ref[kb, :, :half] = s_re
            pb_ref[kb, :, half:] = s_im
            l_re = jnp.where(is_fwd, sf_ref[k, :, :half], sb_ref[kb, :, :half])
            l_im = jnp.where(is_fwd, sf_ref[k, :, half:], sb_ref[kb, :, half:])
            n_re = a_re * s_re - a_im * s_im + l_re
            n_im = a_re * s_im + a_im * s_re + l_im
            s_re, s_im = n_re, n_im
        st_ref[:, :half] = s_re
        st_ref[:, half:] = s_im

    blk = (cb, G, cw)
    return pl.pallas_call(
        body,
        grid=(nb,),
        in_specs=[pl.BlockSpec(blk, lambda s: (s, 0, 0)),
                  pl.BlockSpec(blk, lambda s: (rev_block(s), 0, 0)),
                  pl.BlockSpec((G, cw), lambda s: (0, 0))],
        out_specs=[pl.BlockSpec(blk, lambda s: (s, 0, 0)),
                   pl.BlockSpec(blk, lambda s: (rev_block(s), 0, 0))],
        out_shape=[jax.ShapeDtypeStruct((nc, G, cw), F32)] * 2,
        scratch_shapes=[pltpu.VMEM((G, cw), F32)],
        compiler_params=_params(("arbitrary",)),
        name="s5_carry",
    )(sloc_t, sloc_t, at)


def _s5_output(u2, kd, wo, pf, pb, d3, j):
    nc = u2.shape[0]
    T = S5_CHUNK
    k = T * LANES
    n = S5_LB_GROUPS * S5_CW
    nt = (((1,), (1,)), ((), ()))

    def body(u2_ref, kd_ref, wo_ref, pf_ref, pb_ref, d_ref, y2_ref,
             x_ref, toep_ref, w_ref, bd_ref, y_ref, sem_in, sem_out):
        lb = pl.program_id(0)
        gathers = _s5_gather_chunks(u2_ref, x_ref, sem_in, lb)
        for lag in range(2 * T - 1):
            for g in range(S5_LB_GROUPS):
                bd_ref[lag * LANES + g * S5_GROUP:lag * LANES + (g + 1) * S5_GROUP, :] = (
                    kd_ref[g, lag * S5_GROUP:(lag + 1) * S5_GROUP, :])
        for s in range(T):
            for t in range(T):
                lag = t - s + T - 1
                toep_ref[s * LANES:(s + 1) * LANES, t * LANES:(t + 1) * LANES] = (
                    bd_ref[lag * LANES:(lag + 1) * LANES, :])
        _s5_block_diag(w_ref, wo_ref)
        is_fwd = (lax.broadcasted_iota(jnp.int32, (nc, n), 1) % LANES) < S5_STATE
        p = jnp.where(is_fwd, pf_ref[...], pb_ref[...]).astype(BF16)
        for cp in gathers:
            cp.wait()
        x = x_ref[...]
        y = jnp.dot(x, toep_ref[...], preferred_element_type=F32)
        y = y + lax.dot_general(p, w_ref[...], nt, preferred_element_type=F32)
        y = y + jnp.concatenate([d_ref[...]] * T, axis=1) * x.astype(F32)
        def scatters(lane_block):
            return [pltpu.make_async_copy(y_ref.at[:, pl.ds(t * LANES, LANES)],
                                          y2_ref.at[:, pl.ds(t * D_MODEL + lane_block * LANES, LANES)], sem_out)
                    for t in range(T)]

        @pl.when(lb > 0)
        def _():
            for cp in scatters(lb - 1):
                cp.wait()

        y_ref[...] = y.astype(BF16)
        for cp in scatters(lb):
            cp.start()

        @pl.when(lb == pl.num_programs(0) - 1)
        def _():
            for cp in scatters(lb):
                cp.wait()

    state = pl.BlockSpec((nc, n), lambda lb: (0, lb))
    return pl.pallas_call(
        body,
        grid=(D_MODEL // LANES,),
        in_specs=[pl.BlockSpec(memory_space=pl.ANY),
                  pl.BlockSpec((S5_LB_GROUPS, 2 * S5_CW, LANES), lambda lb: (lb, 0, 0)),
                  pl.BlockSpec((S5_LB_GROUPS, S5_CW, S5_CW), lambda lb: (lb, 0, 0)),
                  state, state,
                  pl.BlockSpec((None, 1, LANES), lambda lb: (j, 0, lb))],
        out_specs=pl.BlockSpec(memory_space=pl.ANY),
        out_shape=jax.ShapeDtypeStruct((nc, T * D_MODEL), BF16),
        scratch_shapes=[pltpu.VMEM((nc, k), BF16), pltpu.VMEM((k, k), BF16), pltpu.VMEM((k, n), BF16),
                        pltpu.VMEM((2 * T * LANES, LANES), BF16), pltpu.VMEM((nc, k), BF16),
                        pltpu.SemaphoreType.DMA, pltpu.SemaphoreType.DMA],
        compiler_params=_params(("arbitrary",)),
        name="s5_output",
    )(u2, kd, wo, pf, pb, d3)


def _s5_layer(h, layer, j, mod, norm_g, p):
    n_rows = h.shape[0]
    G, T, H = S5_GROUPS, S5_CHUNK, S5_GROUP
    nc = n_rows // T
    tm = _pick(n_rows, MIX_ROW_TILES)
    tn = 512
    u = _fused_matmul(
        name="s5_in", n_rows=n_rows, k_dim=D_MODEL, n_out=D_MODEL, tm=tm, tn=tn,
        pro_args=(h, norm_g, mod), pro_specs=_norm_specs(layer, tm),
        pro_fn=lambda i, a_ref, h_ref, g_ref, m_ref: _norm_mod_into(a_ref, i, tm, h_ref, g_ref, m_ref, 0),
        w=p['s5_w_in'], w_layer=j, w_col_offsets=(0,),
        epi_args=(), epi_specs=(), epi_fn=lambda i, jj, ys: ys[0], out_dtype=BF16)
    u2 = u.reshape(nc, T * D_MODEL)
    kd, we, wo, at = _s5_operators(
        p['s5_lam_re'][j], p['s5_lam_im'][j], p['s5_log_dt'][j],
        jnp.swapaxes(p['s5_b_re'][j], -1, -2), jnp.swapaxes(p['s5_b_im'][j], -1, -2),
        p['s5_c_re'][j], p['s5_c_im'][j])
    sloc = _s5_local_state(u2, we)
    pf, pb = _s5_carry(sloc, at.reshape(G, T * H))
    d3 = p['s5_d'].reshape(-1, 1, D_MODEL)
    y2 = _s5_output(u2, kd, wo, pf.reshape(nc, G * T * H), pb.reshape(nc, G * T * H), d3, j)
    y = y2.reshape(n_rows, D_MODEL)

    def gelu_into(i, a_ref, y_ref):
        a_ref[...] = _gelu_tanh(y_ref[...].astype(F32)).astype(BF16)

    return _fused_matmul(
        name="s5_out", n_rows=n_rows, k_dim=D_MODEL, n_out=D_MODEL, tm=tm, tn=tn,
        pro_args=(y,), pro_specs=[pl.BlockSpec((tm, D_MODEL), lambda i, jj: (i, 0), pipeline_mode=pl.Buffered(1))],
        pro_fn=gelu_into,
        w=p['s5_w_out'], w_layer=j, w_col_offsets=(0, D_MODEL),
        epi_args=(h, mod), epi_specs=_res_specs(layer, tm, tn),
        epi_fn=lambda i, jj, ys, h_ref, m_ref: _gated_residual(
            i, tm, 2, ys[0] * _sigmoid(ys[1]), h_ref, m_ref),
        out_dtype=F32)


def _sg_layer(h, layer, j, mod, norm_g, p):
    n_rows = h.shape[0]
    tm = _pick(n_rows, MIX_ROW_TILES)
    tn = 512
    z = _fused_matmul(
        name="sg_in", n_rows=n_rows, k_dim=D_MODEL, n_out=2 * D_MODEL, tm=tm, tn=tn,
        pro_args=(h, norm_g, mod), pro_specs=_norm_specs(layer, tm),
        pro_fn=lambda i, a_ref, h_ref, g_ref, m_ref: _norm_mod_into(a_ref, i, tm, h_ref, g_ref, m_ref, 0),
        w=p['sg_w_in'], w_layer=j, w_col_offsets=(0,),
        epi_args=(), epi_specs=(), epi_fn=lambda i, jj, ys: _gelu_tanh(ys[0]), out_dtype=BF16)

    def gate(i, a_ref, u_ref, v_ref, g_ref, b_ref, ws_ref, bs_ref):
        ln_g, ln_b = g_ref[...], b_ref[...]

        def chunk(c, carry):
            rows = pl.ds(pl.multiple_of(c * SG_CHUNK, SG_CHUNK), SG_CHUNK)
            v = _layer_norm(v_ref[rows, :].astype(F32), ln_g, ln_b).astype(BF16)
            for hd in range(SG_HEADS):
                cols = slice(hd * SG_HEAD_DIM, (hd + 1) * SG_HEAD_DIM)
                sv = jnp.dot(ws_ref[hd].astype(BF16), v[:, cols], preferred_element_type=F32)
                gated = u_ref[rows, cols].astype(F32) * (sv + bs_ref[:, hd:hd + 1])
                a_ref[rows, cols] = gated.astype(BF16)
            return carry

        lax.fori_loop(0, tm // SG_CHUNK, chunk, 0)

    return _fused_matmul(
        name="sg_out", n_rows=n_rows, k_dim=D_MODEL, n_out=D_MODEL, tm=tm, tn=tn,
        pro_args=(z, z, p['sg_ln_g'], p['sg_ln_b'], p['sg_w_s'], jnp.swapaxes(p['sg_b_s'], -1, -2)),
        pro_specs=[
            pl.BlockSpec((tm, D_MODEL), lambda i, jj: (i, 0), pipeline_mode=pl.Buffered(1)),
            pl.BlockSpec((tm, D_MODEL), lambda i, jj: (i, 1), pipeline_mode=pl.Buffered(1)),
            pl.BlockSpec((1, D_MODEL), lambda i, jj: (j, 0)),
            pl.BlockSpec((1, D_MODEL), lambda i, jj: (j, 0)),
            pl.BlockSpec((None, SG_HEADS, SG_CHUNK, SG_CHUNK), lambda i, jj: (j, 0, 0, 0)),
            pl.BlockSpec((None, SG_CHUNK, SG_HEADS), lambda i, jj: (j, 0, 0)),
        ],
        pro_fn=gate,
        w=p['sg_w_out'], w_layer=j, w_col_offsets=(0,),
        epi_args=(h, mod), epi_specs=_res_specs(layer, tm, tn),
        epi_fn=lambda i, jj, ys, h_ref, m_ref: _gated_residual(i, tm, 2, ys[0], h_ref, m_ref),
        out_dtype=F32)


def _conv_norm(z, dw_w, dw_b, ln_g, ln_b, j):
    n_rows = z.shape[0]
    tm = CTX_LEN
    n_tiles = n_rows // tm
    hb = tm // CONV_HALO
    half = CONV_WIDTH // 2
    rb, cbw = 64, LANES

    def body(prev_ref, cur_ref, next_ref, w_ref, b_ref, g_ref, beta_ref, out_ref, zp_ref, acc_ref):
        i = pl.program_id(0)
        prev_ok = i >= 2
        next_ok = jnp.logical_and(i >= 1, i < n_tiles - 1)
        zp_ref[0:CONV_HALO, :] = jnp.where(prev_ok, prev_ref[...], 0.0)
        zp_ref[CONV_HALO:CONV_HALO + tm, :] = cur_ref[...]
        zp_ref[CONV_HALO + tm:, :] = jnp.where(next_ok, next_ref[...], 0.0)

        def col_block(cb, carry):
            c0 = pl.multiple_of(cb * cbw, cbw)
            w = w_ref[:, pl.ds(c0, cbw)]
            for r in range(tm // rb):
                win = zp_ref[r * rb:r * rb + rb + 2 * CONV_HALO, pl.ds(c0, cbw)]
                span = rb + 2 * CONV_HALO - SUBLANES
                phases = [win[ph:ph + span, :] for ph in range(SUBLANES)]
                acc = jnp.zeros((rb, cbw), F32)
                for k in range(CONV_WIDTH):
                    off = CONV_HALO - half + k
                    base = off - off % SUBLANES
                    acc = acc + phases[off % SUBLANES][base:base + rb, :] * w[k:k + 1, :]
                acc_ref[r * rb:(r + 1) * rb, pl.ds(c0, cbw)] = acc
            return carry

        lax.fori_loop(0, D_MODEL // cbw, col_block, 0)
        y = _layer_norm(acc_ref[...] + b_ref[...], g_ref[...], beta_ref[...])
        out_ref[...] = _silu(y).astype(BF16)

    vec = pl.BlockSpec((1, D_MODEL), lambda i: (j, 0))
    return pl.pallas_call(
        body,
        grid=(n_tiles,),
        in_specs=[
            pl.BlockSpec((CONV_HALO, D_MODEL), lambda i: (jnp.maximum(i * hb - 1, 0), 0)),
            pl.BlockSpec((tm, D_MODEL), lambda i: (i, 0)),
            pl.BlockSpec((CONV_HALO, D_MODEL), lambda i: (jnp.minimum((i + 1) * hb, n_tiles * hb - 1), 0)),
            pl.BlockSpec((None, CONV_WIDTH, D_MODEL), lambda i: (j, 0, 0)),
            vec, vec, vec,
        ],
        out_specs=pl.BlockSpec((tm, D_MODEL), lambda i: (i, 0)),
        out_shape=jax.ShapeDtypeStruct((n_rows, D_MODEL), BF16),
        scratch_shapes=[pltpu.VMEM((tm + 2 * CONV_HALO, D_MODEL), F32), pltpu.VMEM((tm, D_MODEL), F32)],
        compiler_params=_params(("arbitrary",)),
        name="conv_norm",
    )(z, z, z, dw_w, dw_b, ln_g, ln_b)


def _conv_layer(h, layer, j, mod, norm_g, p):
    n_rows = h.shape[0]
    tm = _pick(n_rows, MIX_ROW_TILES)
    tn = 512
    z = _fused_matmul(
        name="cv_pw1", n_rows=n_rows, k_dim=D_MODEL, n_out=D_MODEL, tm=tm, tn=tn,
        pro_args=(h, norm_g, mod), pro_specs=_norm_specs(layer, tm),
        pro_fn=lambda i, a_ref, h_ref, g_ref, m_ref: _norm_mod_into(a_ref, i, tm, h_ref, g_ref, m_ref, 0),
        w=p['cv_w_pw1'], w_layer=j, w_col_offsets=(0, D_MODEL),
        epi_args=(), epi_specs=(), epi_fn=lambda i, jj, ys: ys[0] * _sigmoid(ys[1]), out_dtype=F32)
    zc = _conv_norm(z, p['cv_dw_w'], p['cv_dw_b'], p['cv_ln_g'], p['cv_ln_b'], j)

    def copy_into(i, a_ref, z_ref):
        a_ref[...] = z_ref[...]

    return _fused_matmul(
        name="cv_pw2", n_rows=n_rows, k_dim=D_MODEL, n_out=D_MODEL, tm=tm, tn=tn,
        pro_args=(zc,), pro_specs=[pl.BlockSpec((tm, D_MODEL), lambda i, jj: (i, 0), pipeline_mode=pl.Buffered(1))],
        pro_fn=copy_into,
        w=p['cv_w_pw2'], w_layer=j, w_col_offsets=(0,),
        epi_args=(h, mod), epi_specs=_res_specs(layer, tm, tn),
        epi_fn=lambda i, jj, ys, h_ref, m_ref: _gated_residual(i, tm, 2, ys[0], h_ref, m_ref),
        out_dtype=F32)


SLAB_RANK1, SLAB_RANK2, SLAB_EXP1, SLAB_EXP2, SLAB_W1, SLAB_W2 = range(6)


def _router(h, layer, k, mod, norm_g, w_router):
    n_rows = h.shape[0]
    tm = _pick(n_rows, (768, 256))
    wr = jnp.pad(w_router[k], ((0, 0), (0, LANES - N_EXPERTS)))

    def body(h_ref, g_ref, m_ref, wr_ref, hn_ref, slab_ref, cnt_ref, base_ref):
        i = pl.program_id(0)

        @pl.when(i == 0)
        def _():
            base_ref[...] = jnp.zeros_like(base_ref)

        hn = _norm_mod(i, tm, h_ref, g_ref, m_ref, 3)
        hn_ref[...] = hn
        logits = jnp.dot(hn, wr_ref[...], precision=HIGHEST, preferred_element_type=F32)
        lane = lax.broadcasted_iota(jnp.int32, (tm, LANES), 1).astype(F32)
        neg = jnp.float32(-jnp.inf)
        lg = jnp.where(lane < N_EXPERTS, logits, neg)
        m1 = jnp.max(lg, axis=1, keepdims=True)
        i1 = jnp.min(jnp.where(lg == m1, lane, float(LANES)), axis=1, keepdims=True)
        lg2 = jnp.where(lane == i1, neg, lg)
        m2 = jnp.max(lg2, axis=1, keepdims=True)
        i2 = jnp.min(jnp.where(lg2 == m2, lane, float(LANES)), axis=1, keepdims=True)
        e = jnp.exp(m2 - m1)
        w1 = 1.0 / (1.0 + e)
        w2 = e / (1.0 + e)
        onehot = jnp.where(lane == i1, 1.0, 0.0) + jnp.where(lane == i2, 1.0, 0.0)
        tri = jnp.where(lax.broadcasted_iota(jnp.int32, (tm, tm), 1)
                        < lax.broadcasted_iota(jnp.int32, (tm, tm), 0), 1.0, 0.0).astype(BF16)
        base = base_ref[0:1, :]
        excl = jnp.dot(tri, onehot.astype(BF16), preferred_element_type=F32) + base
        r1 = jnp.sum(jnp.where(lane == i1, excl, 0.0), axis=1, keepdims=True)
        r2 = jnp.sum(jnp.where(lane == i2, excl, 0.0), axis=1, keepdims=True)
        total = base + jnp.sum(onehot, axis=0, keepdims=True)
        base_ref[...] = jnp.broadcast_to(total, base_ref.shape)
        cnt_ref[...] = jnp.broadcast_to(total, cnt_ref.shape)
        slab = jnp.zeros((tm, LANES), F32)
        for ln, val in ((SLAB_RANK1, r1), (SLAB_RANK2, r2), (SLAB_EXP1, i1), (SLAB_EXP2, i2),
                        (SLAB_W1, w1), (SLAB_W2, w2)):
            slab = jnp.where(lane == float(ln), val, slab)
        slab_ref[...] = slab

    return pl.pallas_call(
        body,
        grid=(n_rows // tm,),
        in_specs=[
            pl.BlockSpec((tm, D_MODEL), lambda i: (i, 0)),
            pl.BlockSpec((None, 1, D_MODEL), lambda i: (layer, 0, 0)),
            pl.BlockSpec((None, 2, N_MOD, D_MODEL), lambda i: (layer, 0, 0, 0)),
            pl.BlockSpec((D_MODEL, LANES), lambda i: (0, 0)),
        ],
        out_specs=[pl.BlockSpec((tm, D_MODEL), lambda i: (i, 0)),
                   pl.BlockSpec((tm, LANES), lambda i: (i, 0)),
                   pl.BlockSpec((8, LANES), lambda i: (0, 0))],
        out_shape=[jax.ShapeDtypeStruct((n_rows, D_MODEL), F32),
                   jax.ShapeDtypeStruct((n_rows, LANES), F32),
                   jax.ShapeDtypeStruct((8, LANES), F32)],
        scratch_shapes=[pltpu.VMEM((8, LANES), F32)],
        compiler_params=_params(("arbitrary",)),
        name="router",
    )(h, norm_g, mod, wr)


def _routing_tables(slab, counts, tb, nblk):
    cnt = counts[0, :N_EXPERTS].astype(jnp.int32)
    padded = ((cnt + tb - 1) // tb) * tb
    ends = jnp.cumsum(padded)
    starts = ends - padded
    pos1 = starts[slab[:, SLAB_EXP1].astype(jnp.int32)] + slab[:, SLAB_RANK1].astype(jnp.int32)
    pos2 = starts[slab[:, SLAB_EXP2].astype(jnp.int32)] + slab[:, SLAB_RANK2].astype(jnp.int32)
    nvalid = ends[-1] // tb
    blk = jnp.minimum(jnp.arange(nblk, dtype=jnp.int32), nvalid - 1)
    blk_expert = jnp.minimum(jnp.searchsorted(ends, blk * tb, side='right'), N_EXPERTS - 1).astype(jnp.int32)
    left = cnt[blk_expert] - (blk * tb - starts[blk_expert])
    blk_rows = ((jnp.clip(left, 0, tb) + MOE_SUB - 1) // MOE_SUB) * MOE_SUB
    return pos1, pos2, blk_expert, blk_rows.astype(jnp.int32), nvalid.reshape(1).astype(jnp.int32)


def _dispatch(hn, pos1, pos2, n_slots):
    n_rows = hn.shape[0]
    tm = CTX_LEN

    def body(p1_ref, p2_ref, hn_ref, init_ref, xs_ref, sem):
        i = pl.program_id(0)

        def row_copy(t, p):
            return pltpu.make_async_copy(hn_ref.at[pl.ds(t, 1)], xs_ref.at[pl.ds(p, 1)], sem)

        def issue(t, carry):
            row_copy(t, p1_ref[i * tm + t]).start()
            row_copy(t, p2_ref[i * tm + t]).start()
            return carry

        def drain(t, carry):
            row_copy(0, 0).wait()
            row_copy(0, 0).wait()
            return carry

        lax.fori_loop(0, tm, issue, 0, unroll=8)
        lax.fori_loop(0, tm, drain, 0, unroll=8)

    return pl.pallas_call(
        body,
        grid_spec=pltpu.PrefetchScalarGridSpec(
            num_scalar_prefetch=2,
            grid=(n_rows // tm,),
            in_specs=[pl.BlockSpec((tm, D_MODEL), lambda i, p1, p2: (i, 0)),
                      pl.BlockSpec(memory_space=pl.ANY)],
            out_specs=pl.BlockSpec(memory_space=pl.ANY),
            scratch_shapes=[pltpu.SemaphoreType.DMA],
        ),
        out_shape=jax.ShapeDtypeStruct((n_slots, D_MODEL), F32),
        input_output_aliases={3: 0},
        compiler_params=_params(("arbitrary",)),
        name="moe_dispatch",
    )(pos1, pos2, hn, jnp.zeros((n_slots, D_MODEL), F32))


def _swiglu_chunk(a, wg_ref, wu_ref, wd_ref):
    g = jnp.dot(a, wg_ref[...].astype(BF16), preferred_element_type=F32)
    u = jnp.dot(a, wu_ref[...].astype(BF16), preferred_element_type=F32)
    return jnp.dot((_silu(g) * u).astype(BF16), wd_ref[...].astype(BF16), preferred_element_type=F32)


def _moe_ffn(xs, blk_expert, blk_rows, nvalid, w_gate, w_up, w_down, e_base, tb):
    n_slots = xs.shape[0]
    d_ff = w_gate.shape[-1]
    tf = FFN_CHUNK
    nf = d_ff // tf

    def body(be_ref, br_ref, nv_ref, x_ref, wg_ref, wu_ref, wd_ref, out_ref, a_ref):
        b = pl.program_id(0)
        f = pl.program_id(1)
        used = b < nv_ref[0]

        @pl.when(jnp.logical_and(used, f == 0))
        def _():
            a_ref[...] = x_ref[...].astype(BF16)
            out_ref[...] = jnp.zeros_like(out_ref)

        for n_rows_used in range(MOE_SUB, tb + 1, MOE_SUB):
            @pl.when(jnp.logical_and(used, br_ref[b] == n_rows_used))
            def _():
                rows = slice(0, n_rows_used)
                out_ref[rows, :] += _swiglu_chunk(a_ref[rows, :], wg_ref, wu_ref, wd_ref)

        @pl.when(jnp.logical_and(jnp.logical_not(used), f == 0))
        def _():
            out_ref[...] = jnp.zeros_like(out_ref)

    def f_idx(b, f, nv):
        return jnp.where(b < nv[0], f, nf - 1)

    once = pl.Buffered(1)
    rows = lambda b, f, be, br, nv: (jnp.minimum(b, nv[0] - 1), 0)
    return pl.pallas_call(
        body,
        grid_spec=pltpu.PrefetchScalarGridSpec(
            num_scalar_prefetch=3,
            grid=(n_slots // tb, nf),
            in_specs=[
                pl.BlockSpec((tb, D_MODEL), rows, pipeline_mode=once),
                pl.BlockSpec((None, D_MODEL, tf), lambda b, f, be, br, nv: (e_base + be[b], 0, f_idx(b, f, nv))),
                pl.BlockSpec((None, D_MODEL, tf), lambda b, f, be, br, nv: (e_base + be[b], 0, f_idx(b, f, nv))),
                pl.BlockSpec((None, tf, D_MODEL), lambda b, f, be, br, nv: (e_base + be[b], f_idx(b, f, nv), 0)),
            ],
            out_specs=pl.BlockSpec((tb, D_MODEL), lambda b, f, be, br, nv: (b, 0), pipeline_mode=once),
            scratch_shapes=[pltpu.VMEM((tb, D_MODEL), BF16)],
        ),
        out_shape=jax.ShapeDtypeStruct((n_slots, D_MODEL), F32),
        compiler_params=_params(("arbitrary", "arbitrary")),
        name="moe_ffn",
    )(blk_expert, blk_rows, nvalid, xs, w_gate, w_up, w_down)


def _combine(h, ys, slab, pos1, pos2, layer, mod):
    n_rows = h.shape[0]
    tm = CTX_LEN

    def body(p1_ref, p2_ref, h_ref, slab_ref, m_ref, ys_ref, out_ref, y1_ref, y2_ref, sem):
        i = pl.program_id(0)

        def row_copy(p, dst_ref, t):
            return pltpu.make_async_copy(ys_ref.at[pl.ds(p, 1)], dst_ref.at[pl.ds(t, 1)], sem)

        def issue(t, carry):
            row_copy(p1_ref[i * tm + t], y1_ref, t).start()
            row_copy(p2_ref[i * tm + t], y2_ref, t).start()
            return carry

        def drain(t, carry):
            row_copy(0, y1_ref, 0).wait()
            row_copy(0, y2_ref, 0).wait()
            return carry

        lax.fori_loop(0, tm, issue, 0, unroll=8)
        lax.fori_loop(0, tm, drain, 0, unroll=8)
        w1 = slab_ref[:, SLAB_W1:SLAB_W1 + 1]
        w2 = slab_ref[:, SLAB_W2:SLAB_W2 + 1]
        moe = w1 * y1_ref[...] + w2 * y2_ref[...]
        out_ref[...] = h_ref[...] + _mod_row(m_ref, 5, _is_ctx(i, tm)) * moe

    return pl.pallas_call(
        body,
        grid_spec=pltpu.PrefetchScalarGridSpec(
            num_scalar_prefetch=2,
            grid=(n_rows // tm,),
            in_specs=[
                pl.BlockSpec((tm, D_MODEL), lambda i, p1, p2: (i, 0)),
                pl.BlockSpec((tm, LANES), lambda i, p1, p2: (i, 0)),
                pl.BlockSpec((None, 2, N_MOD, D_MODEL), lambda i, p1, p2: (layer, 0, 0, 0)),
                pl.BlockSpec(memory_space=pl.ANY),
            ],
            out_specs=pl.BlockSpec((tm, D_MODEL), lambda i, p1, p2: (i, 0)),
            scratch_shapes=[pltpu.VMEM((tm, D_MODEL), F32), pltpu.VMEM((tm, D_MODEL), F32),
                            pltpu.SemaphoreType.DMA],
        ),
        out_shape=jax.ShapeDtypeStruct((n_rows, D_MODEL), F32),
        compiler_params=_params(("arbitrary",)),
        name="moe_combine",
    )(pos1, pos2, h, slab, mod, ys)


def _moe_layer(h, layer, k, mod, norm_g, w_router, w_gate, w_up, w_down):
    n_rows = h.shape[0]
    d_ff = w_gate.shape[-1]
    tb = 1024 if n_rows >= 4096 else 256
    nblk = -(-(TOP_K * n_rows + N_EXPERTS * (tb - 1)) // tb)
    hn, slab, counts = _router(h, layer, k, mod, norm_g, w_router)
    pos1, pos2, blk_expert, blk_rows, nvalid = _routing_tables(slab, counts, tb, nblk)
    xs = _dispatch(hn, pos1, pos2, nblk * tb)
    ys = _moe_ffn(xs, blk_expert, blk_rows, nvalid, w_gate.reshape(-1, D_MODEL, d_ff),
                  w_up.reshape(-1, D_MODEL, d_ff), w_down.reshape(-1, d_ff, D_MODEL), k * N_EXPERTS, tb)
    return _combine(h, ys, slab, pos1, pos2, layer, mod)


def _ffn(h, layer, mod, norm_g, w_gate, w_up, w_down, k):
    n_rows = h.shape[0]
    d_ff = w_gate.shape[-1]
    tm = _pick(n_rows, (768, 256))
    tf = FFN_CHUNK
    nf = d_ff // tf

    def body(h_ref, g_ref, m_ref, wg_ref, wu_ref, wd_ref, out_ref, a_ref):
        i = pl.program_id(0)
        f = pl.program_id(1)

        @pl.when(f == 0)
        def _():
            _norm_mod_into(a_ref, i, tm, h_ref, g_ref, m_ref, 3)
            out_ref[...] = jnp.zeros_like(out_ref)

        out_ref[...] += _swiglu_chunk(a_ref[...], wg_ref, wu_ref, wd_ref)

        @pl.when(f == nf - 1)
        def _():
            def chunk(r, carry):
                rows = pl.ds(pl.multiple_of(r * MOD_CHUNK, MOD_CHUNK), MOD_CHUNK)
                m = jnp.where(i * tm + r * MOD_CHUNK < CTX_LEN, 1, 0)
                out_ref[rows, :] = h_ref[rows, :] + m_ref[m, pl.ds(5, 1), :] * out_ref[rows, :]
                return carry

            lax.fori_loop(0, tm // MOD_CHUNK, chunk, 0)

    once = pl.Buffered(1)
    return pl.pallas_call(
        body,
        grid=(n_rows // tm, nf),
        in_specs=[
            pl.BlockSpec((tm, D_MODEL), lambda i, f: (i, 0), pipeline_mode=once),
            pl.BlockSpec((None, 1, D_MODEL), lambda i, f: (layer, 0, 0)),
            pl.BlockSpec((None, 2, N_MOD, D_MODEL), lambda i, f: (layer, 0, 0, 0)),
            pl.BlockSpec((None, D_MODEL, tf), lambda i, f: (k, 0, f)),
            pl.BlockSpec((None, D_MODEL, tf), lambda i, f: (k, 0, f)),
            pl.BlockSpec((None, tf, D_MODEL), lambda i, f: (k, f, 0)),
        ],
        out_specs=pl.BlockSpec((tm, D_MODEL), lambda i, f: (i, 0), pipeline_mode=once),
        out_shape=jax.ShapeDtypeStruct((n_rows, D_MODEL), F32),
        scratch_shapes=[pltpu.VMEM((tm, D_MODEL), BF16)],
        compiler_params=_params(("arbitrary", "arbitrary")),
        name="ffn",
    )(h, norm_g, mod, w_gate, w_up, w_down)


def _final_norm(h, final_g):
    n_rows = h.shape[0]
    tm = CTX_LEN

    def body(h_ref, g_ref, out_ref):
        x = h_ref[...]
        out_ref[...] = x * lax.rsqrt(jnp.mean(x * x, axis=-1, keepdims=True) + EPS) * g_ref[...]

    return pl.pallas_call(
        body,
        grid=((n_rows - CTX_LEN) // tm,),
        in_specs=[pl.BlockSpec((tm, D_MODEL), lambda i: (i + 1, 0)),
                  pl.BlockSpec((1, D_MODEL), lambda i: (0, 0))],
        out_specs=pl.BlockSpec((tm, D_MODEL), lambda i: (i, 0)),
        out_shape=jax.ShapeDtypeStruct((n_rows - CTX_LEN, D_MODEL), F32),
        compiler_params=_params(("arbitrary",)),
        name="final_norm",
    )(h, final_g.reshape(1, D_MODEL))


def _sincos_tables(rows):
    quarter = D_MODEL // 4
    omega = 1.0 / (POS_BASE ** (jnp.arange(quarter, dtype=F32) / quarter))

    def emb(pos):
        ang = pos[:, None] * omega[None, :]
        return jnp.concatenate([jnp.sin(ang), jnp.cos(ang)], axis=-1)

    return emb(jnp.arange(rows, dtype=F32)), emb(jnp.arange(GRID_W, dtype=F32))


def kernel(x, c, ctx, c_ctx, ada_w, ada_b, norm_mix_g, norm_ffn_g, final_g, s5_w_in, s5_lam_re, s5_lam_im, s5_log_dt, s5_b_re, s5_b_im, s5_c_re, s5_c_im, s5_d, s5_w_out, sg_w_in, sg_ln_g, sg_ln_b, sg_w_s, sg_b_s, sg_w_out, cv_w_pw1, cv_dw_w, cv_dw_b, cv_ln_g, cv_ln_b, cv_w_pw2, ff_w_gate, ff_w_up, ff_w_down, moe_w_router, moe_w_gate, moe_w_up, moe_w_down):
    bsz, seq, dim = x.shape
    assert bsz == 1 and dim == D_MODEL and ctx.shape == (1, CTX_LEN, D_MODEL)
    p = dict(s5_w_in=s5_w_in, s5_lam_re=s5_lam_re, s5_lam_im=s5_lam_im, s5_log_dt=s5_log_dt,
             s5_b_re=s5_b_re, s5_b_im=s5_b_im, s5_c_re=s5_c_re, s5_c_im=s5_c_im, s5_d=s5_d,
             s5_w_out=s5_w_out, sg_w_in=sg_w_in, sg_ln_g=sg_ln_g, sg_ln_b=sg_ln_b, sg_w_s=sg_w_s,
             sg_b_s=sg_b_s, sg_w_out=sg_w_out, cv_w_pw1=cv_w_pw1, cv_dw_w=cv_dw_w, cv_dw_b=cv_dw_b,
             cv_ln_g=cv_ln_g, cv_ln_b=cv_ln_b, cv_w_pw2=cv_w_pw2)
    row_tab, col_tab = _sincos_tables(seq // GRID_W)
    h = _embed(x[0], ctx[0], row_tab, col_tab)
    cond8 = jnp.concatenate([c, c_ctx[None, :], jnp.zeros((6, D_MODEL), F32)], axis=0)
    mod = _modulation(cond8, ada_w, ada_b)[:, :2, :].reshape(DEPTH, 2, N_MOD, D_MODEL)
    g_mix = norm_mix_g.reshape(DEPTH, 1, D_MODEL)
    g_ffn = norm_ffn_g.reshape(DEPTH, 1, D_MODEL)
    for layer in range(DEPTH):
        kind, j = layer % N_MIXERS, layer // N_MIXERS
        if kind == 0:
            h = _s5_layer(h, layer, j, mod, g_mix, p)
        elif kind == 1:
            h = _sg_layer(h, layer, j, mod, g_mix, p)
        else:
            h = _conv_layer(h, layer, j, mod, g_mix, p)
        k = layer // 2
        if layer % 2 == 0:
            h = _ffn(h, layer, mod, g_ffn, ff_w_gate, ff_w_up, ff_w_down, k)
        else:
            h = _moe_layer(h, layer, k, mod, g_ffn, moe_w_router, moe_w_gate, moe_w_up, moe_w_down)
    return _final_norm(h, final_g)[None]
```

```python
import functools
import math

import jax
import jax.numpy as jnp
from jax import lax
from jax.experimental import pallas as pl
from jax.experimental.pallas import tpu as pltpu

F32 = jnp.float32
BF16 = jnp.bfloat16
HIGHEST = lax.Precision.HIGHEST

D_MODEL = 2048
DEPTH = 4
GRID_W = 64
CTX_LEN = 256
POS_BASE = 10000.0
N_MIXERS = 3
N_MOD = 6
EPS = 1e-6
S5_GROUP = 16
S5_GROUPS = D_MODEL // S5_GROUP
S5_STATE = 64
S5_CHUNK = 16
S5_CW = S5_CHUNK * S5_GROUP
SG_CHUNK = 128
SG_HEADS = 8
SG_HEAD_DIM = D_MODEL // SG_HEADS
CONV_WIDTH = 31
CONV_HALO = 16
N_EXPERTS = 8
TOP_K = 2
LANES = 128
SUBLANES = 8
S5_LB_GROUPS = LANES // S5_GROUP
MOD_CHUNK = 128
FFN_CHUNK = 512
MOE_SUB = 256
MIX_ROW_TILES = (1408, 768, 256)
VMEM_LIMIT = 56 * 1024 * 1024


def _pick(n, candidates):
    for c in candidates:
        if n % c == 0:
            return c
    raise ValueError(f"no tile in {candidates} divides {n}")


def _params(sem):
    return pltpu.CompilerParams(dimension_semantics=sem, vmem_limit_bytes=VMEM_LIMIT)


def _sigmoid(x):
    return 1.0 / (1.0 + jnp.exp(-x))


def _silu(x):
    return x * _sigmoid(x)


def _gelu_tanh(x):
    return 0.5 * x * (1.0 + jnp.tanh(math.sqrt(2.0 / math.pi) * (x + 0.044715 * (x * x * x))))


def _is_ctx(i, tm):
    rows = i * tm + lax.broadcasted_iota(jnp.int32, (tm, 1), 0)
    return rows < CTX_LEN


def _mod_row(mod_ref, k, is_ctx):
    return jnp.where(is_ctx, mod_ref[1, k:k + 1, :], mod_ref[0, k:k + 1, :])


def _norm_mod_rows(row0, h, g, mod_ref, k_shift):
    inv = lax.rsqrt(jnp.mean(h * h, axis=-1, keepdims=True) + EPS)
    hn = (h * inv) * g
    is_ctx = (row0 + lax.broadcasted_iota(jnp.int32, (h.shape[0], 1), 0)) < CTX_LEN
    return hn * (1.0 + _mod_row(mod_ref, k_shift + 1, is_ctx)) + _mod_row(mod_ref, k_shift, is_ctx)


def _norm_mod(i, tm, h_ref, g_ref, mod_ref, k_shift):
    return _norm_mod_rows(i * tm, h_ref[...], g_ref[...], mod_ref, k_shift)


def _norm_mod_into(a_ref, i, tm, h_ref, g_ref, mod_ref, k_shift):
    assert tm % MOD_CHUNK == 0 and CTX_LEN % MOD_CHUNK == 0
    g = g_ref[...]

    def chunk(r, carry):
        rows = pl.ds(pl.multiple_of(r * MOD_CHUNK, MOD_CHUNK), MOD_CHUNK)
        m = jnp.where(i * tm + r * MOD_CHUNK < CTX_LEN, 1, 0)
        h = h_ref[rows, :]
        inv = lax.rsqrt(jnp.mean(h * h, axis=-1, keepdims=True) + EPS)
        scale = mod_ref[m, pl.ds(k_shift + 1, 1), :]
        shift = mod_ref[m, pl.ds(k_shift, 1), :]
        a_ref[rows, :] = (((h * inv) * g) * (1.0 + scale) + shift).astype(a_ref.dtype)
        return carry

    lax.fori_loop(0, tm // MOD_CHUNK, chunk, 0)


def _layer_norm(v, g, b):
    mu = jnp.mean(v, axis=-1, keepdims=True)
    vc = v - mu
    var = jnp.mean(vc * vc, axis=-1, keepdims=True)
    return vc * lax.rsqrt(var + EPS) * g + b


def _fused_matmul(*, name, n_rows, k_dim, n_out, tm, tn, pro_args, pro_specs, pro_fn,
                  w, w_layer, w_col_offsets, epi_args, epi_specs, epi_fn, out_dtype):
    n_pro, n_w, n_epi = len(pro_args), len(w_col_offsets), len(epi_args)
    nj = n_out // tn

    def body(*refs):
        pro_refs = refs[:n_pro]
        w_refs = refs[n_pro:n_pro + n_w]
        epi_refs = refs[n_pro + n_w:n_pro + n_w + n_epi]
        out_ref, a_ref = refs[-2], refs[-1]
        i = pl.program_id(0)
        j = pl.program_id(1)

        @pl.when(j == 0)
        def _():
            pro_fn(i, a_ref, *pro_refs)

        a = a_ref[...]
        ys = [jnp.dot(a, wr[...].astype(BF16), preferred_element_type=F32) for wr in w_refs]
        out_ref[...] = epi_fn(i, j, ys, *epi_refs).astype(out_dtype)

    w_specs = [
        pl.BlockSpec((None, k_dim, tn), functools.partial(lambda i, j, o: (w_layer, 0, j + o), o=off // tn))
        for off in w_col_offsets
    ]
    return pl.pallas_call(
        body,
        grid=(n_rows // tm, nj),
        in_specs=list(pro_specs) + w_specs + list(epi_specs),
        out_specs=pl.BlockSpec((tm, tn), lambda i, j: (i, j)),
        out_shape=jax.ShapeDtypeStruct((n_rows, n_out), out_dtype),
        scratch_shapes=[pltpu.VMEM((tm, k_dim), BF16)],
        compiler_params=_params(("arbitrary", "arbitrary")),
        name=name,
    )(*pro_args, *([w] * n_w), *epi_args)


def _norm_specs(layer, tm):
    return [
        pl.BlockSpec((tm, D_MODEL), lambda i, j: (i, 0), pipeline_mode=pl.Buffered(1)),
        pl.BlockSpec((None, 1, D_MODEL), lambda i, j: (layer, 0, 0)),
        pl.BlockSpec((None, 2, N_MOD, D_MODEL), lambda i, j: (layer, 0, 0, 0)),
    ]


def _res_specs(layer, tm, tn):
    return [
        pl.BlockSpec((tm, tn), lambda i, j: (i, j)),
        pl.BlockSpec((None, 2, N_MOD, tn), lambda i, j: (layer, 0, 0, j)),
    ]


def _gated_residual(i, tm, k_gate, y, h_ref, mod_ref):
    return h_ref[...] + _mod_row(mod_ref, k_gate, _is_ctx(i, tm)) * y


def _embed(x2, ctx2, row_tab, col_tab):
    seq = x2.shape[0]
    tm = 4 * GRID_W
    half = D_MODEL // 2
    n_tiles = (CTX_LEN + seq) // tm
    rows_per_tile = tm // GRID_W
    row_tab3 = row_tab.reshape(seq // tm, rows_per_tile, half)

    def body(ctx_ref, x_ref, row_ref, col_ref, out_ref):
        i = pl.program_id(0)

        @pl.when(i == 0)
        def _():
            out_ref[...] = ctx_ref[...]

        @pl.when(i > 0)
        def _():
            rt = row_ref[0]
            row_part = jnp.concatenate(
                [jnp.broadcast_to(rt[r:r + 1, :], (GRID_W, half)) for r in range(rows_per_tile)], axis=0)
            col_part = jnp.concatenate([col_ref[...]] * rows_per_tile, axis=0)
            out_ref[:, :half] = x_ref[:, :half] + row_part
            out_ref[:, half:] = x_ref[:, half:] + col_part

    assert CTX_LEN == tm
    return pl.pallas_call(
        body,
        grid=(n_tiles,),
        in_specs=[
            pl.BlockSpec((tm, D_MODEL), lambda i: (0, 0)),
            pl.BlockSpec((tm, D_MODEL), lambda i: (jnp.maximum(i - 1, 0), 0)),
            pl.BlockSpec((1, rows_per_tile, half), lambda i: (jnp.maximum(i - 1, 0), 0, 0)),
            pl.BlockSpec((GRID_W, half), lambda i: (0, 0)),
        ],
        out_specs=pl.BlockSpec((tm, D_MODEL), lambda i: (i, 0)),
        out_shape=jax.ShapeDtypeStruct((CTX_LEN + seq, D_MODEL), F32),
        compiler_params=_params(("arbitrary",)),
        name="embed",
    )(ctx2, x2, row_tab3, col_tab)


def _modulation(cond8, ada_w, ada_b):
    n = N_MOD * D_MODEL
    tn = 1024

    def body(c_ref, w_ref, b_ref, out_ref):
        a = _silu(c_ref[...]).astype(BF16)
        out_ref[...] = jnp.dot(a, w_ref[...].astype(BF16), preferred_element_type=F32) + b_ref[...]

    return pl.pallas_call(
        body,
        grid=(DEPTH, n // tn),
        in_specs=[
            pl.BlockSpec((8, D_MODEL), lambda l, j: (0, 0)),
            pl.BlockSpec((None, D_MODEL, tn), lambda l, j: (l, 0, j)),
            pl.BlockSpec((None, 1, tn), lambda l, j: (l, 0, j)),
        ],
        out_specs=pl.BlockSpec((None, 8, tn), lambda l, j: (l, 0, j)),
        out_shape=jax.ShapeDtypeStruct((DEPTH, 8, n), F32),
        compiler_params=_params(("arbitrary", "arbitrary")),
        name="modulation",
    )(cond8, ada_w, ada_b.reshape(DEPTH, 1, n))


def _s5_operators(lam_re, lam_im, log_dt, bt_re, bt_im, c_re, c_im):
    G, T, H, P, CW = S5_GROUPS, S5_CHUNK, S5_GROUP, S5_STATE, S5_CW

    def body(lre_ref, lim_ref, ldt_ref, btr_ref, bti_ref, cr_ref, ci_ref,
             kd_ref, we_ref, wo_ref, at_ref):
        slot = pl.program_id(0) % S5_LB_GROUPS
        row_blk = lax.broadcasted_iota(jnp.int32, (LANES, 2 * P), 0) // H
        kn, we_parts, wo_parts, at_parts = [], [], [], []
        for d in range(2):
            lr = jnp.minimum(lre_ref[d, 0], -1e-4)
            li = lim_ref[d, 0]
            dt = jnp.exp(ldt_ref[d, 0])
            mag = jnp.exp(lr * dt)
            ar = mag * jnp.cos(li * dt)
            ai = mag * jnp.sin(li * dt)
            xr = ar - 1.0
            xi = ai
            den = lr * lr + li * li
            kr = (xr * lr + xi * li) / den
            ki = (xi * lr - xr * li) / den
            btr, bti = btr_ref[d, 0], bti_ref[d, 0]
            bbr = kr * btr - ki * bti
            bbi = kr * bti + ki * btr
            cr, ci = cr_ref[d, 0], ci_ref[d, 0]
            pr = [jnp.ones((1, P), F32)]
            pi = [jnp.zeros((1, P), F32)]
            for _ in range(T):
                pr.append(pr[-1] * ar - pi[-1] * ai)
                pi.append(pr[-2] * ai + pi[-1] * ar)
            ba_re = [bbr * pr[l] - bbi * pi[l] for l in range(T)]
            ba_im = [bbr * pi[l] + bbi * pr[l] for l in range(T)]
            rising = list(range(T))
            falling = rising[::-1]
            e_order = falling if d == 0 else rising
            e_re = jnp.concatenate([ba_re[l] for l in e_order], axis=0)
            e_im = jnp.concatenate([ba_im[l] for l in e_order], axis=0)
            k_order = rising if d == 0 else falling
            k_lhs = jnp.concatenate([jnp.concatenate([ba_re[l] for l in k_order], axis=0),
                                     jnp.concatenate([ba_im[l] for l in k_order], axis=0)], axis=1)
            cmat = jnp.concatenate([cr, -ci], axis=1)
            cmat_slot = jnp.where(row_blk == slot, jnp.concatenate([cmat] * S5_LB_GROUPS, axis=0), 0.0)
            kn.append(lax.dot_general(k_lhs, cmat_slot, (((1,), (1,)), ((), ())), precision=HIGHEST,
                                      preferred_element_type=F32))
            lags = [t + 1 for t in range(T)] if d == 0 else [T - t for t in range(T)]
            ca_re = jnp.concatenate([cr * pr[l] - ci * pi[l] for l in lags], axis=0)
            ca_im = jnp.concatenate([cr * pi[l] + ci * pr[l] for l in lags], axis=0)
            we_parts.append((e_re, e_im))
            wo_parts.append((ca_re, -ca_im))
            at_parts.append((pr[T], pi[T]))
        kf, kb = kn
        kd_ref[0] = jnp.concatenate(
            [kb[:CW - H], kb[CW - H:] + kf[:H], kf[H:], jnp.zeros((H, LANES), F32)], axis=0).astype(BF16)
        we_ref[0] = jnp.concatenate(
            [we_parts[0][0], we_parts[1][0], we_parts[0][1], we_parts[1][1]], axis=1).astype(BF16)
        wo_ref[0] = jnp.concatenate(
            [wo_parts[0][0], wo_parts[1][0], wo_parts[0][1], wo_parts[1][1]], axis=1).astype(BF16)
        at_ref[0] = jnp.concatenate(
            [at_parts[0][0], at_parts[1][0], at_parts[0][1], at_parts[1][1]], axis=1)

    vec_spec = pl.BlockSpec((2, 1, 1, P), lambda g: (0, g, 0, 0))
    mat_spec = pl.BlockSpec((2, 1, H, P), lambda g: (0, g, 0, 0))
    op_spec = pl.BlockSpec((1, CW, CW), lambda g: (g, 0, 0))
    op_shape = jax.ShapeDtypeStruct((G, CW, CW), BF16)
    return pl.pallas_call(
        body,
        grid=(G,),
        in_specs=[vec_spec, vec_spec, pl.BlockSpec((2, 1, 1, 1), lambda g: (0, g, 0, 0)),
                  mat_spec, mat_spec, mat_spec, mat_spec],
        out_specs=[pl.BlockSpec((1, 2 * CW, LANES), lambda g: (g, 0, 0)), op_spec, op_spec,
                   pl.BlockSpec((1, 1, CW), lambda g: (g, 0, 0))],
        out_shape=[jax.ShapeDtypeStruct((G, 2 * CW, LANES), BF16), op_shape, op_shape,
                   jax.ShapeDtypeStruct((G, 1, CW), F32)],
        compiler_params=_params(("arbitrary",)),
        name="s5_operators",
    )(lam_re.reshape(2, G, 1, P), lam_im.reshape(2, G, 1, P), log_dt.reshape(2, G, 1, 1),
      bt_re, bt_im, c_re, c_im)


def _s5_gather_chunks(u2_ref, x_ref, sems, lb):
    def copies(lane_block, slot):
        return [pltpu.make_async_copy(u2_ref.at[:, pl.ds(s * D_MODEL + lane_block * LANES, LANES)],
                                      x_ref.at[slot, :, pl.ds(s * LANES, LANES)], sems.at[slot])
                for s in range(S5_CHUNK)]

    slot = lb % 2

    @pl.when(lb == 0)
    def _():
        for cp in copies(0, 0):
            cp.start()

    @pl.when(lb + 1 < pl.num_programs(0))
    def _():
        for cp in copies(lb + 1, 1 - slot):
            cp.start()

    return copies(lb, slot), slot


def _s5_block_diag(dst_ref, src_ref):
    dst_ref[...] = jnp.zeros_like(dst_ref)
    for g in range(S5_LB_GROUPS):
        for s in range(S5_CHUNK):
            dst_ref[s * LANES + g * S5_GROUP:s * LANES + (g + 1) * S5_GROUP, g * S5_CW:(g + 1) * S5_CW] = (
                src_ref[g, s * S5_GROUP:(s + 1) * S5_GROUP, :])


def _s5_local_state(u2, we):
    nc = u2.shape[0]
    k = S5_CHUNK * LANES
    n = S5_LB_GROUPS * S5_CW

    def body(u2_ref, we_ref, out_ref, x_ref, w_ref, sems):
        lb = pl.program_id(0)
        gathers, slot = _s5_gather_chunks(u2_ref, x_ref, sems, lb)
        _s5_block_diag(w_ref, we_ref)
        for cp in gathers:
            cp.wait()
        states = jnp.dot(x_ref[slot], w_ref[...], preferred_element_type=F32)
        for g in range(S5_LB_GROUPS):
            out_ref[:, g, :] = states[:, g * S5_CW:(g + 1) * S5_CW]

    return pl.pallas_call(
        body,
        grid=(D_MODEL // LANES,),
        in_specs=[pl.BlockSpec(memory_space=pl.ANY),
                  pl.BlockSpec((S5_LB_GROUPS, S5_CW, S5_CW), lambda lb: (lb, 0, 0))],
        out_specs=pl.BlockSpec((nc, S5_LB_GROUPS, S5_CW), lambda lb: (0, lb, 0)),
        out_shape=jax.ShapeDtypeStruct((nc, S5_GROUPS, S5_CW), F32),
        scratch_shapes=[pltpu.VMEM((2, nc, k), BF16), pltpu.VMEM((k, n), BF16), pltpu.SemaphoreType.DMA((2,))],
        compiler_params=_params(("arbitrary",)),
        name="s5_local_state",
    )(u2, we)


def _s5_carry(sloc_t, at):
    nc, G, cw = sloc_t.shape
    cb = CTX_LEN // S5_CHUNK
    nb = nc // cb
    half = cw // 2

    def rev_block(s):
        return jnp.where(s == 0, 0, nb - s)

    def body(sf_ref, sb_ref, at_ref, pf_ref, pb_ref, st_ref):
        s = pl.program_id(0)

        @pl.when(s == 0)
        def _():
            st_ref[...] = jnp.zeros_like(st_ref)

        is_fwd = (lax.broadcasted_iota(jnp.int32, (G, half), 1) < S5_STATE)
        a_re, a_im = at_ref[:, :half], at_ref[:, half:]
        s_re, s_im = st_ref[:, :half], st_ref[:, half:]
        for k in range(cb):
            kb = cb - 1 - k
            pf_ref[k, :, :half] = s_re.astype(BF16)
            pf_ref[k, :, half:] = s_im.astype(BF16)
            pb_ref[kb, :, :half] = s_re.astype(BF16)
            pb_ref[kb, :, half:] = s_im.astype(BF16)
            l_re = jnp.where(is_fwd, sf_ref[k, :, :half], sb_ref[kb, :, :half])
            l_im = jnp.where(is_fwd, sf_ref[k, :, half:], sb_ref[kb, :, half:])
            n_re = a_re * s_re - a_im * s_im + l_re
            n_im = a_re * s_im + a_im * s_re + l_im
            s_re, s_im = n_re, n_im
        st_ref[:, :half] = s_re
        st_ref[:, half:] = s_im

    blk = (cb, G, cw)
    return pl.pallas_call(
        body,
        grid=(nb,),
        in_specs=[pl.BlockSpec(blk, lambda s: (s, 0, 0)),
                  pl.BlockSpec(blk, lambda s: (rev_block(s), 0, 0)),
                  pl.BlockSpec((G, cw), lambda s: (0, 0))],
        out_specs=[pl.BlockSpec(blk, lambda s: (s, 0, 0)),
                   pl.BlockSpec(blk, lambda s: (rev_block(s), 0, 0))],
        out_shape=[jax.ShapeDtypeStruct((nc, G, cw), BF16)] * 2,
        scratch_shapes=[pltpu.VMEM((G, cw), F32)],
        compiler_params=_params(("arbitrary",)),
        name="s5_carry",
    )(sloc_t, sloc_t, at)


def _s5_output(u2, kd, wo, pf, pb, d3, j):
    nc = u2.shape[0]
    T = S5_CHUNK
    k = T * LANES
    n = S5_LB_GROUPS * S5_CW
    nt = (((1,), (1,)), ((), ()))

    def body(u2_ref, kd_ref, wo_ref, pf_ref, pb_ref, d_ref, y2_ref,
             x_ref, toep_ref, w_ref, bd_ref, y_ref, sem_in, sem_out):
        lb = pl.program_id(0)
        gathers, slot = _s5_gather_chunks(u2_ref, x_ref, sem_in, lb)
        for lag in range(2 * T - 1):
            for g in range(S5_LB_GROUPS):
                bd_ref[lag * LANES + g * S5_GROUP:lag * LANES + (g + 1) * S5_GROUP, :] = (
                    kd_ref[g, lag * S5_GROUP:(lag + 1) * S5_GROUP, :])
        for s in range(T):
            for t in range(T):
                lag = t - s + T - 1
                toep_ref[s * LANES:(s + 1) * LANES, t * LANES:(t + 1) * LANES] = (
                    bd_ref[lag * LANES:(lag + 1) * LANES, :])
        _s5_block_diag(w_ref, wo_ref)
        is_fwd = (lax.broadcasted_iota(jnp.int32, (nc, n), 1) % LANES) < S5_STATE
        p = jnp.where(is_fwd, pf_ref[...], pb_ref[...])
        for cp in gathers:
            cp.wait()
        x = x_ref[slot]
        y = jnp.dot(x, toep_ref[...], preferred_element_type=F32)
        y = y + lax.dot_general(p, w_ref[...], nt, preferred_element_type=F32)
        y = y + jnp.concatenate([d_ref[...]] * T, axis=1) * x.astype(F32)
        def scatters(lane_block):
            return [pltpu.make_async_copy(y_ref.at[:, pl.ds(t * LANES, LANES)],
                                          y2_ref.at[:, pl.ds(t * D_MODEL + lane_block * LANES, LANES)], sem_out)
                    for t in range(T)]

        @pl.when(lb > 0)
        def _():
            for cp in scatters(lb - 1):
                cp.wait()

        y_ref[...] = y.astype(BF16)
        for cp in scatters(lb):
            cp.start()

        @pl.when(lb == pl.num_programs(0) - 1)
        def _():
            for cp in scatters(lb):
                cp.wait()

    state = pl.BlockSpec((nc, n), lambda lb: (0, lb))
    return pl.pallas_call(
        body,
        grid=(D_MODEL // LANES,),
        in_specs=[pl.BlockSpec(memory_space=pl.ANY),
                  pl.BlockSpec((S5_LB_GROUPS, 2 * S5_CW, LANES), lambda lb: (lb, 0, 0)),
                  pl.BlockSpec((S5_LB_GROUPS, S5_CW, S5_CW), lambda lb: (lb, 0, 0)),
                  state, state,
                  pl.BlockSpec((None, 1, LANES), lambda lb: (j, 0, lb))],
        out_specs=pl.BlockSpec(memory_space=pl.ANY),
        out_shape=jax.ShapeDtypeStruct((nc, T * D_MODEL), BF16),
        scratch_shapes=[pltpu.VMEM((2, nc, k), BF16), pltpu.VMEM((k, k), BF16), pltpu.VMEM((k, n), BF16),
                        pltpu.VMEM((2 * T * LANES, LANES), BF16), pltpu.VMEM((nc, k), BF16),
                        pltpu.SemaphoreType.DMA((2,)), pltpu.SemaphoreType.DMA],
        compiler_params=_params(("arbitrary",)),
        name="s5_output",
    )(u2, kd, wo, pf, pb, d3)


def _s5_layer(h, layer, j, mod, norm_g, p):
    n_rows = h.shape[0]
    G, T, H = S5_GROUPS, S5_CHUNK, S5_GROUP
    nc = n_rows // T
    tm = _pick(n_rows, MIX_ROW_TILES)
    tn = 512
    u = _fused_matmul(
        name="s5_in", n_rows=n_rows, k_dim=D_MODEL, n_out=D_MODEL, tm=tm, tn=tn,
        pro_args=(h, norm_g, mod), pro_specs=_norm_specs(layer, tm),
        pro_fn=lambda i, a_ref, h_ref, g_ref, m_ref: _norm_mod_into(a_ref, i, tm, h_ref, g_ref, m_ref, 0),
        w=p['s5_w_in'], w_layer=j, w_col_offsets=(0,),
        epi_args=(), epi_specs=(), epi_fn=lambda i, jj, ys: ys[0], out_dtype=BF16)
    u2 = u.reshape(nc, T * D_MODEL)
    kd, we, wo, at = _s5_operators(
        p['s5_lam_re'][j], p['s5_lam_im'][j], p['s5_log_dt'][j],
        jnp.swapaxes(p['s5_b_re'][j], -1, -2), jnp.swapaxes(p['s5_b_im'][j], -1, -2),
        p['s5_c_re'][j], p['s5_c_im'][j])
    sloc = _s5_local_state(u2, we)
    pf, pb = _s5_carry(sloc, at.reshape(G, T * H))
    d3 = p['s5_d'].reshape(-1, 1, D_MODEL)
    y2 = _s5_output(u2, kd, wo, pf.reshape(nc, G * T * H), pb.reshape(nc, G * T * H), d3, j)
    y = y2.reshape(n_rows, D_MODEL)

    def gelu_into(i, a_ref, y_ref):
        a_ref[...] = _gelu_tanh(y_ref[...].astype(F32)).astype(BF16)

    return _fused_matmul(
        name="s5_out", n_rows=n_rows, k_dim=D_MODEL, n_out=D_MODEL, tm=tm, tn=tn,
        pro_args=(y,), pro_specs=[pl.BlockSpec((tm, D_MODEL), lambda i, jj: (i, 0), pipeline_mode=pl.Buffered(1))],
        pro_fn=gelu_into,
        w=p['s5_w_out'], w_layer=j, w_col_offsets=(0, D_MODEL),
        epi_args=(h, mod), epi_specs=_res_specs(layer, tm, tn),
        epi_fn=lambda i, jj, ys, h_ref, m_ref: _gated_residual(
            i, tm, 2, ys[0] * _sigmoid(ys[1]), h_ref, m_ref),
        out_dtype=F32)


def _sg_layer(h, layer, j, mod, norm_g, p):
    n_rows = h.shape[0]
    tm = _pick(n_rows, MIX_ROW_TILES)
    tn = 512
    z = _fused_matmul(
        name="sg_in", n_rows=n_rows, k_dim=D_MODEL, n_out=2 * D_MODEL, tm=tm, tn=tn,
        pro_args=(h, norm_g, mod), pro_specs=_norm_specs(layer, tm),
        pro_fn=lambda i, a_ref, h_ref, g_ref, m_ref: _norm_mod_into(a_ref, i, tm, h_ref, g_ref, m_ref, 0),
        w=p['sg_w_in'], w_layer=j, w_col_offsets=(0,),
        epi_args=(), epi_specs=(), epi_fn=lambda i, jj, ys: _gelu_tanh(ys[0]), out_dtype=BF16)

    def gate(i, a_ref, u_ref, v_ref, g_ref, b_ref, ws_ref, bs_ref):
        ln_g, ln_b = g_ref[...], b_ref[...]

        def chunk(c, carry):
            rows = pl.ds(pl.multiple_of(c * SG_CHUNK, SG_CHUNK), SG_CHUNK)
            v = _layer_norm(v_ref[rows, :].astype(F32), ln_g, ln_b).astype(BF16)
            for hd in range(SG_HEADS):
                cols = slice(hd * SG_HEAD_DIM, (hd + 1) * SG_HEAD_DIM)
                sv = jnp.dot(ws_ref[hd].astype(BF16), v[:, cols], preferred_element_type=F32)
                gated = u_ref[rows, cols].astype(F32) * (sv + bs_ref[:, hd:hd + 1])
                a_ref[rows, cols] = gated.astype(BF16)
            return carry

        lax.fori_loop(0, tm // SG_CHUNK, chunk, 0)

    return _fused_matmul(
        name="sg_out", n_rows=n_rows, k_dim=D_MODEL, n_out=D_MODEL, tm=tm, tn=tn,
        pro_args=(z, z, p['sg_ln_g'], p['sg_ln_b'], p['sg_w_s'], jnp.swapaxes(p['sg_b_s'], -1, -2)),
        pro_specs=[
            pl.BlockSpec((tm, D_MODEL), lambda i, jj: (i, 0), pipeline_mode=pl.Buffered(1)),
            pl.BlockSpec((tm, D_MODEL), lambda i, jj: (i, 1), pipeline_mode=pl.Buffered(1)),
            pl.BlockSpec((1, D_MODEL), lambda i, jj: (j, 0)),
            pl.BlockSpec((1, D_MODEL), lambda i, jj: (j, 0)),
            pl.BlockSpec((None, SG_HEADS, SG_CHUNK, SG_CHUNK), lambda i, jj: (j, 0, 0, 0)),
            pl.BlockSpec((None, SG_CHUNK, SG_HEADS), lambda i, jj: (j, 0, 0)),
        ],
        pro_fn=gate,
        w=p['sg_w_out'], w_layer=j, w_col_offsets=(0,),
        epi_args=(h, mod), epi_specs=_res_specs(layer, tm, tn),
        epi_fn=lambda i, jj, ys, h_ref, m_ref: _gated_residual(i, tm, 2, ys[0], h_ref, m_ref),
        out_dtype=F32)


def _conv_norm(z, dw_w, dw_b, ln_g, ln_b, j):
    n_rows = z.shape[0]
    tm = CTX_LEN
    n_tiles = n_rows // tm
    hb = tm // CONV_HALO
    half = CONV_WIDTH // 2
    rb, cbw = 64, LANES

    def body(prev_ref, cur_ref, next_ref, w_ref, b_ref, g_ref, beta_ref, out_ref, zp_ref, acc_ref):
        i = pl.program_id(0)
        prev_ok = i >= 2
        next_ok = jnp.logical_and(i >= 1, i < n_tiles - 1)
        zp_ref[0:CONV_HALO, :] = jnp.where(prev_ok, prev_ref[...], 0.0)
        zp_ref[CONV_HALO:CONV_HALO + tm, :] = cur_ref[...]
        zp_ref[CONV_HALO + tm:, :] = jnp.where(next_ok, next_ref[...], 0.0)

        def col_block(cb, carry):
            c0 = pl.multiple_of(cb * cbw, cbw)
            w = w_ref[:, pl.ds(c0, cbw)]
            for r in range(tm // rb):
                win = zp_ref[r * rb:r * rb + rb + 2 * CONV_HALO, pl.ds(c0, cbw)]
                span = rb + 2 * CONV_HALO - SUBLANES
                phases = [win[ph:ph + span, :] for ph in range(SUBLANES)]
                acc = jnp.zeros((rb, cbw), F32)
                for k in range(CONV_WIDTH):
                    off = CONV_HALO - half + k
                    base = off - off % SUBLANES
                    acc = acc + phases[off % SUBLANES][base:base + rb, :] * w[k:k + 1, :]
                acc_ref[r * rb:(r + 1) * rb, pl.ds(c0, cbw)] = acc
            return carry

        lax.fori_loop(0, D_MODEL // cbw, col_block, 0)
        y = _layer_norm(acc_ref[...] + b_ref[...], g_ref[...], beta_ref[...])
        out_ref[...] = _silu(y).astype(BF16)

    vec = pl.BlockSpec((1, D_MODEL), lambda i: (j, 0))
    return pl.pallas_call(
        body,
        grid=(n_tiles,),
        in_specs=[
            pl.BlockSpec((CONV_HALO, D_MODEL), lambda i: (jnp.maximum(i * hb - 1, 0), 0)),
            pl.BlockSpec((tm, D_MODEL), lambda i: (i, 0)),
            pl.BlockSpec((CONV_HALO, D_MODEL), lambda i: (jnp.minimum((i + 1) * hb, n_tiles * hb - 1), 0)),
            pl.BlockSpec((None, CONV_WIDTH, D_MODEL), lambda i: (j, 0, 0)),
            vec, vec, vec,
        ],
        out_specs=pl.BlockSpec((tm, D_MODEL), lambda i: (i, 0)),
        out_shape=jax.ShapeDtypeStruct((n_rows, D_MODEL), BF16),
        scratch_shapes=[pltpu.VMEM((tm + 2 * CONV_HALO, D_MODEL), F32), pltpu.VMEM((tm, D_MODEL), F32)],
        compiler_params=_params(("arbitrary",)),
        name="conv_norm",
    )(z, z, z, dw_w, dw_b, ln_g, ln_b)


def _conv_layer(h, layer, j, mod, norm_g, p):
    n_rows = h.shape[0]
    tm = _pick(n_rows, MIX_ROW_TILES)
    tn = 512
    z = _fused_matmul(
        name="cv_pw1", n_rows=n_rows, k_dim=D_MODEL, n_out=D_MODEL, tm=tm, tn=tn,
        pro_args=(h, norm_g, mod), pro_specs=_norm_specs(layer, tm),
        pro_fn=lambda i, a_ref, h_ref, g_ref, m_ref: _norm_mod_into(a_ref, i, tm, h_ref, g_ref, m_ref, 0),
        w=p['cv_w_pw1'], w_layer=j, w_col_offsets=(0, D_MODEL),
        epi_args=(), epi_specs=(), epi_fn=lambda i, jj, ys: ys[0] * _sigmoid(ys[1]), out_dtype=F32)
    zc = _conv_norm(z, p['cv_dw_w'], p['cv_dw_b'], p['cv_ln_g'], p['cv_ln_b'], j)

    def copy_into(i, a_ref, z_ref):
        a_ref[...] = z_ref[...]

    return _fused_matmul(
        name="cv_pw2", n_rows=n_rows, k_dim=D_MODEL, n_out=D_MODEL, tm=tm, tn=tn,
        pro_args=(zc,), pro_specs=[pl.BlockSpec((tm, D_MODEL), lambda i, jj: (i, 0), pipeline_mode=pl.Buffered(1))],
        pro_fn=copy_into,
        w=p['cv_w_pw2'], w_layer=j, w_col_offsets=(0,),
        epi_args=(h, mod), epi_specs=_res_specs(layer, tm, tn),
        epi_fn=lambda i, jj, ys, h_ref, m_ref: _gated_residual(i, tm, 2, ys[0], h_ref, m_ref),
        out_dtype=F32)


SLAB_RANK1, SLAB_RANK2, SLAB_EXP1, SLAB_EXP2, SLAB_W1, SLAB_W2 = range(6)


def _router(h, layer, k, mod, norm_g, w_router):
    n_rows = h.shape[0]
    tm = _pick(n_rows, (768, 256))
    wr = jnp.pad(w_router[k], ((0, 0), (0, LANES - N_EXPERTS)))

    def body(h_ref, g_ref, m_ref, wr_ref, hn_ref, slab_ref, cnt_ref, base_ref):
        i = pl.program_id(0)

        @pl.when(i == 0)
        def _():
            base_ref[...] = jnp.zeros_like(base_ref)

        hn = _norm_mod(i, tm, h_ref, g_ref, m_ref, 3)
        hn_ref[...] = hn
        logits = jnp.dot(hn, wr_ref[...], precision=HIGHEST, preferred_element_type=F32)
        lane = lax.broadcasted_iota(jnp.int32, (tm, LANES), 1).astype(F32)
        neg = jnp.float32(-jnp.inf)
        lg = jnp.where(lane < N_EXPERTS, logits, neg)
        m1 = jnp.max(lg, axis=1, keepdims=True)
        i1 = jnp.min(jnp.where(lg == m1, lane, float(LANES)), axis=1, keepdims=True)
        lg2 = jnp.where(lane == i1, neg, lg)
        m2 = jnp.max(lg2, axis=1, keepdims=True)
        i2 = jnp.min(jnp.where(lg2 == m2, lane, float(LANES)), axis=1, keepdims=True)
        e = jnp.exp(m2 - m1)
        w1 = 1.0 / (1.0 + e)
        w2 = e / (1.0 + e)
        onehot = jnp.where(lane == i1, 1.0, 0.0) + jnp.where(lane == i2, 1.0, 0.0)
        tri = jnp.where(lax.broadcasted_iota(jnp.int32, (tm, tm), 1)
                        < lax.broadcasted_iota(jnp.int32, (tm, tm), 0), 1.0, 0.0).astype(BF16)
        base = base_ref[0:1, :]
        excl = jnp.dot(tri, onehot.astype(BF16), preferred_element_type=F32) + base
        r1 = jnp.sum(jnp.where(lane == i1, excl, 0.0), axis=1, keepdims=True)
        r2 = jnp.sum(jnp.where(lane == i2, excl, 0.0), axis=1, keepdims=True)
        total = base + jnp.sum(onehot, axis=0, keepdims=True)
        base_ref[...] = jnp.broadcast_to(total, base_ref.shape)
        cnt_ref[...] = jnp.broadcast_to(total, cnt_ref.shape)
        slab = jnp.zeros((tm, LANES), F32)
        for ln, val in ((SLAB_RANK1, r1), (SLAB_RANK2, r2), (SLAB_EXP1, i1), (SLAB_EXP2, i2),
                        (SLAB_W1, w1), (SLAB_W2, w2)):
            slab = jnp.where(lane == float(ln), val, slab)
        slab_ref[...] = slab

    return pl.pallas_call(
        body,
        grid=(n_rows // tm,),
        in_specs=[
            pl.BlockSpec((tm, D_MODEL), lambda i: (i, 0)),
            pl.BlockSpec((None, 1, D_MODEL), lambda i: (layer, 0, 0)),
            pl.BlockSpec((None, 2, N_MOD, D_MODEL), lambda i: (layer, 0, 0, 0)),
            pl.BlockSpec((D_MODEL, LANES), lambda i: (0, 0)),
        ],
        out_specs=[pl.BlockSpec((tm, D_MODEL), lambda i: (i, 0)),
                   pl.BlockSpec((tm, LANES), lambda i: (i, 0)),
                   pl.BlockSpec((8, LANES), lambda i: (0, 0))],
        out_shape=[jax.ShapeDtypeStruct((n_rows, D_MODEL), F32),
                   jax.ShapeDtypeStruct((n_rows, LANES), F32),
                   jax.ShapeDtypeStruct((8, LANES), F32)],
        scratch_shapes=[pltpu.VMEM((8, LANES), F32)],
        compiler_params=_params(("arbitrary",)),
        name="router",
    )(h, norm_g, mod, wr)


def _routing_tables(slab, counts, tb, nblk):
    cnt = counts[0, :N_EXPERTS].astype(jnp.int32)
    padded = ((cnt + tb - 1) // tb) * tb
    ends = jnp.cumsum(padded)
    starts = ends - padded
    pos1 = starts[slab[:, SLAB_EXP1].astype(jnp.int32)] + slab[:, SLAB_RANK1].astype(jnp.int32)
    pos2 = starts[slab[:, SLAB_EXP2].astype(jnp.int32)] + slab[:, SLAB_RANK2].astype(jnp.int32)
    nvalid = ends[-1] // tb
    blk = jnp.minimum(jnp.arange(nblk, dtype=jnp.int32), nvalid - 1)
    blk_expert = jnp.minimum(jnp.searchsorted(ends, blk * tb, side='right'), N_EXPERTS - 1).astype(jnp.int32)
    left = cnt[blk_expert] - (blk * tb - starts[blk_expert])
    blk_rows = ((jnp.clip(left, 0, tb) + MOE_SUB - 1) // MOE_SUB) * MOE_SUB
    return pos1, pos2, blk_expert, blk_rows.astype(jnp.int32), nvalid.reshape(1).astype(jnp.int32)


def _dispatch(hn, pos1, pos2, n_slots):
    n_rows = hn.shape[0]
    tm = CTX_LEN

    def body(p1_ref, p2_ref, hn_ref, init_ref, xs_ref, sem):
        i = pl.program_id(0)

        def row_copy(t, p):
            return pltpu.make_async_copy(hn_ref.at[pl.ds(t, 1)], xs_ref.at[pl.ds(p, 1)], sem)

        def issue(t, carry):
            row_copy(t, p1_ref[i * tm + t]).start()
            row_copy(t, p2_ref[i * tm + t]).start()
            return carry

        def drain(t, carry):
            row_copy(0, 0).wait()
            row_copy(0, 0).wait()
            return carry

        lax.fori_loop(0, tm, issue, 0, unroll=8)
        lax.fori_loop(0, tm, drain, 0, unroll=8)

    return pl.pallas_call(
        body,
        grid_spec=pltpu.PrefetchScalarGridSpec(
            num_scalar_prefetch=2,
            grid=(n_rows // tm,),
            in_specs=[pl.BlockSpec((tm, D_MODEL), lambda i, p1, p2: (i, 0)),
                      pl.BlockSpec(memory_space=pl.ANY)],
            out_specs=pl.BlockSpec(memory_space=pl.ANY),
            scratch_shapes=[pltpu.SemaphoreType.DMA],
        ),
        out_shape=jax.ShapeDtypeStruct((n_slots, D_MODEL), F32),
        input_output_aliases={3: 0},
        compiler_params=_params(("arbitrary",)),
        name="moe_dispatch",
    )(pos1, pos2, hn, jnp.zeros((n_slots, D_MODEL), F32))


def _swiglu_chunk(a, wg_ref, wu_ref, wd_ref):
    g = jnp.dot(a, wg_ref[...].astype(BF16), preferred_element_type=F32)
    u = jnp.dot(a, wu_ref[...].astype(BF16), preferred_element_type=F32)
    return jnp.dot((_silu(g) * u).astype(BF16), wd_ref[...].astype(BF16), preferred_element_type=F32)


def _moe_ffn(xs, blk_expert, blk_rows, nvalid, w_gate, w_up, w_down, e_base, tb):
    n_slots = xs.shape[0]
    d_ff = w_gate.shape[-1]
    tf = FFN_CHUNK
    nf = d_ff // tf

    def body(be_ref, br_ref, nv_ref, x_ref, wg_ref, wu_ref, wd_ref, out_ref, a_ref):
        b = pl.program_id(0)
        f = pl.program_id(1)
        used = b < nv_ref[0]

        @pl.when(jnp.logical_and(used, f == 0))
        def _():
            a_ref[...] = x_ref[...].astype(BF16)
            out_ref[...] = jnp.zeros_like(out_ref)

        for n_rows_used in range(MOE_SUB, tb + 1, MOE_SUB):
            @pl.when(jnp.logical_and(used, br_ref[b] == n_rows_used))
            def _():
                rows = slice(0, n_rows_used)
                out_ref[rows, :] += _swiglu_chunk(a_ref[rows, :], wg_ref, wu_ref, wd_ref)

        @pl.when(jnp.logical_and(jnp.logical_not(used), f == 0))
        def _():
            out_ref[...] = jnp.zeros_like(out_ref)

    def f_idx(b, f, nv):
        return jnp.where(b < nv[0], f, nf - 1)

    once = pl.Buffered(1)
    rows = lambda b, f, be, br, nv: (jnp.minimum(b, nv[0] - 1), 0)
    return pl.pallas_call(
        body,
        grid_spec=pltpu.PrefetchScalarGridSpec(
            num_scalar_prefetch=3,
            grid=(n_slots // tb, nf),
            in_specs=[
                pl.BlockSpec((tb, D_MODEL), rows, pipeline_mode=once),
                pl.BlockSpec((None, D_MODEL, tf), lambda b, f, be, br, nv: (e_base + be[b], 0, f_idx(b, f, nv))),
                pl.BlockSpec((None, D_MODEL, tf), lambda b, f, be, br, nv: (e_base + be[b], 0, f_idx(b, f, nv))),
                pl.BlockSpec((None, tf, D_MODEL), lambda b, f, be, br, nv: (e_base + be[b], f_idx(b, f, nv), 0)),
            ],
            out_specs=pl.BlockSpec((tb, D_MODEL), lambda b, f, be, br, nv: (b, 0), pipeline_mode=once),
            scratch_shapes=[pltpu.VMEM((tb, D_MODEL), BF16)],
        ),
        out_shape=jax.ShapeDtypeStruct((n_slots, D_MODEL), F32),
        compiler_params=_params(("arbitrary", "arbitrary")),
        name="moe_ffn",
    )(blk_expert, blk_rows, nvalid, xs, w_gate, w_up, w_down)


def _combine(h, ys, slab, pos1, pos2, layer, mod):
    n_rows = h.shape[0]
    tm = CTX_LEN

    def body(p1_ref, p2_ref, h_ref, slab_ref, m_ref, ys_ref, out_ref, y1_ref, y2_ref, sem):
        i = pl.program_id(0)

        def row_copy(p, dst_ref, t):
            return pltpu.make_async_copy(ys_ref.at[pl.ds(p, 1)], dst_ref.at[pl.ds(t, 1)], sem)

        def issue(t, carry):
            row_copy(p1_ref[i * tm + t], y1_ref, t).start()
            row_copy(p2_ref[i * tm + t], y2_ref, t).start()
            return carry

        def drain(t, carry):
            row_copy(0, y1_ref, 0).wait()
            row_copy(0, y2_ref, 0).wait()
            return carry

        lax.fori_loop(0, tm, issue, 0, unroll=8)
        lax.fori_loop(0, tm, drain, 0, unroll=8)
        w1 = slab_ref[:, SLAB_W1:SLAB_W1 + 1]
        w2 = slab_ref[:, SLAB_W2:SLAB_W2 + 1]
        moe = w1 * y1_ref[...] + w2 * y2_ref[...]
        out_ref[...] = h_ref[...] + _mod_row(m_ref, 5, _is_ctx(i, tm)) * moe

    return pl.pallas_call(
        body,
        grid_spec=pltpu.PrefetchScalarGridSpec(
            num_scalar_prefetch=2,
            grid=(n_rows // tm,),
            in_specs=[
                pl.BlockSpec((tm, D_MODEL), lambda i, p1, p2: (i, 0)),
                pl.BlockSpec((tm, LANES), lambda i, p1, p2: (i, 0)),
                pl.BlockSpec((None, 2, N_MOD, D_MODEL), lambda i, p1, p2: (layer, 0, 0, 0)),
                pl.BlockSpec(memory_space=pl.ANY),
            ],
            out_specs=pl.BlockSpec((tm, D_MODEL), lambda i, p1, p2: (i, 0)),
            scratch_shapes=[pltpu.VMEM((tm, D_MODEL), F32), pltpu.VMEM((tm, D_MODEL), F32),
                            pltpu.SemaphoreType.DMA],
        ),
        out_shape=jax.ShapeDtypeStruct((n_rows, D_MODEL), F32),
        compiler_params=_params(("arbitrary",)),
        name="moe_combine",
    )(pos1, pos2, h, slab, mod, ys)


def _moe_layer(h, layer, k, mod, norm_g, w_router, w_gate, w_up, w_down):
    n_rows = h.shape[0]
    d_ff = w_gate.shape[-1]
    tb = 1024 if n_rows >= 4096 else 256
    nblk = -(-(TOP_K * n_rows + N_EXPERTS * (tb - 1)) // tb)
    hn, slab, counts = _router(h, layer, k, mod, norm_g, w_router)
    pos1, pos2, blk_expert, blk_rows, nvalid = _routing_tables(slab, counts, tb, nblk)
    xs = _dispatch(hn, pos1, pos2, nblk * tb)
    ys = _moe_ffn(xs, blk_expert, blk_rows, nvalid, w_gate.reshape(-1, D_MODEL, d_ff),
                  w_up.reshape(-1, D_MODEL, d_ff), w_down.reshape(-1, d_ff, D_MODEL), k * N_EXPERTS, tb)
    return _combine(h, ys, slab, pos1, pos2, layer, mod)


def _ffn(h, layer, mod, norm_g, w_gate, w_up, w_down, k):
    n_rows = h.shape[0]
    d_ff = w_gate.shape[-1]
    tm = _pick(n_rows, (768, 256))
    tf = FFN_CHUNK
    nf = d_ff // tf

    def body(h_ref, g_ref, m_ref, wg_ref, wu_ref, wd_ref, out_ref, a_ref):
        i = pl.program_id(0)
        f = pl.program_id(1)

        @pl.when(f == 0)
        def _():
            _norm_mod_into(a_ref, i, tm, h_ref, g_ref, m_ref, 3)
            out_ref[...] = jnp.zeros_like(out_ref)

        out_ref[...] += _swiglu_chunk(a_ref[...], wg_ref, wu_ref, wd_ref)

        @pl.when(f == nf - 1)
        def _():
            def chunk(r, carry):
                rows = pl.ds(pl.multiple_of(r * MOD_CHUNK, MOD_CHUNK), MOD_CHUNK)
                m = jnp.where(i * tm + r * MOD_CHUNK < CTX_LEN, 1, 0)
                out_ref[rows, :] = h_ref[rows, :] + m_ref[m, pl.ds(5, 1), :] * out_ref[rows, :]
                return carry

            lax.fori_loop(0, tm // MOD_CHUNK, chunk, 0)

    once = pl.Buffered(1)
    return pl.pallas_call(
        body,
        grid=(n_rows // tm, nf),
        in_specs=[
            pl.BlockSpec((tm, D_MODEL), lambda i, f: (i, 0), pipeline_mode=once),
            pl.BlockSpec((None, 1, D_MODEL), lambda i, f: (layer, 0, 0)),
            pl.BlockSpec((None, 2, N_MOD, D_MODEL), lambda i, f: (layer, 0, 0, 0)),
            pl.BlockSpec((None, D_MODEL, tf), lambda i, f: (k, 0, f)),
            pl.BlockSpec((None, D_MODEL, tf), lambda i, f: (k, 0, f)),
            pl.BlockSpec((None, tf, D_MODEL), lambda i, f: (k, f, 0)),
        ],
        out_specs=pl.BlockSpec((tm, D_MODEL), lambda i, f: (i, 0), pipeline_mode=once),
        out_shape=jax.ShapeDtypeStruct((n_rows, D_MODEL), F32),
        scratch_shapes=[pltpu.VMEM((tm, D_MODEL), BF16)],
        compiler_params=_params(("arbitrary", "arbitrary")),
        name="ffn",
    )(h, norm_g, mod, w_gate, w_up, w_down)


def _final_norm(h, final_g):
    n_rows = h.shape[0]
    tm = CTX_LEN

    def body(h_ref, g_ref, out_ref):
        x = h_ref[...]
        out_ref[...] = x * lax.rsqrt(jnp.mean(x * x, axis=-1, keepdims=True) + EPS) * g_ref[...]

    return pl.pallas_call(
        body,
        grid=((n_rows - CTX_LEN) // tm,),
        in_specs=[pl.BlockSpec((tm, D_MODEL), lambda i: (i + 1, 0)),
                  pl.BlockSpec((1, D_MODEL), lambda i: (0, 0))],
        out_specs=pl.BlockSpec((tm, D_MODEL), lambda i: (i, 0)),
        out_shape=jax.ShapeDtypeStruct((n_rows - CTX_LEN, D_MODEL), F32),
        compiler_params=_params(("arbitrary",)),
        name="final_norm",
    )(h, final_g.reshape(1, D_MODEL))


def _sincos_tables(rows):
    quarter = D_MODEL // 4
    omega = 1.0 / (POS_BASE ** (jnp.arange(quarter, dtype=F32) / quarter))

    def emb(pos):
        ang = pos[:, None] * omega[None, :]
        return jnp.concatenate([jnp.sin(ang), jnp.cos(ang)], axis=-1)

    return emb(jnp.arange(rows, dtype=F32)), emb(jnp.arange(GRID_W, dtype=F32))


def kernel(x, c, ctx, c_ctx, ada_w, ada_b, norm_mix_g, norm_ffn_g, final_g, s5_w_in, s5_lam_re, s5_lam_im, s5_log_dt, s5_b_re, s5_b_im, s5_c_re, s5_c_im, s5_d, s5_w_out, sg_w_in, sg_ln_g, sg_ln_b, sg_w_s, sg_b_s, sg_w_out, cv_w_pw1, cv_dw_w, cv_dw_b, cv_ln_g, cv_ln_b, cv_w_pw2, ff_w_gate, ff_w_up, ff_w_down, moe_w_router, moe_w_gate, moe_w_up, moe_w_down):
    bsz, seq, dim = x.shape
    assert bsz == 1 and dim == D_MODEL and ctx.shape == (1, CTX_LEN, D_MODEL)
    p = dict(s5_w_in=s5_w_in, s5_lam_re=s5_lam_re, s5_lam_im=s5_lam_im, s5_log_dt=s5_log_dt,
             s5_b_re=s5_b_re, s5_b_im=s5_b_im, s5_c_re=s5_c_re, s5_c_im=s5_c_im, s5_d=s5_d,
             s5_w_out=s5_w_out, sg_w_in=sg_w_in, sg_ln_g=sg_ln_g, sg_ln_b=sg_ln_b, sg_w_s=sg_w_s,
             sg_b_s=sg_b_s, sg_w_out=sg_w_out, cv_w_pw1=cv_w_pw1, cv_dw_w=cv_dw_w, cv_dw_b=cv_dw_b,
             cv_ln_g=cv_ln_g, cv_ln_b=cv_ln_b, cv_w_pw2=cv_w_pw2)
    row_tab, col_tab = _sincos_tables(seq // GRID_W)
    h = _embed(x[0], ctx[0], row_tab, col_tab)
    cond8 = jnp.concatenate([c, c_ctx[None, :], jnp.zeros((6, D_MODEL), F32)], axis=0)
    mod = _modulation(cond8, ada_w, ada_b)[:, :2, :].reshape(DEPTH, 2, N_MOD, D_MODEL)
    g_mix = norm_mix_g.reshape(DEPTH, 1, D_MODEL)
    g_ffn = norm_ffn_g.reshape(DEPTH, 1, D_MODEL)
    for layer in range(DEPTH):
        kind, j = layer % N_MIXERS, layer // N_MIXERS
        if kind == 0:
            h = _s5_layer(h, layer, j, mod, g_mix, p)
        elif kind == 1:
            h = _sg_layer(h, layer, j, mod, g_mix, p)
        else:
            h = _conv_layer(h, layer, j, mod, g_mix, p)
        k = layer // 2
        if layer % 2 == 0:
            h = _ffn(h, layer, mod, g_ffn, ff_w_gate, ff_w_up, ff_w_down, k)
        else:
            h = _moe_layer(h, layer, k, mod, g_ffn, moe_w_router, moe_w_gate, moe_w_up, moe_w_down)
    return _final_norm(h, final_g)[None]
```

```python
import functools
import math

import jax
import jax.numpy as jnp
from jax import lax
from jax.experimental import pallas as pl
from jax.experimental.pallas import tpu as pltpu

F32 = jnp.float32
BF16 = jnp.bfloat16
HIGHEST = lax.Precision.HIGHEST

D_MODEL = 2048
DEPTH = 4
GRID_W = 64
CTX_LEN = 256
POS_BASE = 10000.0
N_MIXERS = 3
N_MOD = 6
EPS = 1e-6
S5_GROUP = 16
S5_GROUPS = D_MODEL // S5_GROUP
S5_STATE = 64
S5_CHUNK = 16
S5_CW = S5_CHUNK * S5_GROUP
SG_CHUNK = 128
SG_HEADS = 8
SG_HEAD_DIM = D_MODEL // SG_HEADS
CONV_WIDTH = 31
CONV_HALO = 16
N_EXPERTS = 8
TOP_K = 2
LANES = 128
SUBLANES = 8
S5_LB_GROUPS = LANES // S5_GROUP
MOD_CHUNK = 128
FFN_CHUNK = 512
MOE_SUB = 256
MIX_ROW_TILES = (1408, 768, 256)
VMEM_LIMIT = 56 * 1024 * 1024


def _pick(n, candidates):
    for c in candidates:
        if n % c == 0:
            return c
    raise ValueError(f"no tile in {candidates} divides {n}")


def _params(sem):
    return pltpu.CompilerParams(dimension_semantics=sem, vmem_limit_bytes=VMEM_LIMIT)


def _sigmoid(x):
    return 1.0 / (1.0 + jnp.exp(-x))


def _silu(x):
    return x * _sigmoid(x)


def _gelu_tanh(x):
    return 0.5 * x * (1.0 + jnp.tanh(math.sqrt(2.0 / math.pi) * (x + 0.044715 * (x * x * x))))


def _is_ctx(i, tm):
    rows = i * tm + lax.broadcasted_iota(jnp.int32, (tm, 1), 0)
    return rows < CTX_LEN


def _mod_row(mod_ref, k, is_ctx):
    return jnp.where(is_ctx, mod_ref[1, k:k + 1, :], mod_ref[0, k:k + 1, :])


def _norm_mod_rows(row0, h, g, mod_ref, k_shift):
    inv = lax.rsqrt(jnp.mean(h * h, axis=-1, keepdims=True) + EPS)
    hn = (h * inv) * g
    is_ctx = (row0 + lax.broadcasted_iota(jnp.int32, (h.shape[0], 1), 0)) < CTX_LEN
    return hn * (1.0 + _mod_row(mod_ref, k_shift + 1, is_ctx)) + _mod_row(mod_ref, k_shift, is_ctx)


def _norm_mod(i, tm, h_ref, g_ref, mod_ref, k_shift):
    return _norm_mod_rows(i * tm, h_ref[...], g_ref[...], mod_ref, k_shift)


def _norm_mod_into(a_ref, i, tm, h_ref, g_ref, mod_ref, k_shift):
    assert tm % MOD_CHUNK == 0 and CTX_LEN % MOD_CHUNK == 0
    g = g_ref[...]

    def chunk(r, carry):
        rows = pl.ds(pl.multiple_of(r * MOD_CHUNK, MOD_CHUNK), MOD_CHUNK)
        m = jnp.where(i * tm + r * MOD_CHUNK < CTX_LEN, 1, 0)
        h = h_ref[rows, :]
        inv = lax.rsqrt(jnp.mean(h * h, axis=-1, keepdims=True) + EPS)
        scale = mod_ref[m, pl.ds(k_shift + 1, 1), :]
        shift = mod_ref[m, pl.ds(k_shift, 1), :]
        a_ref[rows, :] = (((h * inv) * g) * (1.0 + scale) + shift).astype(a_ref.dtype)
        return carry

    lax.fori_loop(0, tm // MOD_CHUNK, chunk, 0)


def _layer_norm(v, g, b):
    mu = jnp.mean(v, axis=-1, keepdims=True)
    vc = v - mu
    var = jnp.mean(vc * vc, axis=-1, keepdims=True)
    return vc * lax.rsqrt(var + EPS) * g + b


def _fused_matmul(*, name, n_rows, k_dim, n_out, tm, tn, pro_args, pro_specs, pro_fn,
                  w, w_layer, w_col_offsets, epi_args, epi_specs, epi_fn, out_dtype):
    n_pro, n_w, n_epi = len(pro_args), len(w_col_offsets), len(epi_args)
    nj = n_out // tn

    def body(*refs):
        pro_refs = refs[:n_pro]
        w_refs = refs[n_pro:n_pro + n_w]
        epi_refs = refs[n_pro + n_w:n_pro + n_w + n_epi]
        out_ref, a_ref = refs[-2], refs[-1]
        i = pl.program_id(0)
        j = pl.program_id(1)

        @pl.when(j == 0)
        def _():
            pro_fn(i, a_ref, *pro_refs)

        a = a_ref[...]
        ys = [jnp.dot(a, wr[...].astype(BF16), preferred_element_type=F32) for wr in w_refs]
        out_ref[...] = epi_fn(i, j, ys, *epi_refs).astype(out_dtype)

    w_specs = [
        pl.BlockSpec((None, k_dim, tn), functools.partial(lambda i, j, o: (w_layer, 0, j + o), o=off // tn))
        for off in w_col_offsets
    ]
    return pl.pallas_call(
        body,
        grid=(n_rows // tm, nj),
        in_specs=list(pro_specs) + w_specs + list(epi_specs),
        out_specs=pl.BlockSpec((tm, tn), lambda i, j: (i, j)),
        out_shape=jax.ShapeDtypeStruct((n_rows, n_out), out_dtype),
        scratch_shapes=[pltpu.VMEM((tm, k_dim), BF16)],
        compiler_params=_params(("arbitrary", "arbitrary")),
        name=name,
    )(*pro_args, *([w] * n_w), *epi_args)


def _norm_specs(layer, tm, single_buffer_rows=False):
    return [
        pl.BlockSpec((tm, D_MODEL), lambda i, j: (i, 0),
                     pipeline_mode=pl.Buffered(1 if single_buffer_rows else 2)),
        pl.BlockSpec((None, 1, D_MODEL), lambda i, j: (layer, 0, 0)),
        pl.BlockSpec((None, 2, N_MOD, D_MODEL), lambda i, j: (layer, 0, 0, 0)),
    ]


def _res_specs(layer, tm, tn):
    return [
        pl.BlockSpec((tm, tn), lambda i, j: (i, j)),
        pl.BlockSpec((None, 2, N_MOD, tn), lambda i, j: (layer, 0, 0, j)),
    ]


def _gated_residual(i, tm, k_gate, y, h_ref, mod_ref):
    return h_ref[...] + _mod_row(mod_ref, k_gate, _is_ctx(i, tm)) * y


def _embed(x2, ctx2, row_tab, col_tab):
    seq = x2.shape[0]
    tm = 4 * GRID_W
    half = D_MODEL // 2
    n_tiles = (CTX_LEN + seq) // tm
    rows_per_tile = tm // GRID_W
    row_tab3 = row_tab.reshape(seq // tm, rows_per_tile, half)

    def body(ctx_ref, x_ref, row_ref, col_ref, out_ref):
        i = pl.program_id(0)

        @pl.when(i == 0)
        def _():
            out_ref[...] = ctx_ref[...]

        @pl.when(i > 0)
        def _():
            rt = row_ref[0]
            row_part = jnp.concatenate(
                [jnp.broadcast_to(rt[r:r + 1, :], (GRID_W, half)) for r in range(rows_per_tile)], axis=0)
            col_part = jnp.concatenate([col_ref[...]] * rows_per_tile, axis=0)
            out_ref[:, :half] = x_ref[:, :half] + row_part
            out_ref[:, half:] = x_ref[:, half:] + col_part

    assert CTX_LEN == tm
    return pl.pallas_call(
        body,
        grid=(n_tiles,),
        in_specs=[
            pl.BlockSpec((tm, D_MODEL), lambda i: (0, 0)),
            pl.BlockSpec((tm, D_MODEL), lambda i: (jnp.maximum(i - 1, 0), 0)),
            pl.BlockSpec((1, rows_per_tile, half), lambda i: (jnp.maximum(i - 1, 0), 0, 0)),
            pl.BlockSpec((GRID_W, half), lambda i: (0, 0)),
        ],
        out_specs=pl.BlockSpec((tm, D_MODEL), lambda i: (i, 0)),
        out_shape=jax.ShapeDtypeStruct((CTX_LEN + seq, D_MODEL), F32),
        compiler_params=_params(("arbitrary",)),
        name="embed",
    )(ctx2, x2, row_tab3, col_tab)


def _modulation(cond8, ada_w, ada_b):
    n = N_MOD * D_MODEL
    tn = 1024

    def body(c_ref, w_ref, b_ref, out_ref):
        a = _silu(c_ref[...]).astype(BF16)
        out_ref[...] = jnp.dot(a, w_ref[...].astype(BF16), preferred_element_type=F32) + b_ref[...]

    return pl.pallas_call(
        body,
        grid=(DEPTH, n // tn),
        in_specs=[
            pl.BlockSpec((8, D_MODEL), lambda l, j: (0, 0)),
            pl.BlockSpec((None, D_MODEL, tn), lambda l, j: (l, 0, j)),
            pl.BlockSpec((None, 1, tn), lambda l, j: (l, 0, j)),
        ],
        out_specs=pl.BlockSpec((None, 8, tn), lambda l, j: (l, 0, j)),
        out_shape=jax.ShapeDtypeStruct((DEPTH, 8, n), F32),
        compiler_params=_params(("arbitrary", "arbitrary")),
        name="modulation",
    )(cond8, ada_w, ada_b.reshape(DEPTH, 1, n))


def _s5_operators(lam_re, lam_im, log_dt, bt_re, bt_im, c_re, c_im):
    G, T, H, P, CW = S5_GROUPS, S5_CHUNK, S5_GROUP, S5_STATE, S5_CW

    def body(lre_ref, lim_ref, ldt_ref, btr_ref, bti_ref, cr_ref, ci_ref,
             kd_ref, we_ref, wo_ref, at_ref):
        slot = pl.program_id(0) % S5_LB_GROUPS
        row_blk = lax.broadcasted_iota(jnp.int32, (LANES, 2 * P), 0) // H
        kn, we_parts, wo_parts, at_parts = [], [], [], []
        for d in range(2):
            lr = jnp.minimum(lre_ref[d, 0], -1e-4)
            li = lim_ref[d, 0]
            dt = jnp.exp(ldt_ref[d, 0])
            mag = jnp.exp(lr * dt)
            ar = mag * jnp.cos(li * dt)
            ai = mag * jnp.sin(li * dt)
            xr = ar - 1.0
            xi = ai
            den = lr * lr + li * li
            kr = (xr * lr + xi * li) / den
            ki = (xi * lr - xr * li) / den
            btr, bti = btr_ref[d, 0], bti_ref[d, 0]
            bbr = kr * btr - ki * bti
            bbi = kr * bti + ki * btr
            cr, ci = cr_ref[d, 0], ci_ref[d, 0]
            pr = [jnp.ones((1, P), F32)]
            pi = [jnp.zeros((1, P), F32)]
            for _ in range(T):
                pr.append(pr[-1] * ar - pi[-1] * ai)
                pi.append(pr[-2] * ai + pi[-1] * ar)
            ba_re = [bbr * pr[l] - bbi * pi[l] for l in range(T)]
            ba_im = [bbr * pi[l] + bbi * pr[l] for l in range(T)]
            rising = list(range(T))
            falling = rising[::-1]
            e_order = falling if d == 0 else rising
            e_re = jnp.concatenate([ba_re[l] for l in e_order], axis=0)
            e_im = jnp.concatenate([ba_im[l] for l in e_order], axis=0)
            k_order = rising if d == 0 else falling
            k_lhs = jnp.concatenate([jnp.concatenate([ba_re[l] for l in k_order], axis=0),
                                     jnp.concatenate([ba_im[l] for l in k_order], axis=0)], axis=1)
            cmat = jnp.concatenate([cr, -ci], axis=1)
            cmat_slot = jnp.where(row_blk == slot, jnp.concatenate([cmat] * S5_LB_GROUPS, axis=0), 0.0)
            kn.append(lax.dot_general(k_lhs, cmat_slot, (((1,), (1,)), ((), ())), precision=HIGHEST,
                                      preferred_element_type=F32))
            lags = [t + 1 for t in range(T)] if d == 0 else [T - t for t in range(T)]
            ca_re = jnp.concatenate([cr * pr[l] - ci * pi[l] for l in lags], axis=0)
            ca_im = jnp.concatenate([cr * pi[l] + ci * pr[l] for l in lags], axis=0)
            we_parts.append((e_re, e_im))
            wo_parts.append((ca_re, -ca_im))
            at_parts.append((pr[T], pi[T]))
        kf, kb = kn
        kd_ref[0] = jnp.concatenate(
            [kb[:CW - H], kb[CW - H:] + kf[:H], kf[H:], jnp.zeros((H, LANES), F32)], axis=0).astype(BF16)
        we_ref[0] = jnp.concatenate(
            [we_parts[0][0], we_parts[1][0], we_parts[0][1], we_parts[1][1]], axis=1).astype(BF16)
        wo_ref[0] = jnp.concatenate(
            [wo_parts[0][0], wo_parts[1][0], wo_parts[0][1], wo_parts[1][1]], axis=1).astype(BF16)
        at_ref[0] = jnp.concatenate(
            [at_parts[0][0], at_parts[1][0], at_parts[0][1], at_parts[1][1]], axis=1)

    vec_spec = pl.BlockSpec((2, 1, 1, P), lambda g: (0, g, 0, 0))
    mat_spec = pl.BlockSpec((2, 1, H, P), lambda g: (0, g, 0, 0))
    op_spec = pl.BlockSpec((1, CW, CW), lambda g: (g, 0, 0))
    op_shape = jax.ShapeDtypeStruct((G, CW, CW), BF16)
    return pl.pallas_call(
        body,
        grid=(G,),
        in_specs=[vec_spec, vec_spec, pl.BlockSpec((2, 1, 1, 1), lambda g: (0, g, 0, 0)),
                  mat_spec, mat_spec, mat_spec, mat_spec],
        out_specs=[pl.BlockSpec((1, 2 * CW, LANES), lambda g: (g, 0, 0)), op_spec, op_spec,
                   pl.BlockSpec((1, 1, CW), lambda g: (g, 0, 0))],
        out_shape=[jax.ShapeDtypeStruct((G, 2 * CW, LANES), BF16), op_shape, op_shape,
                   jax.ShapeDtypeStruct((G, 1, CW), F32)],
        compiler_params=_params(("arbitrary",)),
        name="s5_operators",
    )(lam_re.reshape(2, G, 1, P), lam_im.reshape(2, G, 1, P), log_dt.reshape(2, G, 1, 1),
      bt_re, bt_im, c_re, c_im)


def _s5_gather_chunks(u2_ref, x_ref, sems, lb):
    def copies(lane_block, slot):
        return [pltpu.make_async_copy(u2_ref.at[:, pl.ds(s * D_MODEL + lane_block * LANES, LANES)],
                                      x_ref.at[slot, :, pl.ds(s * LANES, LANES)], sems.at[slot])
                for s in range(S5_CHUNK)]

    slot = lb % 2

    @pl.when(lb == 0)
    def _():
        for cp in copies(0, 0):
            cp.start()

    @pl.when(lb + 1 < pl.num_programs(0))
    def _():
        for cp in copies(lb + 1, 1 - slot):
            cp.start()

    return copies(lb, slot), slot


def _s5_block_diag(dst_ref, src_ref):
    dst_ref[...] = jnp.zeros_like(dst_ref)
    for g in range(S5_LB_GROUPS):
        for s in range(S5_CHUNK):
            dst_ref[s * LANES + g * S5_GROUP:s * LANES + (g + 1) * S5_GROUP, g * S5_CW:(g + 1) * S5_CW] = (
                src_ref[g, s * S5_GROUP:(s + 1) * S5_GROUP, :])


def _s5_local_state(u2, we):
    nc = u2.shape[0]
    k = S5_CHUNK * LANES
    n = S5_LB_GROUPS * S5_CW

    def body(u2_ref, we_ref, out_ref, x_ref, w_ref, sems):
        lb = pl.program_id(0)
        gathers, slot = _s5_gather_chunks(u2_ref, x_ref, sems, lb)
        _s5_block_diag(w_ref, we_ref)
        for cp in gathers:
            cp.wait()
        states = jnp.dot(x_ref[slot], w_ref[...], preferred_element_type=F32)
        for g in range(S5_LB_GROUPS):
            out_ref[:, g, :] = states[:, g * S5_CW:(g + 1) * S5_CW]

    return pl.pallas_call(
        body,
        grid=(D_MODEL // LANES,),
        in_specs=[pl.BlockSpec(memory_space=pl.ANY),
                  pl.BlockSpec((S5_LB_GROUPS, S5_CW, S5_CW), lambda lb: (lb, 0, 0))],
        out_specs=pl.BlockSpec((nc, S5_LB_GROUPS, S5_CW), lambda lb: (0, lb, 0)),
        out_shape=jax.ShapeDtypeStruct((nc, S5_GROUPS, S5_CW), F32),
        scratch_shapes=[pltpu.VMEM((2, nc, k), BF16), pltpu.VMEM((k, n), BF16), pltpu.SemaphoreType.DMA((2,))],
        compiler_params=_params(("arbitrary",)),
        name="s5_local_state",
    )(u2, we)


def _s5_carry(sloc_t, at):
    nc, G, cw = sloc_t.shape
    cb = CTX_LEN // S5_CHUNK
    nb = nc // cb
    half = cw // 2

    def rev_block(s):
        return jnp.where(s == 0, 0, nb - s)

    def body(sf_ref, sb_ref, at_ref, pf_ref, pb_ref, st_ref):
        s = pl.program_id(0)

        @pl.when(s == 0)
        def _():
            st_ref[...] = jnp.zeros_like(st_ref)

        is_fwd = (lax.broadcasted_iota(jnp.int32, (G, half), 1) < S5_STATE)
        a_re, a_im = at_ref[:, :half], at_ref[:, half:]
        s_re, s_im = st_ref[:, :half], st_ref[:, half:]
        for k in range(cb):
            kb = cb - 1 - k
            pf_ref[k, :, :half] = s_re.astype(BF16)
            pf_ref[k, :, half:] = s_im.astype(BF16)
            pb_ref[kb, :, :half] = s_re.astype(BF16)
            pb_ref[kb, :, half:] = s_im.astype(BF16)
            l_re = jnp.where(is_fwd, sf_ref[k, :, :half], sb_ref[kb, :, :half])
            l_im = jnp.where(is_fwd, sf_ref[k, :, half:], sb_ref[kb, :, half:])
            n_re = a_re * s_re - a_im * s_im + l_re
            n_im = a_re * s_im + a_im * s_re + l_im
            s_re, s_im = n_re, n_im
        st_ref[:, :half] = s_re
        st_ref[:, half:] = s_im

    blk = (cb, G, cw)
    return pl.pallas_call(
        body,
        grid=(nb,),
        in_specs=[pl.BlockSpec(blk, lambda s: (s, 0, 0)),
                  pl.BlockSpec(blk, lambda s: (rev_block(s), 0, 0)),
                  pl.BlockSpec((G, cw), lambda s: (0, 0))],
        out_specs=[pl.BlockSpec(blk, lambda s: (s, 0, 0)),
                   pl.BlockSpec(blk, lambda s: (rev_block(s), 0, 0))],
        out_shape=[jax.ShapeDtypeStruct((nc, G, cw), BF16)] * 2,
        scratch_shapes=[pltpu.VMEM((G, cw), F32)],
        compiler_params=_params(("arbitrary",)),
        name="s5_carry",
    )(sloc_t, sloc_t, at)


def _s5_output(u2, kd, wo, pf, pb, d3, j):
    nc = u2.shape[0]
    T = S5_CHUNK
    k = T * LANES
    n = S5_LB_GROUPS * S5_CW
    nt = (((1,), (1,)), ((), ()))

    def body(u2_ref, kd_ref, wo_ref, pf_ref, pb_ref, d_ref, y2_ref,
             x_ref, toep_ref, w_ref, bd_ref, y_ref, sem_in, sem_out):
        lb = pl.program_id(0)
        gathers, slot = _s5_gather_chunks(u2_ref, x_ref, sem_in, lb)
        for lag in range(2 * T - 1):
            for g in range(S5_LB_GROUPS):
                bd_ref[lag * LANES + g * S5_GROUP:lag * LANES + (g + 1) * S5_GROUP, :] = (
                    kd_ref[g, lag * S5_GROUP:(lag + 1) * S5_GROUP, :])
        for s in range(T):
            for t in range(T):
                lag = t - s + T - 1
                toep_ref[s * LANES:(s + 1) * LANES, t * LANES:(t + 1) * LANES] = (
                    bd_ref[lag * LANES:(lag + 1) * LANES, :])
        _s5_block_diag(w_ref, wo_ref)
        is_fwd = (lax.broadcasted_iota(jnp.int32, (nc, n), 1) % LANES) < S5_STATE
        p = jnp.where(is_fwd, pf_ref[...], pb_ref[...])
        for cp in gathers:
            cp.wait()
        x = x_ref[slot]
        y = jnp.dot(x, toep_ref[...], preferred_element_type=F32)
        y = y + lax.dot_general(p, w_ref[...], nt, preferred_element_type=F32)
        y = y + jnp.concatenate([d_ref[...]] * T, axis=1) * x.astype(F32)
        def scatters(lane_block):
            return [pltpu.make_async_copy(y_ref.at[:, pl.ds(t * LANES, LANES)],
                                          y2_ref.at[:, pl.ds(t * D_MODEL + lane_block * LANES, LANES)], sem_out)
                    for t in range(T)]

        @pl.when(lb > 0)
        def _():
            for cp in scatters(lb - 1):
                cp.wait()

        y_ref[...] = y.astype(BF16)
        for cp in scatters(lb):
            cp.start()

        @pl.when(lb == pl.num_programs(0) - 1)
        def _():
            for cp in scatters(lb):
                cp.wait()

    state = pl.BlockSpec((nc, n), lambda lb: (0, lb))
    return pl.pallas_call(
        body,
        grid=(D_MODEL // LANES,),
        in_specs=[pl.BlockSpec(memory_space=pl.ANY),
                  pl.BlockSpec((S5_LB_GROUPS, 2 * S5_CW, LANES), lambda lb: (lb, 0, 0)),
                  pl.BlockSpec((S5_LB_GROUPS, S5_CW, S5_CW), lambda lb: (lb, 0, 0)),
                  state, state,
                  pl.BlockSpec((None, 1, LANES), lambda lb: (j, 0, lb))],
        out_specs=pl.BlockSpec(memory_space=pl.ANY),
        out_shape=jax.ShapeDtypeStruct((nc, T * D_MODEL), BF16),
        scratch_shapes=[pltpu.VMEM((2, nc, k), BF16), pltpu.VMEM((k, k), BF16), pltpu.VMEM((k, n), BF16),
                        pltpu.VMEM((2 * T * LANES, LANES), BF16), pltpu.VMEM((nc, k), BF16),
                        pltpu.SemaphoreType.DMA((2,)), pltpu.SemaphoreType.DMA],
        compiler_params=_params(("arbitrary",)),
        name="s5_output",
    )(u2, kd, wo, pf, pb, d3)


def _s5_layer(h, layer, j, mod, norm_g, p):
    n_rows = h.shape[0]
    G, T, H = S5_GROUPS, S5_CHUNK, S5_GROUP
    nc = n_rows // T
    tm = _pick(n_rows, MIX_ROW_TILES)
    tn = 512
    u = _fused_matmul(
        name="s5_in", n_rows=n_rows, k_dim=D_MODEL, n_out=D_MODEL, tm=tm, tn=tn,
        pro_args=(h, norm_g, mod), pro_specs=_norm_specs(layer, tm),
        pro_fn=lambda i, a_ref, h_ref, g_ref, m_ref: _norm_mod_into(a_ref, i, tm, h_ref, g_ref, m_ref, 0),
        w=p['s5_w_in'], w_layer=j, w_col_offsets=(0,),
        epi_args=(), epi_specs=(), epi_fn=lambda i, jj, ys: ys[0], out_dtype=BF16)
    u2 = u.reshape(nc, T * D_MODEL)
    kd, we, wo, at = _s5_operators(
        p['s5_lam_re'][j], p['s5_lam_im'][j], p['s5_log_dt'][j],
        jnp.swapaxes(p['s5_b_re'][j], -1, -2), jnp.swapaxes(p['s5_b_im'][j], -1, -2),
        p['s5_c_re'][j], p['s5_c_im'][j])
    sloc = _s5_local_state(u2, we)
    pf, pb = _s5_carry(sloc, at.reshape(G, T * H))
    d3 = p['s5_d'].reshape(-1, 1, D_MODEL)
    y2 = _s5_output(u2, kd, wo, pf.reshape(nc, G * T * H), pb.reshape(nc, G * T * H), d3, j)
    y = y2.reshape(n_rows, D_MODEL)

    def gelu_into(i, a_ref, y_ref):
        a_ref[...] = _gelu_tanh(y_ref[...].astype(F32)).astype(BF16)

    return _fused_matmul(
        name="s5_out", n_rows=n_rows, k_dim=D_MODEL, n_out=D_MODEL, tm=tm, tn=tn,
        pro_args=(y,), pro_specs=[pl.BlockSpec((tm, D_MODEL), lambda i, jj: (i, 0))],
        pro_fn=gelu_into,
        w=p['s5_w_out'], w_layer=j, w_col_offsets=(0, D_MODEL),
        epi_args=(h, mod), epi_specs=_res_specs(layer, tm, tn),
        epi_fn=lambda i, jj, ys, h_ref, m_ref: _gated_residual(
            i, tm, 2, ys[0] * _sigmoid(ys[1]), h_ref, m_ref),
        out_dtype=F32)


def _sg_layer(h, layer, j, mod, norm_g, p):
    n_rows = h.shape[0]
    tm = _pick(n_rows, MIX_ROW_TILES)
    tn = 512
    z = _fused_matmul(
        name="sg_in", n_rows=n_rows, k_dim=D_MODEL, n_out=2 * D_MODEL, tm=tm, tn=tn,
        pro_args=(h, norm_g, mod), pro_specs=_norm_specs(layer, tm),
        pro_fn=lambda i, a_ref, h_ref, g_ref, m_ref: _norm_mod_into(a_ref, i, tm, h_ref, g_ref, m_ref, 0),
        w=p['sg_w_in'], w_layer=j, w_col_offsets=(0,),
        epi_args=(), epi_specs=(), epi_fn=lambda i, jj, ys: _gelu_tanh(ys[0]), out_dtype=BF16)

    def gate(i, a_ref, u_ref, v_ref, g_ref, b_ref, ws_ref, bs_ref):
        ln_g, ln_b = g_ref[...], b_ref[...]

        def chunk(c, carry):
            rows = pl.ds(pl.multiple_of(c * SG_CHUNK, SG_CHUNK), SG_CHUNK)
            v = _layer_norm(v_ref[rows, :].astype(F32), ln_g, ln_b).astype(BF16)
            for hd in range(SG_HEADS):
                cols = slice(hd * SG_HEAD_DIM, (hd + 1) * SG_HEAD_DIM)
                sv = jnp.dot(ws_ref[hd].astype(BF16), v[:, cols], preferred_element_type=F32)
                gated = u_ref[rows, cols].astype(F32) * (sv + bs_ref[:, hd:hd + 1])
                a_ref[rows, cols] = gated.astype(BF16)
            return carry

        lax.fori_loop(0, tm // SG_CHUNK, chunk, 0)

    return _fused_matmul(
        name="sg_out", n_rows=n_rows, k_dim=D_MODEL, n_out=D_MODEL, tm=tm, tn=tn,
        pro_args=(z, z, p['sg_ln_g'], p['sg_ln_b'], p['sg_w_s'], jnp.swapaxes(p['sg_b_s'], -1, -2)),
        pro_specs=[
            pl.BlockSpec((tm, D_MODEL), lambda i, jj: (i, 0)),
            pl.BlockSpec((tm, D_MODEL), lambda i, jj: (i, 1)),
            pl.BlockSpec((1, D_MODEL), lambda i, jj: (j, 0)),
            pl.BlockSpec((1, D_MODEL), lambda i, jj: (j, 0)),
            pl.BlockSpec((None, SG_HEADS, SG_CHUNK, SG_CHUNK), lambda i, jj: (j, 0, 0, 0)),
            pl.BlockSpec((None, SG_CHUNK, SG_HEADS), lambda i, jj: (j, 0, 0)),
        ],
        pro_fn=gate,
        w=p['sg_w_out'], w_layer=j, w_col_offsets=(0,),
        epi_args=(h, mod), epi_specs=_res_specs(layer, tm, tn),
        epi_fn=lambda i, jj, ys, h_ref, m_ref: _gated_residual(i, tm, 2, ys[0], h_ref, m_ref),
        out_dtype=F32)


def _conv_norm(z, dw_w, dw_b, ln_g, ln_b, j):
    n_rows = z.shape[0]
    tm = CTX_LEN
    n_tiles = n_rows // tm
    hb = tm // CONV_HALO
    half = CONV_WIDTH // 2
    rb, cbw = 64, LANES

    def body(prev_ref, cur_ref, next_ref, w_ref, b_ref, g_ref, beta_ref, out_ref, zp_ref, acc_ref):
        i = pl.program_id(0)
        prev_ok = i >= 2
        next_ok = jnp.logical_and(i >= 1, i < n_tiles - 1)
        zp_ref[0:CONV_HALO, :] = jnp.where(prev_ok, prev_ref[...], 0.0)
        zp_ref[CONV_HALO:CONV_HALO + tm, :] = cur_ref[...]
        zp_ref[CONV_HALO + tm:, :] = jnp.where(next_ok, next_ref[...], 0.0)

        def col_block(cb, carry):
            c0 = pl.multiple_of(cb * cbw, cbw)
            w = w_ref[:, pl.ds(c0, cbw)]
            for r in range(tm // rb):
                win = zp_ref[r * rb:r * rb + rb + 2 * CONV_HALO, pl.ds(c0, cbw)]
                span = rb + 2 * CONV_HALO - SUBLANES
                phases = [win[ph:ph + span, :] for ph in range(SUBLANES)]
                acc = jnp.zeros((rb, cbw), F32)
                for k in range(CONV_WIDTH):
                    off = CONV_HALO - half + k
                    base = off - off % SUBLANES
                    acc = acc + phases[off % SUBLANES][base:base + rb, :] * w[k:k + 1, :]
                acc_ref[r * rb:(r + 1) * rb, pl.ds(c0, cbw)] = acc
            return carry

        lax.fori_loop(0, D_MODEL // cbw, col_block, 0)
        y = _layer_norm(acc_ref[...] + b_ref[...], g_ref[...], beta_ref[...])
        out_ref[...] = _silu(y).astype(BF16)

    vec = pl.BlockSpec((1, D_MODEL), lambda i: (j, 0))
    return pl.pallas_call(
        body,
        grid=(n_tiles,),
        in_specs=[
            pl.BlockSpec((CONV_HALO, D_MODEL), lambda i: (jnp.maximum(i * hb - 1, 0), 0)),
            pl.BlockSpec((tm, D_MODEL), lambda i: (i, 0)),
            pl.BlockSpec((CONV_HALO, D_MODEL), lambda i: (jnp.minimum((i + 1) * hb, n_tiles * hb - 1), 0)),
            pl.BlockSpec((None, CONV_WIDTH, D_MODEL), lambda i: (j, 0, 0)),
            vec, vec, vec,
        ],
        out_specs=pl.BlockSpec((tm, D_MODEL), lambda i: (i, 0)),
        out_shape=jax.ShapeDtypeStruct((n_rows, D_MODEL), BF16),
        scratch_shapes=[pltpu.VMEM((tm + 2 * CONV_HALO, D_MODEL), F32), pltpu.VMEM((tm, D_MODEL), F32)],
        compiler_params=_params(("arbitrary",)),
        name="conv_norm",
    )(z, z, z, dw_w, dw_b, ln_g, ln_b)


def _conv_layer(h, layer, j, mod, norm_g, p):
    n_rows = h.shape[0]
    tm = _pick(n_rows, MIX_ROW_TILES)
    tn = 512
    z = _fused_matmul(
        name="cv_pw1", n_rows=n_rows, k_dim=D_MODEL, n_out=D_MODEL, tm=tm, tn=tn,
        pro_args=(h, norm_g, mod), pro_specs=_norm_specs(layer, tm, single_buffer_rows=True),
        pro_fn=lambda i, a_ref, h_ref, g_ref, m_ref: _norm_mod_into(a_ref, i, tm, h_ref, g_ref, m_ref, 0),
        w=p['cv_w_pw1'], w_layer=j, w_col_offsets=(0, D_MODEL),
        epi_args=(), epi_specs=(), epi_fn=lambda i, jj, ys: ys[0] * _sigmoid(ys[1]), out_dtype=F32)
    zc = _conv_norm(z, p['cv_dw_w'], p['cv_dw_b'], p['cv_ln_g'], p['cv_ln_b'], j)

    def copy_into(i, a_ref, z_ref):
        a_ref[...] = z_ref[...]

    return _fused_matmul(
        name="cv_pw2", n_rows=n_rows, k_dim=D_MODEL, n_out=D_MODEL, tm=tm, tn=tn,
        pro_args=(zc,), pro_specs=[pl.BlockSpec((tm, D_MODEL), lambda i, jj: (i, 0))],
        pro_fn=copy_into,
        w=p['cv_w_pw2'], w_layer=j, w_col_offsets=(0,),
        epi_args=(h, mod), epi_specs=_res_specs(layer, tm, tn),
        epi_fn=lambda i, jj, ys, h_ref, m_ref: _gated_residual(i, tm, 2, ys[0], h_ref, m_ref),
        out_dtype=F32)


SLAB_RANK1, SLAB_RANK2, SLAB_EXP1, SLAB_EXP2, SLAB_W1, SLAB_W2 = range(6)


def _router(h, layer, k, mod, norm_g, w_router):
    n_rows = h.shape[0]
    tm = _pick(n_rows, (768, 256))
    wr = jnp.pad(w_router[k], ((0, 0), (0, LANES - N_EXPERTS)))

    def body(h_ref, g_ref, m_ref, wr_ref, hn_ref, slab_ref, cnt_ref, base_ref):
        i = pl.program_id(0)

        @pl.when(i == 0)
        def _():
            base_ref[...] = jnp.zeros_like(base_ref)

        hn = _norm_mod(i, tm, h_ref, g_ref, m_ref, 3)
        hn_ref[...] = hn
        logits = jnp.dot(hn, wr_ref[...], precision=HIGHEST, preferred_element_type=F32)
        lane = lax.broadcasted_iota(jnp.int32, (tm, LANES), 1).astype(F32)
        neg = jnp.float32(-jnp.inf)
        lg = jnp.where(lane < N_EXPERTS, logits, neg)
        m1 = jnp.max(lg, axis=1, keepdims=True)
        i1 = jnp.min(jnp.where(lg == m1, lane, float(LANES)), axis=1, keepdims=True)
        lg2 = jnp.where(lane == i1, neg, lg)
        m2 = jnp.max(lg2, axis=1, keepdims=True)
        i2 = jnp.min(jnp.where(lg2 == m2, lane, float(LANES)), axis=1, keepdims=True)
        e = jnp.exp(m2 - m1)
        w1 = 1.0 / (1.0 + e)
        w2 = e / (1.0 + e)
        onehot = jnp.where(lane == i1, 1.0, 0.0) + jnp.where(lane == i2, 1.0, 0.0)
        tri = jnp.where(lax.broadcasted_iota(jnp.int32, (tm, tm), 1)
                        < lax.broadcasted_iota(jnp.int32, (tm, tm), 0), 1.0, 0.0).astype(BF16)
        base = base_ref[0:1, :]
        excl = jnp.dot(tri, onehot.astype(BF16), preferred_element_type=F32) + base
        r1 = jnp.sum(jnp.where(lane == i1, excl, 0.0), axis=1, keepdims=True)
        r2 = jnp.sum(jnp.where(lane == i2, excl, 0.0), axis=1, keepdims=True)
        total = base + jnp.sum(onehot, axis=0, keepdims=True)
        base_ref[...] = jnp.broadcast_to(total, base_ref.shape)
        cnt_ref[...] = jnp.broadcast_to(total, cnt_ref.shape)
        slab = jnp.zeros((tm, LANES), F32)
        for ln, val in ((SLAB_RANK1, r1), (SLAB_RANK2, r2), (SLAB_EXP1, i1), (SLAB_EXP2, i2),
                        (SLAB_W1, w1), (SLAB_W2, w2)):
            slab = jnp.where(lane == float(ln), val, slab)
        slab_ref[...] = slab

    return pl.pallas_call(
        body,
        grid=(n_rows // tm,),
        in_specs=[
            pl.BlockSpec((tm, D_MODEL), lambda i: (i, 0)),
            pl.BlockSpec((None, 1, D_MODEL), lambda i: (layer, 0, 0)),
            pl.BlockSpec((None, 2, N_MOD, D_MODEL), lambda i: (layer, 0, 0, 0)),
            pl.BlockSpec((D_MODEL, LANES), lambda i: (0, 0)),
        ],
        out_specs=[pl.BlockSpec((tm, D_MODEL), lambda i: (i, 0)),
                   pl.BlockSpec((tm, LANES), lambda i: (i, 0)),
                   pl.BlockSpec((8, LANES), lambda i: (0, 0))],
        out_shape=[jax.ShapeDtypeStruct((n_rows, D_MODEL), F32),
                   jax.ShapeDtypeStruct((n_rows, LANES), F32),
                   jax.ShapeDtypeStruct((8, LANES), F32)],
        scratch_shapes=[pltpu.VMEM((8, LANES), F32)],
        compiler_params=_params(("arbitrary",)),
        name="router",
    )(h, norm_g, mod, wr)


def _routing_tables(slab, counts, tb, nblk):
    cnt = counts[0, :N_EXPERTS].astype(jnp.int32)
    padded = ((cnt + tb - 1) // tb) * tb
    ends = jnp.cumsum(padded)
    starts = ends - padded
    pos1 = starts[slab[:, SLAB_EXP1].astype(jnp.int32)] + slab[:, SLAB_RANK1].astype(jnp.int32)
    pos2 = starts[slab[:, SLAB_EXP2].astype(jnp.int32)] + slab[:, SLAB_RANK2].astype(jnp.int32)
    nvalid = ends[-1] // tb
    blk = jnp.minimum(jnp.arange(nblk, dtype=jnp.int32), nvalid - 1)
    blk_expert = jnp.minimum(jnp.searchsorted(ends, blk * tb, side='right'), N_EXPERTS - 1).astype(jnp.int32)
    left = cnt[blk_expert] - (blk * tb - starts[blk_expert])
    blk_rows = ((jnp.clip(left, 0, tb) + MOE_SUB - 1) // MOE_SUB) * MOE_SUB
    return pos1, pos2, blk_expert, blk_rows.astype(jnp.int32), nvalid.reshape(1).astype(jnp.int32)


def _dispatch(hn, pos1, pos2, n_slots):
    n_rows = hn.shape[0]
    tm = CTX_LEN

    def body(p1_ref, p2_ref, hn_ref, init_ref, xs_ref, sem):
        i = pl.program_id(0)

        def row_copy(t, p):
            return pltpu.make_async_copy(hn_ref.at[pl.ds(t, 1)], xs_ref.at[pl.ds(p, 1)], sem)

        def issue(t, carry):
            row_copy(t, p1_ref[i * tm + t]).start()
            row_copy(t, p2_ref[i * tm + t]).start()
            return carry

        def drain(t, carry):
            row_copy(0, 0).wait()
            row_copy(0, 0).wait()
            return carry

        lax.fori_loop(0, tm, issue, 0, unroll=8)
        lax.fori_loop(0, tm, drain, 0, unroll=8)

    return pl.pallas_call(
        body,
        grid_spec=pltpu.PrefetchScalarGridSpec(
            num_scalar_prefetch=2,
            grid=(n_rows // tm,),
            in_specs=[pl.BlockSpec((tm, D_MODEL), lambda i, p1, p2: (i, 0)),
                      pl.BlockSpec(memory_space=pl.ANY)],
            out_specs=pl.BlockSpec(memory_space=pl.ANY),
            scratch_shapes=[pltpu.SemaphoreType.DMA],
        ),
        out_shape=jax.ShapeDtypeStruct((n_slots, D_MODEL), F32),
        input_output_aliases={3: 0},
        compiler_params=_params(("arbitrary",)),
        name="moe_dispatch",
    )(pos1, pos2, hn, jnp.zeros((n_slots, D_MODEL), F32))


def _swiglu_chunk(a, wg_ref, wu_ref, wd_ref):
    g = jnp.dot(a, wg_ref[...].astype(BF16), preferred_element_type=F32)
    u = jnp.dot(a, wu_ref[...].astype(BF16), preferred_element_type=F32)
    return jnp.dot((_silu(g) * u).astype(BF16), wd_ref[...].astype(BF16), preferred_element_type=F32)


def _moe_ffn(xs, blk_expert, blk_rows, nvalid, w_gate, w_up, w_down, e_base, tb):
    n_slots = xs.shape[0]
    d_ff = w_gate.shape[-1]
    tf = FFN_CHUNK
    nf = d_ff // tf

    def body(be_ref, br_ref, nv_ref, x_ref, wg_ref, wu_ref, wd_ref, out_ref, a_ref):
        b = pl.program_id(0)
        f = pl.program_id(1)
        used = b < nv_ref[0]

        @pl.when(jnp.logical_and(used, f == 0))
        def _():
            a_ref[...] = x_ref[...].astype(BF16)
            out_ref[...] = jnp.zeros_like(out_ref)

        for n_rows_used in range(MOE_SUB, tb + 1, MOE_SUB):
            @pl.when(jnp.logical_and(used, br_ref[b] == n_rows_used))
            def _():
                rows = slice(0, n_rows_used)
                out_ref[rows, :] += _swiglu_chunk(a_ref[rows, :], wg_ref, wu_ref, wd_ref)

        @pl.when(jnp.logical_and(jnp.logical_not(used), f == 0))
        def _():
            out_ref[...] = jnp.zeros_like(out_ref)

    def f_idx(b, f, nv):
        return jnp.where(b < nv[0], f, nf - 1)

    once = pl.Buffered(1)
    rows = lambda b, f, be, br, nv: (jnp.minimum(b, nv[0] - 1), 0)
    return pl.pallas_call(
        body,
        grid_spec=pltpu.PrefetchScalarGridSpec(
            num_scalar_prefetch=3,
            grid=(n_slots // tb, nf),
            in_specs=[
                pl.BlockSpec((tb, D_MODEL), rows, pipeline_mode=once),
                pl.BlockSpec((None, D_MODEL, tf), lambda b, f, be, br, nv: (e_base + be[b], 0, f_idx(b, f, nv))),
                pl.BlockSpec((None, D_MODEL, tf), lambda b, f, be, br, nv: (e_base + be[b], 0, f_idx(b, f, nv))),
                pl.BlockSpec((None, tf, D_MODEL), lambda b, f, be, br, nv: (e_base + be[b], f_idx(b, f, nv), 0)),
            ],
            out_specs=pl.BlockSpec((tb, D_MODEL), lambda b, f, be, br, nv: (b, 0), pipeline_mode=once),
            scratch_shapes=[pltpu.VMEM((tb, D_MODEL), BF16)],
        ),
        out_shape=jax.ShapeDtypeStruct((n_slots, D_MODEL), F32),
        compiler_params=_params(("arbitrary", "arbitrary")),
        name="moe_ffn",
    )(blk_expert, blk_rows, nvalid, xs, w_gate, w_up, w_down)


def _combine(h, ys, slab, pos1, pos2, layer, mod):
    n_rows = h.shape[0]
    tm = CTX_LEN

    def body(p1_ref, p2_ref, h_ref, slab_ref, m_ref, ys_ref, out_ref, y1_ref, y2_ref, sem):
        i = pl.program_id(0)

        def row_copy(p, dst_ref, t):
            return pltpu.make_async_copy(ys_ref.at[pl.ds(p, 1)], dst_ref.at[pl.ds(t, 1)], sem)

        def issue(t, carry):
            row_copy(p1_ref[i * tm + t], y1_ref, t).start()
            row_copy(p2_ref[i * tm + t], y2_ref, t).start()
            return carry

        def drain(t, carry):
            row_copy(0, y1_ref, 0).wait()
            row_copy(0, y2_ref, 0).wait()
            return carry

        lax.fori_loop(0, tm, issue, 0, unroll=8)
        lax.fori_loop(0, tm, drain, 0, unroll=8)
        w1 = slab_ref[:, SLAB_W1:SLAB_W1 + 1]
        w2 = slab_ref[:, SLAB_W2:SLAB_W2 + 1]
        moe = w1 * y1_ref[...] + w2 * y2_ref[...]
        out_ref[...] = h_ref[...] + _mod_row(m_ref, 5, _is_ctx(i, tm)) * moe

    return pl.pallas_call(
        body,
        grid_spec=pltpu.PrefetchScalarGridSpec(
            num_scalar_prefetch=2,
            grid=(n_rows // tm,),
            in_specs=[
                pl.BlockSpec((tm, D_MODEL), lambda i, p1, p2: (i, 0)),
                pl.BlockSpec((tm, LANES), lambda i, p1, p2: (i, 0)),
                pl.BlockSpec((None, 2, N_MOD, D_MODEL), lambda i, p1, p2: (layer, 0, 0, 0)),
                pl.BlockSpec(memory_space=pl.ANY),
            ],
            out_specs=pl.BlockSpec((tm, D_MODEL), lambda i, p1, p2: (i, 0)),
            scratch_shapes=[pltpu.VMEM((tm, D_MODEL), F32), pltpu.VMEM((tm, D_MODEL), F32),
                            pltpu.SemaphoreType.DMA],
        ),
        out_shape=jax.ShapeDtypeStruct((n_rows, D_MODEL), F32),
        compiler_params=_params(("arbitrary",)),
        name="moe_combine",
    )(pos1, pos2, h, slab, mod, ys)


def _moe_layer(h, layer, k, mod, norm_g, w_router, w_gate, w_up, w_down):
    n_rows = h.shape[0]
    d_ff = w_gate.shape[-1]
    tb = 1024 if n_rows >= 4096 else 256
    nblk = -(-(TOP_K * n_rows + N_EXPERTS * (tb - 1)) // tb)
    hn, slab, counts = _router(h, layer, k, mod, norm_g, w_router)
    pos1, pos2, blk_expert, blk_rows, nvalid = _routing_tables(slab, counts, tb, nblk)
    xs = _dispatch(hn, pos1, pos2, nblk * tb)
    ys = _moe_ffn(xs, blk_expert, blk_rows, nvalid, w_gate.reshape(-1, D_MODEL, d_ff),
                  w_up.reshape(-1, D_MODEL, d_ff), w_down.reshape(-1, d_ff, D_MODEL), k * N_EXPERTS, tb)
    return _combine(h, ys, slab, pos1, pos2, layer, mod)


def _ffn(h, layer, mod, norm_g, w_gate, w_up, w_down, k):
    n_rows = h.shape[0]
    d_ff = w_gate.shape[-1]
    tm = _pick(n_rows, (768, 256))
    tf = FFN_CHUNK
    nf = d_ff // tf

    def body(h_ref, g_ref, m_ref, wg_ref, wu_ref, wd_ref, out_ref, a_ref):
        i = pl.program_id(0)
        f = pl.program_id(1)

        @pl.when(f == 0)
        def _():
            _norm_mod_into(a_ref, i, tm, h_ref, g_ref, m_ref, 3)
            out_ref[...] = jnp.zeros_like(out_ref)

        out_ref[...] += _swiglu_chunk(a_ref[...], wg_ref, wu_ref, wd_ref)

        @pl.when(f == nf - 1)
        def _():
            def chunk(r, carry):
                rows = pl.ds(pl.multiple_of(r * MOD_CHUNK, MOD_CHUNK), MOD_CHUNK)
                m = jnp.where(i * tm + r * MOD_CHUNK < CTX_LEN, 1, 0)
                out_ref[rows, :] = h_ref[rows, :] + m_ref[m, pl.ds(5, 1), :] * out_ref[rows, :]
                return carry

            lax.fori_loop(0, tm // MOD_CHUNK, chunk, 0)

    once = pl.Buffered(1)
    return pl.pallas_call(
        body,
        grid=(n_rows // tm, nf),
        in_specs=[
            pl.BlockSpec((tm, D_MODEL), lambda i, f: (i, 0), pipeline_mode=once),
            pl.BlockSpec((None, 1, D_MODEL), lambda i, f: (layer, 0, 0)),
            pl.BlockSpec((None, 2, N_MOD, D_MODEL), lambda i, f: (layer, 0, 0, 0)),
            pl.BlockSpec((None, D_MODEL, tf), lambda i, f: (k, 0, f)),
            pl.BlockSpec((None, D_MODEL, tf), lambda i, f: (k, 0, f)),
            pl.BlockSpec((None, tf, D_MODEL), lambda i, f: (k, f, 0)),
        ],
        out_specs=pl.BlockSpec((tm, D_MODEL), lambda i, f: (i, 0), pipeline_mode=once),
        out_shape=jax.ShapeDtypeStruct((n_rows, D_MODEL), F32),
        scratch_shapes=[pltpu.VMEM((tm, D_MODEL), BF16)],
        compiler_params=_params(("arbitrary", "arbitrary")),
        name="ffn",
    )(h, norm_g, mod, w_gate, w_up, w_down)


def _final_norm(h, final_g):
    n_rows = h.shape[0]
    tm = CTX_LEN

    def body(h_ref, g_ref, out_ref):
        x = h_ref[...]
        out_ref[...] = x * lax.rsqrt(jnp.mean(x * x, axis=-1, keepdims=True) + EPS) * g_ref[...]

    return pl.pallas_call(
        body,
        grid=((n_rows - CTX_LEN) // tm,),
        in_specs=[pl.BlockSpec((tm, D_MODEL), lambda i: (i + 1, 0)),
                  pl.BlockSpec((1, D_MODEL), lambda i: (0, 0))],
        out_specs=pl.BlockSpec((tm, D_MODEL), lambda i: (i, 0)),
        out_shape=jax.ShapeDtypeStruct((n_rows - CTX_LEN, D_MODEL), F32),
        compiler_params=_params(("arbitrary",)),
        name="final_norm",
    )(h, final_g.reshape(1, D_MODEL))


def _sincos_tables(rows):
    quarter = D_MODEL // 4
    omega = 1.0 / (POS_BASE ** (jnp.arange(quarter, dtype=F32) / quarter))

    def emb(pos):
        ang = pos[:, None] * omega[None, :]
        return jnp.concatenate([jnp.sin(ang), jnp.cos(ang)], axis=-1)

    return emb(jnp.arange(rows, dtype=F32)), emb(jnp.arange(GRID_W, dtype=F32))


def kernel(x, c, ctx, c_ctx, ada_w, ada_b, norm_mix_g, norm_ffn_g, final_g, s5_w_in, s5_lam_re, s5_lam_im, s5_log_dt, s5_b_re, s5_b_im, s5_c_re, s5_c_im, s5_d, s5_w_out, sg_w_in, sg_ln_g, sg_ln_b, sg_w_s, sg_b_s, sg_w_out, cv_w_pw1, cv_dw_w, cv_dw_b, cv_ln_g, cv_ln_b, cv_w_pw2, ff_w_gate, ff_w_up, ff_w_down, moe_w_router, moe_w_gate, moe_w_up, moe_w_down):
    bsz, seq, dim = x.shape
    assert bsz == 1 and dim == D_MODEL and ctx.shape == (1, CTX_LEN, D_MODEL)
    p = dict(s5_w_in=s5_w_in, s5_lam_re=s5_lam_re, s5_lam_im=s5_lam_im, s5_log_dt=s5_log_dt,
             s5_b_re=s5_b_re, s5_b_im=s5_b_im, s5_c_re=s5_c_re, s5_c_im=s5_c_im, s5_d=s5_d,
             s5_w_out=s5_w_out, sg_w_in=sg_w_in, sg_ln_g=sg_ln_g, sg_ln_b=sg_ln_b, sg_w_s=sg_w_s,
             sg_b_s=sg_b_s, sg_w_out=sg_w_out, cv_w_pw1=cv_w_pw1, cv_dw_w=cv_dw_w, cv_dw_b=cv_dw_b,
             cv_ln_g=cv_ln_g, cv_ln_b=cv_ln_b, cv_w_pw2=cv_w_pw2)
    row_tab, col_tab = _sincos_tables(seq // GRID_W)
    h = _embed(x[0], ctx[0], row_tab, col_tab)
    cond8 = jnp.concatenate([c, c_ctx[None, :], jnp.zeros((6, D_MODEL), F32)], axis=0)
    mod = _modulation(cond8, ada_w, ada_b)[:, :2, :].reshape(DEPTH, 2, N_MOD, D_MODEL)
    g_mix = norm_mix_g.reshape(DEPTH, 1, D_MODEL)
    g_ffn = norm_ffn_g.reshape(DEPTH, 1, D_MODEL)
    for layer in range(DEPTH):
        kind, j = layer % N_MIXERS, layer // N_MIXERS
        if kind == 0:
            h = _s5_layer(h, layer, j, mod, g_mix, p)
        elif kind == 1:
            h = _sg_layer(h, layer, j, mod, g_mix, p)
        else:
            h = _conv_layer(h, layer, j, mod, g_mix, p)
        k = layer // 2
        if layer % 2 == 0:
            h = _ffn(h, layer, mod, g_ffn, ff_w_gate, ff_w_up, ff_w_down, k)
        else:
            h = _moe_layer(h, layer, k, mod, g_ffn, moe_w_router, moe_w_gate, moe_w_up, moe_w_down)
    return _final_norm(h, final_g)[None]
```

```python
import functools
import math

import jax
import jax.numpy as jnp
from jax import lax
from jax.experimental import pallas as pl
from jax.experimental.pallas import tpu as pltpu

F32 = jnp.float32
BF16 = jnp.bfloat16
HIGHEST = lax.Precision.HIGHEST

D_MODEL = 2048
DEPTH = 4
GRID_W = 64
CTX_LEN = 256
POS_BASE = 10000.0
N_MIXERS = 3
N_MOD = 6
EPS = 1e-6
S5_GROUP = 16
S5_GROUPS = D_MODEL // S5_GROUP
S5_STATE = 64
S5_CHUNK = 16
S5_CW = S5_CHUNK * S5_GROUP
SG_CHUNK = 128
SG_HEADS = 8
SG_HEAD_DIM = D_MODEL // SG_HEADS
CONV_WIDTH = 31
CONV_HALO = 16
N_EXPERTS = 8
TOP_K = 2
LANES = 128
SUBLANES = 8
S5_LB_GROUPS = LANES // S5_GROUP
MOD_CHUNK = 128
FFN_CHUNK = 512
MOE_SUB = 256
MIX_ROW_TILES = (1408, 768, 256)
VMEM_LIMIT = 56 * 1024 * 1024


def _pick(n, candidates):
    for c in candidates:
        if n % c == 0:
            return c
    raise ValueError(f"no tile in {candidates} divides {n}")


def _params(sem):
    return pltpu.CompilerParams(dimension_semantics=sem, vmem_limit_bytes=VMEM_LIMIT)


def _sigmoid(x):
    return 1.0 / (1.0 + jnp.exp(-x))


def _silu(x):
    return x * _sigmoid(x)


def _gelu_tanh(x):
    return 0.5 * x * (1.0 + jnp.tanh(math.sqrt(2.0 / math.pi) * (x + 0.044715 * (x * x * x))))


def _is_ctx(i, tm):
    rows = i * tm + lax.broadcasted_iota(jnp.int32, (tm, 1), 0)
    return rows < CTX_LEN


def _mod_row(mod_ref, k, is_ctx):
    return jnp.where(is_ctx, mod_ref[1, k:k + 1, :], mod_ref[0, k:k + 1, :])


def _norm_mod_rows(row0, h, g, mod_ref, k_shift):
    inv = lax.rsqrt(jnp.mean(h * h, axis=-1, keepdims=True) + EPS)
    hn = (h * inv) * g
    is_ctx = (row0 + lax.broadcasted_iota(jnp.int32, (h.shape[0], 1), 0)) < CTX_LEN
    return hn * (1.0 + _mod_row(mod_ref, k_shift + 1, is_ctx)) + _mod_row(mod_ref, k_shift, is_ctx)


def _norm_mod(i, tm, h_ref, g_ref, mod_ref, k_shift):
    return _norm_mod_rows(i * tm, h_ref[...], g_ref[...], mod_ref, k_shift)


def _norm_mod_into(a_ref, i, tm, h_ref, g_ref, mod_ref, k_shift):
    assert tm % MOD_CHUNK == 0 and CTX_LEN % MOD_CHUNK == 0
    g = g_ref[...]

    def chunk(r, carry):
        rows = pl.ds(pl.multiple_of(r * MOD_CHUNK, MOD_CHUNK), MOD_CHUNK)
        m = jnp.where(i * tm + r * MOD_CHUNK < CTX_LEN, 1, 0)
        h = h_ref[rows, :]
        inv = lax.rsqrt(jnp.mean(h * h, axis=-1, keepdims=True) + EPS)
        scale = mod_ref[m, pl.ds(k_shift + 1, 1), :]
        shift = mod_ref[m, pl.ds(k_shift, 1), :]
        a_ref[rows, :] = (((h * inv) * g) * (1.0 + scale) + shift).astype(a_ref.dtype)
        return carry

    lax.fori_loop(0, tm // MOD_CHUNK, chunk, 0)


def _layer_norm(v, g, b):
    mu = jnp.mean(v, axis=-1, keepdims=True)
    vc = v - mu
    var = jnp.mean(vc * vc, axis=-1, keepdims=True)
    return vc * lax.rsqrt(var + EPS) * g + b


def _fused_matmul(*, name, n_rows, k_dim, n_out, tm, tn, pro_args, pro_specs, pro_fn,
                  w, w_layer, w_col_offsets, epi_args, epi_specs, epi_fn, out_dtype):
    n_pro, n_w, n_epi = len(pro_args), len(w_col_offsets), len(epi_args)
    nj = n_out // tn

    def body(*refs):
        pro_refs = refs[:n_pro]
        w_refs = refs[n_pro:n_pro + n_w]
        epi_refs = refs[n_pro + n_w:n_pro + n_w + n_epi]
        out_ref, a_ref = refs[-2], refs[-1]
        i = pl.program_id(0)
        j = pl.program_id(1)

        @pl.when(j == 0)
        def _():
            pro_fn(i, a_ref, *pro_refs)

        a = a_ref[...]
        ys = [jnp.dot(a, wr[...].astype(BF16), preferred_element_type=F32) for wr in w_refs]
        out_ref[...] = epi_fn(i, j, ys, *epi_refs).astype(out_dtype)

    w_specs = [
        pl.BlockSpec((None, k_dim, tn), functools.partial(lambda i, j, o: (w_layer, 0, j + o), o=off // tn))
        for off in w_col_offsets
    ]
    return pl.pallas_call(
        body,
        grid=(n_rows // tm, nj),
        in_specs=list(pro_specs) + w_specs + list(epi_specs),
        out_specs=pl.BlockSpec((tm, tn), lambda i, j: (i, j)),
        out_shape=jax.ShapeDtypeStruct((n_rows, n_out), out_dtype),
        scratch_shapes=[pltpu.VMEM((tm, k_dim), BF16)],
        compiler_params=_params(("arbitrary", "arbitrary")),
        name=name,
    )(*pro_args, *([w] * n_w), *epi_args)


def _norm_specs(layer, tm, single_buffer_rows=False):
    return [
        pl.BlockSpec((tm, D_MODEL), lambda i, j: (i, 0),
                     pipeline_mode=pl.Buffered(1 if single_buffer_rows else 2)),
        pl.BlockSpec((None, 1, D_MODEL), lambda i, j: (layer, 0, 0)),
        pl.BlockSpec((None, 2, N_MOD, D_MODEL), lambda i, j: (layer, 0, 0, 0)),
    ]


def _res_specs(layer, tm, tn):
    return [
        pl.BlockSpec((tm, tn), lambda i, j: (i, j)),
        pl.BlockSpec((None, 2, N_MOD, tn), lambda i, j: (layer, 0, 0, j)),
    ]


def _gated_residual(i, tm, k_gate, y, h_ref, mod_ref):
    return h_ref[...] + _mod_row(mod_ref, k_gate, _is_ctx(i, tm)) * y


def _embed(x2, ctx2, row_tab, col_tab):
    seq = x2.shape[0]
    tm = 4 * GRID_W
    half = D_MODEL // 2
    n_tiles = (CTX_LEN + seq) // tm
    rows_per_tile = tm // GRID_W
    row_tab3 = row_tab.reshape(seq // tm, rows_per_tile, half)

    def body(ctx_ref, x_ref, row_ref, col_ref, out_ref):
        i = pl.program_id(0)

        @pl.when(i == 0)
        def _():
            out_ref[...] = ctx_ref[...]

        @pl.when(i > 0)
        def _():
            rt = row_ref[0]
            row_part = jnp.concatenate(
                [jnp.broadcast_to(rt[r:r + 1, :], (GRID_W, half)) for r in range(rows_per_tile)], axis=0)
            col_part = jnp.concatenate([col_ref[...]] * rows_per_tile, axis=0)
            out_ref[:, :half] = x_ref[:, :half] + row_part
            out_ref[:, half:] = x_ref[:, half:] + col_part

    assert CTX_LEN == tm
    return pl.pallas_call(
        body,
        grid=(n_tiles,),
        in_specs=[
            pl.BlockSpec((tm, D_MODEL), lambda i: (0, 0)),
            pl.BlockSpec((tm, D_MODEL), lambda i: (jnp.maximum(i - 1, 0), 0)),
            pl.BlockSpec((1, rows_per_tile, half), lambda i: (jnp.maximum(i - 1, 0), 0, 0)),
            pl.BlockSpec((GRID_W, half), lambda i: (0, 0)),
        ],
        out_specs=pl.BlockSpec((tm, D_MODEL), lambda i: (i, 0)),
        out_shape=jax.ShapeDtypeStruct((CTX_LEN + seq, D_MODEL), F32),
        compiler_params=_params(("arbitrary",)),
        name="embed",
    )(ctx2, x2, row_tab3, col_tab)


def _modulation(cond8, ada_w, ada_b):
    n = N_MOD * D_MODEL
    tn = 1024

    def body(c_ref, w_ref, b_ref, out_ref):
        a = _silu(c_ref[...]).astype(BF16)
        out_ref[...] = jnp.dot(a, w_ref[...].astype(BF16), preferred_element_type=F32) + b_ref[...]

    return pl.pallas_call(
        body,
        grid=(DEPTH, n // tn),
        in_specs=[
            pl.BlockSpec((8, D_MODEL), lambda l, j: (0, 0)),
            pl.BlockSpec((None, D_MODEL, tn), lambda l, j: (l, 0, j)),
            pl.BlockSpec((None, 1, tn), lambda l, j: (l, 0, j)),
        ],
        out_specs=pl.BlockSpec((None, 8, tn), lambda l, j: (l, 0, j)),
        out_shape=jax.ShapeDtypeStruct((DEPTH, 8, n), F32),
        compiler_params=_params(("arbitrary", "arbitrary")),
        name="modulation",
    )(cond8, ada_w, ada_b.reshape(DEPTH, 1, n))


def _s5_operators(lam_re, lam_im, log_dt, bt_re, bt_im, c_re, c_im):
    G, T, H, P, CW = S5_GROUPS, S5_CHUNK, S5_GROUP, S5_STATE, S5_CW

    def body(lre_ref, lim_ref, ldt_ref, btr_ref, bti_ref, cr_ref, ci_ref,
             kd_ref, we_ref, wo_ref, at_ref):
        slot = pl.program_id(0) % S5_LB_GROUPS
        row_blk = lax.broadcasted_iota(jnp.int32, (LANES, 2 * P), 0) // H
        kn, we_parts, wo_parts, at_parts = [], [], [], []
        for d in range(2):
            lr = jnp.minimum(lre_ref[d, 0], -1e-4)
            li = lim_ref[d, 0]
            dt = jnp.exp(ldt_ref[d, 0])
            mag = jnp.exp(lr * dt)
            ar = mag * jnp.cos(li * dt)
            ai = mag * jnp.sin(li * dt)
            xr = ar - 1.0
            xi = ai
            den = lr * lr + li * li
            kr = (xr * lr + xi * li) / den
            ki = (xi * lr - xr * li) / den
            btr, bti = btr_ref[d, 0], bti_ref[d, 0]
            bbr = kr * btr - ki * bti
            bbi = kr * bti + ki * btr
            cr, ci = cr_ref[d, 0], ci_ref[d, 0]
            pr = [jnp.ones((1, P), F32)]
            pi = [jnp.zeros((1, P), F32)]
            for _ in range(T):
                pr.append(pr[-1] * ar - pi[-1] * ai)
                pi.append(pr[-2] * ai + pi[-1] * ar)
            ba_re = [bbr * pr[l] - bbi * pi[l] for l in range(T)]
            ba_im = [bbr * pi[l] + bbi * pr[l] for l in range(T)]
            rising = list(range(T))
            falling = rising[::-1]
            e_order = falling if d == 0 else rising
            e_re = jnp.concatenate([ba_re[l] for l in e_order], axis=0)
            e_im = jnp.concatenate([ba_im[l] for l in e_order], axis=0)
            k_order = rising if d == 0 else falling
            k_lhs = jnp.concatenate([jnp.concatenate([ba_re[l] for l in k_order], axis=0),
                                     jnp.concatenate([ba_im[l] for l in k_order], axis=0)], axis=1)
            cmat = jnp.concatenate([cr, -ci], axis=1)
            cmat_slot = jnp.where(row_blk == slot, jnp.concatenate([cmat] * S5_LB_GROUPS, axis=0), 0.0)
            kn.append(lax.dot_general(k_lhs, cmat_slot, (((1,), (1,)), ((), ())), precision=HIGHEST,
                                      preferred_element_type=F32))
            lags = [t + 1 for t in range(T)] if d == 0 else [T - t for t in range(T)]
            ca_re = jnp.concatenate([cr * pr[l] - ci * pi[l] for l in lags], axis=0)
            ca_im = jnp.concatenate([cr * pi[l] + ci * pr[l] for l in lags], axis=0)
            we_parts.append((e_re, e_im))
            wo_parts.append((ca_re, -ca_im))
            at_parts.append((pr[T], pi[T]))
        kf, kb = kn
        kd_ref[0] = jnp.concatenate(
            [kb[:CW - H], kb[CW - H:] + kf[:H], kf[H:], jnp.zeros((H, LANES), F32)], axis=0).astype(BF16)
        we_ref[0] = jnp.concatenate(
            [we_parts[0][0], we_parts[1][0], we_parts[0][1], we_parts[1][1]], axis=1).astype(BF16)
        wo_ref[0] = jnp.concatenate(
            [wo_parts[0][0], wo_parts[1][0], wo_parts[0][1], wo_parts[1][1]], axis=1).astype(BF16)
        at_ref[0] = jnp.concatenate(
            [at_parts[0][0], at_parts[1][0], at_parts[0][1], at_parts[1][1]], axis=1)

    vec_spec = pl.BlockSpec((2, 1, 1, P), lambda g: (0, g, 0, 0))
    mat_spec = pl.BlockSpec((2, 1, H, P), lambda g: (0, g, 0, 0))
    op_spec = pl.BlockSpec((1, CW, CW), lambda g: (g, 0, 0))
    op_shape = jax.ShapeDtypeStruct((G, CW, CW), BF16)
    return pl.pallas_call(
        body,
        grid=(G,),
        in_specs=[vec_spec, vec_spec, pl.BlockSpec((2, 1, 1, 1), lambda g: (0, g, 0, 0)),
                  mat_spec, mat_spec, mat_spec, mat_spec],
        out_specs=[pl.BlockSpec((1, 2 * CW, LANES), lambda g: (g, 0, 0)), op_spec, op_spec,
                   pl.BlockSpec((1, 1, CW), lambda g: (g, 0, 0))],
        out_shape=[jax.ShapeDtypeStruct((G, 2 * CW, LANES), BF16), op_shape, op_shape,
                   jax.ShapeDtypeStruct((G, 1, CW), F32)],
        compiler_params=_params(("arbitrary",)),
        name="s5_operators",
    )(lam_re.reshape(2, G, 1, P), lam_im.reshape(2, G, 1, P), log_dt.reshape(2, G, 1, 1),
      bt_re, bt_im, c_re, c_im)


def _s5_gather_chunks(u2_ref, x_ref, sems, lb):
    def copies(lane_block, slot):
        return [pltpu.make_async_copy(u2_ref.at[:, pl.ds(s * D_MODEL + lane_block * LANES, LANES)],
                                      x_ref.at[slot, :, pl.ds(s * LANES, LANES)], sems.at[slot])
                for s in range(S5_CHUNK)]

    slot = lb % 2

    @pl.when(lb == 0)
    def _():
        for cp in copies(0, 0):
            cp.start()

    @pl.when(lb + 1 < pl.num_programs(0))
    def _():
        for cp in copies(lb + 1, 1 - slot):
            cp.start()

    return copies(lb, slot), slot


def _s5_block_diag(dst_ref, src_ref):
    dst_ref[...] = jnp.zeros_like(dst_ref)
    for g in range(S5_LB_GROUPS):
        for s in range(S5_CHUNK):
            dst_ref[s * LANES + g * S5_GROUP:s * LANES + (g + 1) * S5_GROUP, g * S5_CW:(g + 1) * S5_CW] = (
                src_ref[g, s * S5_GROUP:(s + 1) * S5_GROUP, :])


def _s5_local_state(u2, we):
    nc = u2.shape[0]
    k = S5_CHUNK * LANES
    n = S5_LB_GROUPS * S5_CW

    def body(u2_ref, we_ref, out_ref, x_ref, w_ref, sems):
        lb = pl.program_id(0)
        gathers, slot = _s5_gather_chunks(u2_ref, x_ref, sems, lb)
        _s5_block_diag(w_ref, we_ref)
        for cp in gathers:
            cp.wait()
        states = jnp.dot(x_ref[slot], w_ref[...], preferred_element_type=F32)
        for g in range(S5_LB_GROUPS):
            out_ref[:, g, :] = states[:, g * S5_CW:(g + 1) * S5_CW]

    return pl.pallas_call(
        body,
        grid=(D_MODEL // LANES,),
        in_specs=[pl.BlockSpec(memory_space=pl.ANY),
                  pl.BlockSpec((S5_LB_GROUPS, S5_CW, S5_CW), lambda lb: (lb, 0, 0))],
        out_specs=pl.BlockSpec((nc, S5_LB_GROUPS, S5_CW), lambda lb: (0, lb, 0)),
        out_shape=jax.ShapeDtypeStruct((nc, S5_GROUPS, S5_CW), F32),
        scratch_shapes=[pltpu.VMEM((2, nc, k), BF16), pltpu.VMEM((k, n), BF16), pltpu.SemaphoreType.DMA((2,))],
        compiler_params=_params(("arbitrary",)),
        name="s5_local_state",
    )(u2, we)


def _s5_carry(sloc_t, at):
    nc, G, cw = sloc_t.shape
    cb = CTX_LEN // S5_CHUNK
    nb = nc // cb
    half = cw // 2

    def rev_block(s):
        return jnp.where(s == 0, 0, nb - s)

    def body(sf_ref, sb_ref, at_ref, pf_ref, pb_ref, st_ref):
        s = pl.program_id(0)

        @pl.when(s == 0)
        def _():
            st_ref[...] = jnp.zeros_like(st_ref)

        is_fwd = (lax.broadcasted_iota(jnp.int32, (G, half), 1) < S5_STATE)
        a_re, a_im = at_ref[:, :half], at_ref[:, half:]
        s_re, s_im = st_ref[:, :half], st_ref[:, half:]
        for k in range(cb):
            kb = cb - 1 - k
            pf_ref[k, :, :half] = s_re.astype(BF16)
            pf_ref[k, :, half:] = s_im.astype(BF16)
            pb_ref[kb, :, :half] = s_re.astype(BF16)
            pb_ref[kb, :, half:] = s_im.astype(BF16)
            l_re = jnp.where(is_fwd, sf_ref[k, :, :half], sb_ref[kb, :, :half])
            l_im = jnp.where(is_fwd, sf_ref[k, :, half:], sb_ref[kb, :, half:])
            n_re = a_re * s_re - a_im * s_im + l_re
            n_im = a_re * s_im + a_im * s_re + l_im
            s_re, s_im = n_re, n_im
        st_ref[:, :half] = s_re
        st_ref[:, half:] = s_im

    blk = (cb, G, cw)
    return pl.pallas_call(
        body,
        grid=(nb,),
        in_specs=[pl.BlockSpec(blk, lambda s: (s, 0, 0)),
                  pl.BlockSpec(blk, lambda s: (rev_block(s), 0, 0)),
                  pl.BlockSpec((G, cw), lambda s: (0, 0))],
        out_specs=[pl.BlockSpec(blk, lambda s: (s, 0, 0)),
                   pl.BlockSpec(blk, lambda s: (rev_block(s), 0, 0))],
        out_shape=[jax.ShapeDtypeStruct((nc, G, cw), BF16)] * 2,
        scratch_shapes=[pltpu.VMEM((G, cw), F32)],
        compiler_params=_params(("arbitrary",)),
        name="s5_carry",
    )(sloc_t, sloc_t, at)


def _s5_output(u2, kd, wo, pf, pb, d3, j):
    nc = u2.shape[0]
    T = S5_CHUNK
    k = T * LANES
    n = S5_LB_GROUPS * S5_CW
    nt = (((1,), (1,)), ((), ()))

    def body(u2_ref, kd_ref, wo_ref, pf_ref, pb_ref, d_ref, y2_ref,
             x_ref, toep_ref, w_ref, bd_ref, y_ref, sem_in, sem_out):
        lb = pl.program_id(0)
        gathers, slot = _s5_gather_chunks(u2_ref, x_ref, sem_in, lb)
        for lag in range(2 * T - 1):
            for g in range(S5_LB_GROUPS):
                bd_ref[lag * LANES + g * S5_GROUP:lag * LANES + (g + 1) * S5_GROUP, :] = (
                    kd_ref[g, lag * S5_GROUP:(lag + 1) * S5_GROUP, :])
        for s in range(T):
            for t in range(T):
                lag = t - s + T - 1
                toep_ref[s * LANES:(s + 1) * LANES, t * LANES:(t + 1) * LANES] = (
                    bd_ref[lag * LANES:(lag + 1) * LANES, :])
        _s5_block_diag(w_ref, wo_ref)
        is_fwd = (lax.broadcasted_iota(jnp.int32, (nc, n), 1) % LANES) < S5_STATE
        p = jnp.where(is_fwd, pf_ref[...], pb_ref[...])
        for cp in gathers:
            cp.wait()
        x = x_ref[slot]
        y = jnp.dot(x, toep_ref[...], preferred_element_type=F32)
        y = y + lax.dot_general(p, w_ref[...], nt, preferred_element_type=F32)
        y = y + jnp.concatenate([d_ref[...]] * T, axis=1) * x.astype(F32)
        def scatters(lane_block):
            return [pltpu.make_async_copy(y_ref.at[:, pl.ds(t * LANES, LANES)],
                                          y2_ref.at[:, pl.ds(t * D_MODEL + lane_block * LANES, LANES)], sem_out)
                    for t in range(T)]

        @pl.when(lb > 0)
        def _():
            for cp in scatters(lb - 1):
                cp.wait()

        y_ref[...] = y.astype(BF16)
        for cp in scatters(lb):
            cp.start()

        @pl.when(lb == pl.num_programs(0) - 1)
        def _():
            for cp in scatters(lb):
                cp.wait()

    state = pl.BlockSpec((nc, n), lambda lb: (0, lb))
    return pl.pallas_call(
        body,
        grid=(D_MODEL // LANES,),
        in_specs=[pl.BlockSpec(memory_space=pl.ANY),
                  pl.BlockSpec((S5_LB_GROUPS, 2 * S5_CW, LANES), lambda lb: (lb, 0, 0)),
                  pl.BlockSpec((S5_LB_GROUPS, S5_CW, S5_CW), lambda lb: (lb, 0, 0)),
                  state, state,
                  pl.BlockSpec((None, 1, LANES), lambda lb: (j, 0, lb))],
        out_specs=pl.BlockSpec(memory_space=pl.ANY),
        out_shape=jax.ShapeDtypeStruct((nc, T * D_MODEL), BF16),
        scratch_shapes=[pltpu.VMEM((2, nc, k), BF16), pltpu.VMEM((k, k), BF16), pltpu.VMEM((k, n), BF16),
                        pltpu.VMEM((2 * T * LANES, LANES), BF16), pltpu.VMEM((nc, k), BF16),
                        pltpu.SemaphoreType.DMA((2,)), pltpu.SemaphoreType.DMA],
        compiler_params=_params(("arbitrary",)),
        name="s5_output",
    )(u2, kd, wo, pf, pb, d3)


def _s5_layer(h, layer, j, mod, norm_g, p):
    n_rows = h.shape[0]
    G, T, H = S5_GROUPS, S5_CHUNK, S5_GROUP
    nc = n_rows // T
    tm = _pick(n_rows, MIX_ROW_TILES)
    tn = 512
    u = _fused_matmul(
        name="s5_in", n_rows=n_rows, k_dim=D_MODEL, n_out=D_MODEL, tm=tm, tn=tn,
        pro_args=(h, norm_g, mod), pro_specs=_norm_specs(layer, tm),
        pro_fn=lambda i, a_ref, h_ref, g_ref, m_ref: _norm_mod_into(a_ref, i, tm, h_ref, g_ref, m_ref, 0),
        w=p['s5_w_in'], w_layer=j, w_col_offsets=(0,),
        epi_args=(), epi_specs=(), epi_fn=lambda i, jj, ys: ys[0], out_dtype=BF16)
    u2 = u.reshape(nc, T * D_MODEL)
    kd, we, wo, at = _s5_operators(
        p['s5_lam_re'][j], p['s5_lam_im'][j], p['s5_log_dt'][j],
        jnp.swapaxes(p['s5_b_re'][j], -1, -2), jnp.swapaxes(p['s5_b_im'][j], -1, -2),
        p['s5_c_re'][j], p['s5_c_im'][j])
    sloc = _s5_local_state(u2, we)
    pf, pb = _s5_carry(sloc, at.reshape(G, T * H))
    d3 = p['s5_d'].reshape(-1, 1, D_MODEL)
    y2 = _s5_output(u2, kd, wo, pf.reshape(nc, G * T * H), pb.reshape(nc, G * T * H), d3, j)
    y = y2.reshape(n_rows, D_MODEL)

    def gelu_into(i, a_ref, y_ref):
        a_ref[...] = _gelu_tanh(y_ref[...].astype(F32)).astype(BF16)

    return _fused_matmul(
        name="s5_out", n_rows=n_rows, k_dim=D_MODEL, n_out=D_MODEL, tm=tm, tn=tn,
        pro_args=(y,), pro_specs=[pl.BlockSpec((tm, D_MODEL), lambda i, jj: (i, 0))],
        pro_fn=gelu_into,
        w=p['s5_w_out'], w_layer=j, w_col_offsets=(0, D_MODEL),
        epi_args=(h, mod), epi_specs=_res_specs(layer, tm, tn),
        epi_fn=lambda i, jj, ys, h_ref, m_ref: _gated_residual(
            i, tm, 2, ys[0] * _sigmoid(ys[1]), h_ref, m_ref),
        out_dtype=F32)


def _sg_layer(h, layer, j, mod, norm_g, p):
    n_rows = h.shape[0]
    tm = _pick(n_rows, MIX_ROW_TILES)
    tn = 512
    z = _fused_matmul(
        name="sg_in", n_rows=n_rows, k_dim=D_MODEL, n_out=2 * D_MODEL, tm=tm, tn=tn,
        pro_args=(h, norm_g, mod), pro_specs=_norm_specs(layer, tm),
        pro_fn=lambda i, a_ref, h_ref, g_ref, m_ref: _norm_mod_into(a_ref, i, tm, h_ref, g_ref, m_ref, 0),
        w=p['sg_w_in'], w_layer=j, w_col_offsets=(0,),
        epi_args=(), epi_specs=(), epi_fn=lambda i, jj, ys: _gelu_tanh(ys[0]), out_dtype=BF16)

    def gate(i, a_ref, u_ref, v_ref, g_ref, b_ref, ws_ref, bs_ref):
        ln_g, ln_b = g_ref[...], b_ref[...]

        def chunk(c, carry):
            rows = pl.ds(pl.multiple_of(c * SG_CHUNK, SG_CHUNK), SG_CHUNK)
            v = _layer_norm(v_ref[rows, :].astype(F32), ln_g, ln_b).astype(BF16)
            for hd in range(SG_HEADS):
                cols = slice(hd * SG_HEAD_DIM, (hd + 1) * SG_HEAD_DIM)
                sv = jnp.dot(ws_ref[hd].astype(BF16), v[:, cols], preferred_element_type=F32)
                gated = u_ref[rows, cols].astype(F32) * (sv + bs_ref[:, hd:hd + 1])
                a_ref[rows, cols] = gated.astype(BF16)
            return carry

        lax.fori_loop(0, tm // SG_CHUNK, chunk, 0)

    return _fused_matmul(
        name="sg_out", n_rows=n_rows, k_dim=D_MODEL, n_out=D_MODEL, tm=tm, tn=tn,
        pro_args=(z, z, p['sg_ln_g'], p['sg_ln_b'], p['sg_w_s'], jnp.swapaxes(p['sg_b_s'], -1, -2)),
        pro_specs=[
            pl.BlockSpec((tm, D_MODEL), lambda i, jj: (i, 0)),
            pl.BlockSpec((tm, D_MODEL), lambda i, jj: (i, 1)),
            pl.BlockSpec((1, D_MODEL), lambda i, jj: (j, 0)),
            pl.BlockSpec((1, D_MODEL), lambda i, jj: (j, 0)),
            pl.BlockSpec((None, SG_HEADS, SG_CHUNK, SG_CHUNK), lambda i, jj: (j, 0, 0, 0)),
            pl.BlockSpec((None, SG_CHUNK, SG_HEADS), lambda i, jj: (j, 0, 0)),
        ],
        pro_fn=gate,
        w=p['sg_w_out'], w_layer=j, w_col_offsets=(0,),
        epi_args=(h, mod), epi_specs=_res_specs(layer, tm, tn),
        epi_fn=lambda i, jj, ys, h_ref, m_ref: _gated_residual(i, tm, 2, ys[0], h_ref, m_ref),
        out_dtype=F32)


def _conv_norm(z, dw_w, dw_b, ln_g, ln_b, j):
    n_rows = z.shape[0]
    tm = CTX_LEN
    n_tiles = n_rows // tm
    hb = tm // CONV_HALO
    half = CONV_WIDTH // 2
    rb, cbw = 64, LANES

    def body(prev_ref, cur_ref, next_ref, w_ref, b_ref, g_ref, beta_ref, out_ref, zp_ref, acc_ref):
        i = pl.program_id(0)
        prev_ok = i >= 2
        next_ok = jnp.logical_and(i >= 1, i < n_tiles - 1)
        zp_ref[0:CONV_HALO, :] = jnp.where(prev_ok, prev_ref[...], 0.0)
        zp_ref[CONV_HALO:CONV_HALO + tm, :] = cur_ref[...]
        zp_ref[CONV_HALO + tm:, :] = jnp.where(next_ok, next_ref[...], 0.0)

        def col_block(cb, carry):
            c0 = pl.multiple_of(cb * cbw, cbw)
            w = w_ref[:, pl.ds(c0, cbw)]
            for r in range(tm // rb):
                win = zp_ref[r * rb:r * rb + rb + 2 * CONV_HALO, pl.ds(c0, cbw)]
                span = rb + 2 * CONV_HALO - SUBLANES
                phases = [win[ph:ph + span, :] for ph in range(SUBLANES)]
                acc = jnp.zeros((rb, cbw), F32)
                for k in range(CONV_WIDTH):
                    off = CONV_HALO - half + k
                    base = off - off % SUBLANES
                    acc = acc + phases[off % SUBLANES][base:base + rb, :] * w[k:k + 1, :]
                acc_ref[r * rb:(r + 1) * rb, pl.ds(c0, cbw)] = acc
            return carry

        lax.fori_loop(0, D_MODEL // cbw, col_block, 0)
        y = _layer_norm(acc_ref[...] + b_ref[...], g_ref[...], beta_ref[...])
        out_ref[...] = _silu(y).astype(BF16)

    vec = pl.BlockSpec((1, D_MODEL), lambda i: (j, 0))
    return pl.pallas_call(
        body,
        grid=(n_tiles,),
        in_specs=[
            pl.BlockSpec((CONV_HALO, D_MODEL), lambda i: (jnp.maximum(i * hb - 1, 0), 0)),
            pl.BlockSpec((tm, D_MODEL), lambda i: (i, 0)),
            pl.BlockSpec((CONV_HALO, D_MODEL), lambda i: (jnp.minimum((i + 1) * hb, n_tiles * hb - 1), 0)),
            pl.BlockSpec((None, CONV_WIDTH, D_MODEL), lambda i: (j, 0, 0)),
            vec, vec, vec,
        ],
        out_specs=pl.BlockSpec((tm, D_MODEL), lambda i: (i, 0)),
        out_shape=jax.ShapeDtypeStruct((n_rows, D_MODEL), BF16),
        scratch_shapes=[pltpu.VMEM((tm + 2 * CONV_HALO, D_MODEL), F32), pltpu.VMEM((tm, D_MODEL), F32)],
        compiler_params=_params(("arbitrary",)),
        name="conv_norm",
    )(z, z, z, dw_w, dw_b, ln_g, ln_b)


def _conv_layer(h, layer, j, mod, norm_g, p):
    n_rows = h.shape[0]
    tm = _pick(n_rows, MIX_ROW_TILES)
    tn = 512
    z = _fused_matmul(
        name="cv_pw1", n_rows=n_rows, k_dim=D_MODEL, n_out=D_MODEL, tm=tm, tn=tn,
        pro_args=(h, norm_g, mod), pro_specs=_norm_specs(layer, tm, single_buffer_rows=True),
        pro_fn=lambda i, a_ref, h_ref, g_ref, m_ref: _norm_mod_into(a_ref, i, tm, h_ref, g_ref, m_ref, 0),
        w=p['cv_w_pw1'], w_layer=j, w_col_offsets=(0, D_MODEL),
        epi_args=(), epi_specs=(), epi_fn=lambda i, jj, ys: ys[0] * _sigmoid(ys[1]), out_dtype=F32)
    zc = _conv_norm(z, p['cv_dw_w'], p['cv_dw_b'], p['cv_ln_g'], p['cv_ln_b'], j)

    def copy_into(i, a_ref, z_ref):
        a_ref[...] = z_ref[...]

    return _fused_matmul(
        name="cv_pw2", n_rows=n_rows, k_dim=D_MODEL, n_out=D_MODEL, tm=tm, tn=tn,
        pro_args=(zc,), pro_specs=[pl.BlockSpec((tm, D_MODEL), lambda i, jj: (i, 0))],
        pro_fn=copy_into,
        w=p['cv_w_pw2'], w_layer=j, w_col_offsets=(0,),
        epi_args=(h, mod), epi_specs=_res_specs(layer, tm, tn),
        epi_fn=lambda i, jj, ys, h_ref, m_ref: _gated_residual(i, tm, 2, ys[0], h_ref, m_ref),
        out_dtype=F32)


SLAB_RANK1, SLAB_RANK2, SLAB_EXP1, SLAB_EXP2, SLAB_W1, SLAB_W2 = range(6)


def _router(h, layer, k, mod, norm_g, w_router):
    n_rows = h.shape[0]
    tm = _pick(n_rows, (768, 256))
    wr = jnp.pad(w_router[k], ((0, 0), (0, LANES - N_EXPERTS)))

    def body(h_ref, g_ref, m_ref, wr_ref, hn_ref, slab_ref, cnt_ref, base_ref):
        i = pl.program_id(0)

        @pl.when(i == 0)
        def _():
            base_ref[...] = jnp.zeros_like(base_ref)

        hn = _norm_mod(i, tm, h_ref, g_ref, m_ref, 3)
        hn_ref[...] = hn
        logits = jnp.dot(hn, wr_ref[...], precision=HIGHEST, preferred_element_type=F32)
        lane = lax.broadcasted_iota(jnp.int32, (tm, LANES), 1).astype(F32)
        neg = jnp.float32(-jnp.inf)
        lg = jnp.where(lane < N_EXPERTS, logits, neg)
        m1 = jnp.max(lg, axis=1, keepdims=True)
        i1 = jnp.min(jnp.where(lg == m1, lane, float(LANES)), axis=1, keepdims=True)
        lg2 = jnp.where(lane == i1, neg, lg)
        m2 = jnp.max(lg2, axis=1, keepdims=True)
        i2 = jnp.min(jnp.where(lg2 == m2, lane, float(LANES)), axis=1, keepdims=True)
        e = jnp.exp(m2 - m1)
        w1 = 1.0 / (1.0 + e)
        w2 = e / (1.0 + e)
        onehot = jnp.where(lane == i1, 1.0, 0.0) + jnp.where(lane == i2, 1.0, 0.0)
        tri = jnp.where(lax.broadcasted_iota(jnp.int32, (tm, tm), 1)
                        < lax.broadcasted_iota(jnp.int32, (tm, tm), 0), 1.0, 0.0).astype(BF16)
        base = base_ref[0:1, :]
        excl = jnp.dot(tri, onehot.astype(BF16), preferred_element_type=F32) + base
        r1 = jnp.sum(jnp.where(lane == i1, excl, 0.0), axis=1, keepdims=True)
        r2 = jnp.sum(jnp.where(lane == i2, excl, 0.0), axis=1, keepdims=True)
        total = base + jnp.sum(onehot, axis=0, keepdims=True)
        base_ref[...] = jnp.broadcast_to(total, base_ref.shape)
        cnt_ref[...] = jnp.broadcast_to(total, cnt_ref.shape)
        slab = jnp.zeros((tm, LANES), F32)
        for ln, val in ((SLAB_RANK1, r1), (SLAB_RANK2, r2), (SLAB_EXP1, i1), (SLAB_EXP2, i2),
                        (SLAB_W1, w1), (SLAB_W2, w2)):
            slab = jnp.where(lane == float(ln), val, slab)
        slab_ref[...] = slab

    return pl.pallas_call(
        body,
        grid=(n_rows // tm,),
        in_specs=[
            pl.BlockSpec((tm, D_MODEL), lambda i: (i, 0)),
            pl.BlockSpec((None, 1, D_MODEL), lambda i: (layer, 0, 0)),
            pl.BlockSpec((None, 2, N_MOD, D_MODEL), lambda i: (layer, 0, 0, 0)),
            pl.BlockSpec((D_MODEL, LANES), lambda i: (0, 0)),
        ],
        out_specs=[pl.BlockSpec((tm, D_MODEL), lambda i: (i, 0)),
                   pl.BlockSpec((tm, LANES), lambda i: (i, 0)),
                   pl.BlockSpec((8, LANES), lambda i: (0, 0))],
        out_shape=[jax.ShapeDtypeStruct((n_rows, D_MODEL), F32),
                   jax.ShapeDtypeStruct((n_rows, LANES), F32),
                   jax.ShapeDtypeStruct((8, LANES), F32)],
        scratch_shapes=[pltpu.VMEM((8, LANES), F32)],
        compiler_params=_params(("arbitrary",)),
        name="router",
    )(h, norm_g, mod, wr)


def _routing_tables(slab, counts, tb, nblk):
    cnt = counts[0, :N_EXPERTS].astype(jnp.int32)
    padded = ((cnt + tb - 1) // tb) * tb
    ends = jnp.cumsum(padded)
    starts = ends - padded
    pos1 = starts[slab[:, SLAB_EXP1].astype(jnp.int32)] + slab[:, SLAB_RANK1].astype(jnp.int32)
    pos2 = starts[slab[:, SLAB_EXP2].astype(jnp.int32)] + slab[:, SLAB_RANK2].astype(jnp.int32)
    nvalid = ends[-1] // tb
    blk = jnp.minimum(jnp.arange(nblk, dtype=jnp.int32), nvalid - 1)
    blk_expert = jnp.minimum(jnp.searchsorted(ends, blk * tb, side='right'), N_EXPERTS - 1).astype(jnp.int32)
    left = cnt[blk_expert] - (blk * tb - starts[blk_expert])
    blk_rows = ((jnp.clip(left, 0, tb) + MOE_SUB - 1) // MOE_SUB) * MOE_SUB
    return pos1, pos2, blk_expert, blk_rows.astype(jnp.int32), nvalid.reshape(1).astype(jnp.int32)


def _dispatch(hn, pos1, pos2, n_slots):
    n_rows = hn.shape[0]
    tm = CTX_LEN

    def body(p1_ref, p2_ref, hn_ref, init_ref, xs_ref, sem):
        i = pl.program_id(0)

        def row_copy(t, p):
            return pltpu.make_async_copy(hn_ref.at[pl.ds(t, 1)], xs_ref.at[pl.ds(p, 1)], sem)

        def issue(t, carry):
            row_copy(t, p1_ref[i * tm + t]).start(priority=0)
            row_copy(t, p2_ref[i * tm + t]).start(priority=1)
            return carry

        def drain(t, carry):
            row_copy(0, 0).wait()
            row_copy(0, 0).wait()
            return carry

        lax.fori_loop(0, tm, issue, 0, unroll=8)
        lax.fori_loop(0, tm, drain, 0, unroll=8)

    return pl.pallas_call(
        body,
        grid_spec=pltpu.PrefetchScalarGridSpec(
            num_scalar_prefetch=2,
            grid=(n_rows // tm,),
            in_specs=[pl.BlockSpec((tm, D_MODEL), lambda i, p1, p2: (i, 0)),
                      pl.BlockSpec(memory_space=pl.ANY)],
            out_specs=pl.BlockSpec(memory_space=pl.ANY),
            scratch_shapes=[pltpu.SemaphoreType.DMA],
        ),
        out_shape=jax.ShapeDtypeStruct((n_slots, D_MODEL), F32),
        input_output_aliases={3: 0},
        compiler_params=_params(("arbitrary",)),
        name="moe_dispatch",
    )(pos1, pos2, hn, jnp.zeros((n_slots, D_MODEL), F32))


def _swiglu_chunk(a, wg_ref, wu_ref, wd_ref):
    g = jnp.dot(a, wg_ref[...].astype(BF16), preferred_element_type=F32)
    u = jnp.dot(a, wu_ref[...].astype(BF16), preferred_element_type=F32)
    return jnp.dot((_silu(g) * u).astype(BF16), wd_ref[...].astype(BF16), preferred_element_type=F32)


def _moe_ffn(xs, blk_expert, blk_rows, nvalid, w_gate, w_up, w_down, e_base, tb):
    n_slots = xs.shape[0]
    d_ff = w_gate.shape[-1]
    tf = FFN_CHUNK
    nf = d_ff // tf

    def body(be_ref, br_ref, nv_ref, x_ref, wg_ref, wu_ref, wd_ref, out_ref, a_ref):
        b = pl.program_id(0)
        f = pl.program_id(1)
        used = b < nv_ref[0]

        @pl.when(jnp.logical_and(used, f == 0))
        def _():
            a_ref[...] = x_ref[...].astype(BF16)
            out_ref[...] = jnp.zeros_like(out_ref)

        for n_rows_used in range(MOE_SUB, tb + 1, MOE_SUB):
            @pl.when(jnp.logical_and(used, br_ref[b] == n_rows_used))
            def _():
                rows = slice(0, n_rows_used)
                out_ref[rows, :] += _swiglu_chunk(a_ref[rows, :], wg_ref, wu_ref, wd_ref)

        @pl.when(jnp.logical_and(jnp.logical_not(used), f == 0))
        def _():
            out_ref[...] = jnp.zeros_like(out_ref)

    def f_idx(b, f, nv):
        return jnp.where(b < nv[0], f, nf - 1)

    once = pl.Buffered(1)
    rows = lambda b, f, be, br, nv: (jnp.minimum(b, nv[0] - 1), 0)
    return pl.pallas_call(
        body,
        grid_spec=pltpu.PrefetchScalarGridSpec(
            num_scalar_prefetch=3,
            grid=(n_slots // tb, nf),
            in_specs=[
                pl.BlockSpec((tb, D_MODEL), rows, pipeline_mode=once),
                pl.BlockSpec((None, D_MODEL, tf), lambda b, f, be, br, nv: (e_base + be[b], 0, f_idx(b, f, nv))),
                pl.BlockSpec((None, D_MODEL, tf), lambda b, f, be, br, nv: (e_base + be[b], 0, f_idx(b, f, nv))),
                pl.BlockSpec((None, tf, D_MODEL), lambda b, f, be, br, nv: (e_base + be[b], f_idx(b, f, nv), 0)),
            ],
            out_specs=pl.BlockSpec((tb, D_MODEL), lambda b, f, be, br, nv: (b, 0), pipeline_mode=once),
            scratch_shapes=[pltpu.VMEM((tb, D_MODEL), BF16)],
        ),
        out_shape=jax.ShapeDtypeStruct((n_slots, D_MODEL), F32),
        compiler_params=_params(("arbitrary", "arbitrary")),
        name="moe_ffn",
    )(blk_expert, blk_rows, nvalid, xs, w_gate, w_up, w_down)


def _combine(h, ys, slab, pos1, pos2, layer, mod):
    n_rows = h.shape[0]
    tm = CTX_LEN

    def body(p1_ref, p2_ref, h_ref, slab_ref, m_ref, ys_ref, out_ref, y1_ref, y2_ref, sem):
        i = pl.program_id(0)

        def row_copy(p, dst_ref, t):
            return pltpu.make_async_copy(ys_ref.at[pl.ds(p, 1)], dst_ref.at[pl.ds(t, 1)], sem)

        def issue(t, carry):
            row_copy(p1_ref[i * tm + t], y1_ref, t).start(priority=0)
            row_copy(p2_ref[i * tm + t], y2_ref, t).start(priority=1)
            return carry

        def drain(t, carry):
            row_copy(0, y1_ref, 0).wait()
            row_copy(0, y2_ref, 0).wait()
            return carry

        lax.fori_loop(0, tm, issue, 0, unroll=8)
        lax.fori_loop(0, tm, drain, 0, unroll=8)
        w1 = slab_ref[:, SLAB_W1:SLAB_W1 + 1]
        w2 = slab_ref[:, SLAB_W2:SLAB_W2 + 1]
        moe = w1 * y1_ref[...] + w2 * y2_ref[...]
        out_ref[...] = h_ref[...] + _mod_row(m_ref, 5, _is_ctx(i, tm)) * moe

    return pl.pallas_call(
        body,
        grid_spec=pltpu.PrefetchScalarGridSpec(
            num_scalar_prefetch=2,
            grid=(n_rows // tm,),
            in_specs=[
                pl.BlockSpec((tm, D_MODEL), lambda i, p1, p2: (i, 0)),
                pl.BlockSpec((tm, LANES), lambda i, p1, p2: (i, 0)),
                pl.BlockSpec((None, 2, N_MOD, D_MODEL), lambda i, p1, p2: (layer, 0, 0, 0)),
                pl.BlockSpec(memory_space=pl.ANY),
            ],
            out_specs=pl.BlockSpec((tm, D_MODEL), lambda i, p1, p2: (i, 0)),
            scratch_shapes=[pltpu.VMEM((tm, D_MODEL), F32), pltpu.VMEM((tm, D_MODEL), F32),
                            pltpu.SemaphoreType.DMA],
        ),
        out_shape=jax.ShapeDtypeStruct((n_rows, D_MODEL), F32),
        compiler_params=_params(("arbitrary",)),
        name="moe_combine",
    )(pos1, pos2, h, slab, mod, ys)


def _moe_layer(h, layer, k, mod, norm_g, w_router, w_gate, w_up, w_down):
    n_rows = h.shape[0]
    d_ff = w_gate.shape[-1]
    tb = 1024 if n_rows >= 4096 else 256
    nblk = -(-(TOP_K * n_rows + N_EXPERTS * (tb - 1)) // tb)
    hn, slab, counts = _router(h, layer, k, mod, norm_g, w_router)
    pos1, pos2, blk_expert, blk_rows, nvalid = _routing_tables(slab, counts, tb, nblk)
    xs = _dispatch(hn, pos1, pos2, nblk * tb)
    ys = _moe_ffn(xs, blk_expert, blk_rows, nvalid, w_gate.reshape(-1, D_MODEL, d_ff),
                  w_up.reshape(-1, D_MODEL, d_ff), w_down.reshape(-1, d_ff, D_MODEL), k * N_EXPERTS, tb)
    return _combine(h, ys, slab, pos1, pos2, layer, mod)


def _ffn(h, layer, mod, norm_g, w_gate, w_up, w_down, k):
    n_rows = h.shape[0]
    d_ff = w_gate.shape[-1]
    tm = _pick(n_rows, (768, 256))
    tf = FFN_CHUNK
    nf = d_ff // tf

    def body(h_ref, g_ref, m_ref, wg_ref, wu_ref, wd_ref, out_ref, a_ref):
        i = pl.program_id(0)
        f = pl.program_id(1)

        @pl.when(f == 0)
        def _():
            _norm_mod_into(a_ref, i, tm, h_ref, g_ref, m_ref, 3)
            out_ref[...] = jnp.zeros_like(out_ref)

        out_ref[...] += _swiglu_chunk(a_ref[...], wg_ref, wu_ref, wd_ref)

        @pl.when(f == nf - 1)
        def _():
            def chunk(r, carry):
                rows = pl.ds(pl.multiple_of(r * MOD_CHUNK, MOD_CHUNK), MOD_CHUNK)
                m = jnp.where(i * tm + r * MOD_CHUNK < CTX_LEN, 1, 0)
                out_ref[rows, :] = h_ref[rows, :] + m_ref[m, pl.ds(5, 1), :] * out_ref[rows, :]
                return carry

            lax.fori_loop(0, tm // MOD_CHUNK, chunk, 0)

    once = pl.Buffered(1)
    return pl.pallas_call(
        body,
        grid=(n_rows // tm, nf),
        in_specs=[
            pl.BlockSpec((tm, D_MODEL), lambda i, f: (i, 0), pipeline_mode=once),
            pl.BlockSpec((None, 1, D_MODEL), lambda i, f: (layer, 0, 0)),
            pl.BlockSpec((None, 2, N_MOD, D_MODEL), lambda i, f: (layer, 0, 0, 0)),
            pl.BlockSpec((None, D_MODEL, tf), lambda i, f: (k, 0, f)),
            pl.BlockSpec((None, D_MODEL, tf), lambda i, f: (k, 0, f)),
            pl.BlockSpec((None, tf, D_MODEL), lambda i, f: (k, f, 0)),
        ],
        out_specs=pl.BlockSpec((tm, D_MODEL), lambda i, f: (i, 0), pipeline_mode=once),
        out_shape=jax.ShapeDtypeStruct((n_rows, D_MODEL), F32),
        scratch_shapes=[pltpu.VMEM((tm, D_MODEL), BF16)],
        compiler_params=_params(("arbitrary", "arbitrary")),
        name="ffn",
    )(h, norm_g, mod, w_gate, w_up, w_down)


def _final_norm(h, final_g):
    n_rows = h.shape[0]
    tm = CTX_LEN

    def body(h_ref, g_ref, out_ref):
        x = h_ref[...]
        out_ref[...] = x * lax.rsqrt(jnp.mean(x * x, axis=-1, keepdims=True) + EPS) * g_ref[...]

    return pl.pallas_call(
        body,
        grid=((n_rows - CTX_LEN) // tm,),
        in_specs=[pl.BlockSpec((tm, D_MODEL), lambda i: (i + 1, 0)),
                  pl.BlockSpec((1, D_MODEL), lambda i: (0, 0))],
        out_specs=pl.BlockSpec((tm, D_MODEL), lambda i: (i, 0)),
        out_shape=jax.ShapeDtypeStruct((n_rows - CTX_LEN, D_MODEL), F32),
        compiler_params=_params(("arbitrary",)),
        name="final_norm",
    )(h, final_g.reshape(1, D_MODEL))


def _sincos_tables(rows):
    quarter = D_MODEL // 4
    omega = 1.0 / (POS_BASE ** (jnp.arange(quarter, dtype=F32) / quarter))

    def emb(pos):
        ang = pos[:, None] * omega[None, :]
        return jnp.concatenate([jnp.sin(ang), jnp.cos(ang)], axis=-1)

    return emb(jnp.arange(rows, dtype=F32)), emb(jnp.arange(GRID_W, dtype=F32))


def kernel(x, c, ctx, c_ctx, ada_w, ada_b, norm_mix_g, norm_ffn_g, final_g, s5_w_in, s5_lam_re, s5_lam_im, s5_log_dt, s5_b_re, s5_b_im, s5_c_re, s5_c_im, s5_d, s5_w_out, sg_w_in, sg_ln_g, sg_ln_b, sg_w_s, sg_b_s, sg_w_out, cv_w_pw1, cv_dw_w, cv_dw_b, cv_ln_g, cv_ln_b, cv_w_pw2, ff_w_gate, ff_w_up, ff_w_down, moe_w_router, moe_w_gate, moe_w_up, moe_w_down):
    bsz, seq, dim = x.shape
    assert bsz == 1 and dim == D_MODEL and ctx.shape == (1, CTX_LEN, D_MODEL)
    p = dict(s5_w_in=s5_w_in, s5_lam_re=s5_lam_re, s5_lam_im=s5_lam_im, s5_log_dt=s5_log_dt,
             s5_b_re=s5_b_re, s5_b_im=s5_b_im, s5_c_re=s5_c_re, s5_c_im=s5_c_im, s5_d=s5_d,
             s5_w_out=s5_w_out, sg_w_in=sg_w_in, sg_ln_g=sg_ln_g, sg_ln_b=sg_ln_b, sg_w_s=sg_w_s,
             sg_b_s=sg_b_s, sg_w_out=sg_w_out, cv_w_pw1=cv_w_pw1, cv_dw_w=cv_dw_w, cv_dw_b=cv_dw_b,
             cv_ln_g=cv_ln_g, cv_ln_b=cv_ln_b, cv_w_pw2=cv_w_pw2)
    row_tab, col_tab = _sincos_tables(seq // GRID_W)
    h = _embed(x[0], ctx[0], row_tab, col_tab)
    cond8 = jnp.concatenate([c, c_ctx[None, :], jnp.zeros((6, D_MODEL), F32)], axis=0)
    mod = _modulation(cond8, ada_w, ada_b)[:, :2, :].reshape(DEPTH, 2, N_MOD, D_MODEL)
    g_mix = norm_mix_g.reshape(DEPTH, 1, D_MODEL)
    g_ffn = norm_ffn_g.reshape(DEPTH, 1, D_MODEL)
    for layer in range(DEPTH):
        kind, j = layer % N_MIXERS, layer // N_MIXERS
        if kind == 0:
            h = _s5_layer(h, layer, j, mod, g_mix, p)
        elif kind == 1:
            h = _sg_layer(h, layer, j, mod, g_mix, p)
        else:
            h = _conv_layer(h, layer, j, mod, g_mix, p)
        k = layer // 2
        if layer % 2 == 0:
            h = _ffn(h, layer, mod, g_ffn, ff_w_gate, ff_w_up, ff_w_down, k)
        else:
            h = _moe_layer(h, layer, k, mod, g_ffn, moe_w_router, moe_w_gate, moe_w_up, moe_w_down)
    return _final_norm(h, final_g)[None]
```
